```python
import jax, jax.numpy as jnp
from jax import lax
import numpy as np

D_MODEL = 1024
BATCH = 8
SEQ = 4096
DEPTH = 1

N_META = 16
CONV_DIM = 512
CONV_WIDTH = 31
N_HEADS = 8
QK_NOPE_DIM = 64
QK_ROPE_DIM = 32
QK_HEAD_DIM = QK_NOPE_DIM + QK_ROPE_DIM
V_HEAD_DIM = 64
Q_LORA_RANK = 256
KV_LORA_RANK = 128
ROPE_THETA = 10000.0
Q_BLOCK = 128
N_BRANCHES = 2
N_EXPERTS = 32
TOP_K = 4
D_FF = 1024
SWIGLU_LIMIT = 7.0
SWIGLU_ALPHA = 1.702
MOE_BLOCK = 256
NORM_EPS = 1e-6
NEG_INF = -1e30

COLS_CONV = 2 * CONV_DIM
COLS_Q = Q_LORA_RANK
COLS_KV = KV_LORA_RANK + QK_ROPE_DIM
COLS_GATE = N_BRANCHES * D_MODEL
IN_COLS = COLS_CONV + COLS_Q + COLS_KV + COLS_GATE
SPLITS = (COLS_CONV, COLS_CONV + COLS_Q, COLS_CONV + COLS_Q + COLS_KV)

kernel_name = 'hybrid_meta_conv_mla_moe_layer'


def rms_norm(x, w):
    xf = x.astype(jnp.float32)
    y = xf * lax.rsqrt(jnp.mean(xf * xf, axis=-1, keepdims=True) + NORM_EPS)
    return (y * w.astype(jnp.float32)).astype(x.dtype)


def layer_norm(x, w, b):
    xf = x.astype(jnp.float32)
    mu = jnp.mean(xf, axis=-1, keepdims=True)
    xc = xf - mu
    y = xc * lax.rsqrt(jnp.mean(xc * xc, axis=-1, keepdims=True) + NORM_EPS)
    return (y * w.astype(jnp.float32) + b.astype(jnp.float32)).astype(x.dtype)


def apply_rope(x, pos):
    half = QK_ROPE_DIM // 2
    inv_freq = ROPE_THETA ** (-jnp.arange(half, dtype=jnp.float32) / half)
    ang = pos.astype(jnp.float32)[:, None] * inv_freq[None, :]
    cos = jnp.cos(ang)[:, None, :]
    sin = jnp.sin(ang)[:, None, :]
    xf = x.astype(jnp.float32)
    x1, x2 = xf[..., :half], xf[..., half:]
    return jnp.concatenate([x1 * cos - x2 * sin, x2 * cos + x1 * sin], axis=-1).astype(x.dtype)


def conformer_conv(u, dw_w, dw_b, ln_w, ln_b, w_pw2):
    a, g = u[..., :CONV_DIM], u[..., CONV_DIM:]
    h = a * jax.nn.sigmoid(g)
    h = lax.conv_general_dilated(
        h, dw_w[:, None, :].astype(h.dtype), window_strides=(1,),
        padding=[(CONV_WIDTH - 1, 0)], dimension_numbers=('NWC', 'WIO', 'NWC'),
        feature_group_count=CONV_DIM) + dw_b
    h = jax.nn.silu(layer_norm(h, ln_w, ln_b))
    return h @ w_pw2


def _attend(qb, qpos, k, v):
    s = jnp.einsum('bhqd,bhkd->bhqk', qb, k).astype(jnp.float32) * (QK_HEAD_DIM ** -0.5)
    kpos = jnp.arange(k.shape[2])
    s = jnp.where(kpos[None, :] <= qpos[:, None], s, NEG_INF)
    p = jax.nn.softmax(s, axis=-1).astype(v.dtype)
    return jnp.einsum('bhqk,bhkd->bhqd', p, v)


def causal_block_attention(q, k, v):
    B, H, L, dh = q.shape
    n_real = L - N_META
    nb = n_real // Q_BLOCK
    o_meta = _attend(q[:, :, :N_META], jnp.arange(N_META), k[:, :, :N_META], v[:, :, :N_META])
    q_blocks = q[:, :, N_META:].reshape(B, H, nb, Q_BLOCK, dh).transpose(2, 0, 1, 3, 4)
    pos_blocks = (N_META + jnp.arange(n_real)).reshape(nb, Q_BLOCK)
    o_real = lax.map(lambda a: _attend(a[0], a[1], k, v), (q_blocks, pos_blocks))
    o_real = o_real.transpose(1, 2, 0, 3, 4).reshape(B, H, n_real, V_HEAD_DIM)
    return jnp.concatenate([o_meta, o_real], axis=2)


def mla(q_lat, kv_lat, pos, q_a_norm_w, w_q_b, kv_a_norm_w, w_kv_b, q_norm_w, k_norm_w, w_o):
    B, L, _ = q_lat.shape
    q = (rms_norm(q_lat, q_a_norm_w) @ w_q_b).reshape(B, L, N_HEADS, QK_HEAD_DIM)
    c_kv, k_pe = kv_lat[..., :KV_LORA_RANK], kv_lat[..., KV_LORA_RANK:]
    kv = (rms_norm(c_kv, kv_a_norm_w) @ w_kv_b).reshape(B, L, N_HEADS, QK_NOPE_DIM + V_HEAD_DIM)
    k_nope, v = kv[..., :QK_NOPE_DIM], kv[..., QK_NOPE_DIM:]
    k_pe = jnp.broadcast_to(k_pe[:, :, None, :], (B, L, N_HEADS, QK_ROPE_DIM))
    k = jnp.concatenate([k_nope, k_pe], axis=-1)
    q = rms_norm(q, q_norm_w)
    k = rms_norm(k, k_norm_w)
    q = jnp.concatenate([q[..., :QK_NOPE_DIM], apply_rope(q[..., QK_NOPE_DIM:], pos)], axis=-1)
    k = jnp.concatenate([k[..., :QK_NOPE_DIM], apply_rope(k[..., QK_NOPE_DIM:], pos)], axis=-1)
    o = causal_block_attention(q.transpose(0, 2, 1, 3), k.transpose(0, 2, 1, 3), v.transpose(0, 2, 1, 3))
    o = o.transpose(0, 2, 1, 3).reshape(B, L, N_HEADS * V_HEAD_DIM)
    return o @ w_o


def moe(h, router_w, router_b, w_gate_up, b_gate_up, w_down, b_down):
    B, L, D = h.shape
    xt = h.reshape(-1, D)
    N = xt.shape[0]
    S = N * TOP_K
    logits = (xt @ router_w + router_b).astype(jnp.float32)
    top_val, top_idx = lax.top_k(logits, TOP_K)
    gates = jax.nn.softmax(top_val, axis=-1).astype(h.dtype)
    flat_e = top_idx.reshape(-1)
    order = jnp.argsort(flat_e)
    sorted_e = flat_e[order]
    counts = jnp.bincount(flat_e, length=N_EXPERTS)
    group_start = jnp.cumsum(counts) - counts
    padded = (counts + MOE_BLOCK - 1) // MOE_BLOCK * MOE_BLOCK
    padded_end = jnp.cumsum(padded)
    padded_start = padded_end - padded
    dest = (padded_start[sorted_e] + jnp.arange(S) - group_start[sorted_e]).astype(jnp.int32)
    n_blocks = S // MOE_BLOCK + N_EXPERTS + 1
    P = n_blocks * MOE_BLOCK
    row_token = jnp.zeros((P,), jnp.int32).at[dest].set((order // TOP_K).astype(jnp.int32))
    row_w = jnp.zeros((P,), h.dtype).at[dest].set(gates.reshape(-1)[order])
    block_expert = jnp.minimum(
        jnp.searchsorted(padded_end, jnp.arange(n_blocks) * MOE_BLOCK, side='right'), N_EXPERTS - 1)
    xs = xt[row_token].reshape(n_blocks, MOE_BLOCK, D)
    ws = row_w.reshape(n_blocks, MOE_BLOCK)

    def expert_block(args):
        xb, wb, e = args
        gu = xb @ w_gate_up[e] + b_gate_up[e]
        g, u = gu[:, :D_FF], gu[:, D_FF:]
        g = jnp.minimum(g, SWIGLU_LIMIT)
        u = jnp.clip(u, -SWIGLU_LIMIT, SWIGLU_LIMIT)
        act = (u + 1) * (g * jax.nn.sigmoid(SWIGLU_ALPHA * g))
        return (act @ w_down[e] + b_down[e]) * wb[:, None]

    ys = lax.map(expert_block, (xs, ws, block_expert)).reshape(P, D)
    y = jax.ops.segment_sum(ys, row_token, num_segments=N)
    return y.reshape(B, L, D)


def setup_inputs(seed: int = 0) -> dict:
    key = jax.random.key(seed)
    ks = jax.random.split(key, 32)
    f32 = jnp.float32

    def w(k, shape, fan_in):
        return jax.random.normal(k, shape, f32) * (fan_in ** -0.5)

    def gain(k, shape):
        return 1.0 + 0.02 * jax.random.normal(k, shape, f32)

    def bias(k, shape, s=0.02):
        return s * jax.random.normal(k, shape, f32)

    Dp = DEPTH
    return {
        'x': jax.random.normal(ks[0], (BATCH, SEQ, D_MODEL), f32),
        'meta_tokens': jax.random.normal(ks[1], (N_META, D_MODEL), f32),
        'attn_norm_w': gain(ks[2], (Dp, D_MODEL)),
        'w_in': w(ks[3], (Dp, D_MODEL, IN_COLS), D_MODEL),
        'conv_dw_w': w(ks[4], (Dp, CONV_WIDTH, CONV_DIM), CONV_WIDTH),
        'conv_dw_b': bias(ks[5], (Dp, CONV_DIM)),
        'conv_ln_w': gain(ks[6], (Dp, CONV_DIM)),
        'conv_ln_b': bias(ks[7], (Dp, CONV_DIM)),
        'conv_pw2_w': w(ks[8], (Dp, CONV_DIM, D_MODEL), CONV_DIM),
        'q_a_norm_w': gain(ks[9], (Dp, Q_LORA_RANK)),
        'w_q_b': w(ks[10], (Dp, Q_LORA_RANK, N_HEADS * QK_HEAD_DIM), Q_LORA_RANK),
        'kv_a_norm_w': gain(ks[11], (Dp, KV_LORA_RANK)),
        'w_kv_b': w(ks[12], (Dp, KV_LORA_RANK, N_HEADS * (QK_NOPE_DIM + V_HEAD_DIM)), KV_LORA_RANK),
        'q_norm_w': gain(ks[13], (Dp, QK_HEAD_DIM)),
        'k_norm_w': gain(ks[14], (Dp, QK_HEAD_DIM)),
        'w_o_mla': w(ks[15], (Dp, N_HEADS * V_HEAD_DIM, D_MODEL), N_HEADS * V_HEAD_DIM),
        'gate_b': bias(ks[16], (Dp, COLS_GATE)),
        'w_out': w(ks[17], (Dp, D_MODEL, D_MODEL), D_MODEL),
        'ffn_norm_w': gain(ks[18], (Dp, D_MODEL)),
        'router_w': w(ks[19], (Dp, D_MODEL, N_EXPERTS), D_MODEL),
        'router_b': bias(ks[20], (Dp, N_EXPERTS), 0.01),
        'w_gate_up': w(ks[21], (Dp, N_EXPERTS, D_MODEL, 2 * D_FF), D_MODEL),
        'b_gate_up': bias(ks[22], (Dp, N_EXPERTS, 2 * D_FF)),
        'w_down': w(ks[23], (Dp, N_EXPERTS, D_FF, D_MODEL), D_FF),
        'b_down': bias(ks[24], (Dp, N_EXPERTS, D_MODEL)),
    }


def reference(x, meta_tokens, attn_norm_w, w_in, conv_dw_w, conv_dw_b, conv_ln_w, conv_ln_b,
              conv_pw2_w, q_a_norm_w, w_q_b, kv_a_norm_w, w_kv_b, q_norm_w, k_norm_w, w_o_mla,
              gate_b, w_out, ffn_norm_w, router_w, router_b, w_gate_up, b_gate_up, w_down, b_down):
    B = x.shape[0]
    meta = jnp.broadcast_to(meta_tokens[None].astype(x.dtype), (B, N_META, D_MODEL))
    h = jnp.concatenate([meta, x], axis=1)
    L = h.shape[1]
    pos = jnp.arange(L)
    for l in range(DEPTH):
        hn = rms_norm(h, attn_norm_w[l])
        proj = hn @ w_in[l]
        u_conv, q_lat, kv_lat, g_lin = jnp.split(proj, SPLITS, axis=-1)
        y_conv = conformer_conv(u_conv, conv_dw_w[l], conv_dw_b[l], conv_ln_w[l], conv_ln_b[l], conv_pw2_w[l])
        y_mla = mla(q_lat, kv_lat, pos, q_a_norm_w[l], w_q_b[l], kv_a_norm_w[l], w_kv_b[l],
                    q_norm_w[l], k_norm_w[l], w_o_mla[l])
        g = jax.nn.sigmoid(g_lin + gate_b[l]).reshape(B, L, N_BRANCHES, D_MODEL)
        mixed = g[:, :, 0] * y_conv + g[:, :, 1] * y_mla
        h = h + mixed @ w_out[l]
        h = h + moe(rms_norm(h, ffn_norm_w[l]), router_w[l], router_b[l], w_gate_up[l], b_gate_up[l],
                    w_down[l], b_down[l])
    return h[:, N_META:]
```

```python
import functools
import math

import jax
import jax.numpy as jnp
from jax import lax
from jax.experimental import pallas as pl
from jax.experimental.pallas import tpu as pltpu

N_META = 16
CONV_DIM = 512
CONV_WIDTH = 31
N_HEADS = 8
QK_NOPE_DIM = 64
QK_ROPE_DIM = 32
QK_HEAD_DIM = QK_NOPE_DIM + QK_ROPE_DIM
V_HEAD_DIM = 64
Q_LORA_RANK = 256
KV_LORA_RANK = 128
ROPE_THETA = 10000.0
N_EXPERTS = 32
TOP_K = 4
D_FF = 1024
SWIGLU_LIMIT = 7.0
SWIGLU_ALPHA = 1.702
NORM_EPS = 1e-6
NEG_INF = -1e30

LANES = 128
HALO = 32
VMEM_LIMIT = 56 * 1024 * 1024

F32 = jnp.float32
BF16 = jnp.bfloat16


def _cparams(sem):
    return pltpu.CompilerParams(dimension_semantics=sem, vmem_limit_bytes=VMEM_LIMIT)


def _dot(a, b):
    return jnp.dot(a, b, preferred_element_type=F32)


def _dot_t(a, b):
    return lax.dot_general(a, b, (((1,), (1,)), ((), ())), preferred_element_type=F32)


def _sigmoid(x):
    return 1.0 / (1.0 + jnp.exp(-x))


def _rms(x, w):
    return x * lax.rsqrt(jnp.mean(x * x, axis=-1, keepdims=True) + NORM_EPS) * w


C_CONV = 2 * CONV_DIM
C_LAT = Q_LORA_RANK + KV_LORA_RANK + 2 * LANES
C_GATE = 2 * 1024


def _proj_body(x_ref, anw_ref, win_ref, qaw_ref, wq_ref, kvaw_ref, wkv_ref, qlane_ref, klane_ref,
               cos_ref, sin_ref, gb_ref, glu_ref, q_ref, k_ref, v_ref, g_ref, *, q_scale):
    x = x_ref[...]
    xn = _rms(x, anw_ref[...]).astype(BF16)

    u = _dot(xn, win_ref[:, 0:C_CONV])
    glu_ref[...] = u[:, :CONV_DIM] * _sigmoid(u[:, CONV_DIM:])

    gl = _dot(xn, win_ref[:, C_CONV + C_LAT:]) + gb_ref[...]
    g_ref[...] = _sigmoid(gl).astype(g_ref.dtype)

    lat = _dot(xn, win_ref[:, C_CONV:C_CONV + C_LAT])
    q_lat = lat[:, :Q_LORA_RANK]
    c_kv = lat[:, Q_LORA_RANK:Q_LORA_RANK + KV_LORA_RANK]
    kpe = lat[:, Q_LORA_RANK + KV_LORA_RANK:Q_LORA_RANK + KV_LORA_RANK + LANES]
    kpe_rot = lat[:, Q_LORA_RANK + KV_LORA_RANK + LANES:]

    cos = cos_ref[...]
    sin = sin_ref[...]

    qn = _rms(q_lat, qaw_ref[...]).astype(BF16)
    qq = _dot(qn, wq_ref[...])
    q_cos = qlane_ref[...] * cos
    for h in range(N_HEADS):
        q1 = qq[:, h * LANES:(h + 1) * LANES]
        q2 = qq[:, (N_HEADS + h) * LANES:(N_HEADS + h + 1) * LANES]
        ss = jnp.sum(q1 * q1, axis=-1, keepdims=True)
        s = lax.rsqrt(ss * (1.0 / QK_HEAD_DIM) + NORM_EPS) * q_scale
        q_ref[:, h * LANES:(h + 1) * LANES] = (s * (q1 * q_cos + q2 * sin)).astype(q_ref.dtype)

    cn = _rms(c_kv, kvaw_ref[...]).astype(BF16)
    kv = _dot(cn, wkv_ref[...])
    v_ref[...] = kv[:, N_HEADS * LANES:].astype(v_ref.dtype)
    k_cos = klane_ref[...] * cos
    ss_pe = jnp.sum(kpe * kpe, axis=-1, keepdims=True)
    k_pe_roped = kpe * k_cos + kpe_rot * sin
    for h in range(N_HEADS):
        kn = kv[:, h * LANES:(h + 1) * LANES]
        ss = jnp.sum(kn * kn, axis=-1, keepdims=True) + ss_pe
        s = lax.rsqrt(ss * (1.0 / QK_HEAD_DIM) + NORM_EPS)
        k_ref[:, h * LANES:(h + 1) * LANES] = (s * (kn * k_cos + k_pe_roped)).astype(k_ref.dtype)


def _proj_call(x2, tables, weights, *, tm, tiles_per_seq):
    n = x2.shape[0]
    d = x2.shape[1]
    cos_t, sin_t = tables
    (anw, win, qaw, wq, kvaw, wkv, qlane, klane, gb) = weights
    full = lambda a: pl.BlockSpec(a.shape, lambda i: (0,) * a.ndim)
    row = lambda c: pl.BlockSpec((tm, c), lambda i: (i, 0))
    pos = pl.BlockSpec((tm, LANES), lambda i: (i % tiles_per_seq, 0))
    q_scale = (QK_HEAD_DIM ** -0.5) * math.log2(math.e)
    return pl.pallas_call(
        functools.partial(_proj_body, q_scale=q_scale),
        grid=(n // tm,),
        in_specs=[row(d), full(anw), full(win), full(qaw), full(wq), full(kvaw), full(wkv),
                  full(qlane), full(klane), pos, pos, full(gb)],
        out_specs=[row(CONV_DIM), row(N_HEADS * LANES), row(N_HEADS * LANES),
                   row(N_HEADS * V_HEAD_DIM), row(C_GATE)],
        out_shape=[jax.ShapeDtypeStruct((n, CONV_DIM), F32),
                   jax.ShapeDtypeStruct((n, N_HEADS * LANES), BF16),
                   jax.ShapeDtypeStruct((n, N_HEADS * LANES), BF16),
                   jax.ShapeDtypeStruct((n, N_HEADS * V_HEAD_DIM), BF16),
                   jax.ShapeDtypeStruct((n, C_GATE), BF16)],
        compiler_params=_cparams(("parallel",)),
        name="proj",
    )(x2, anw, win, qaw, wq, kvaw, wkv, qlane, klane, cos_t, sin_t, gb)


def _attn_body(q_ref, k_ref, v_ref, km_ref, vm_ref, o_ref, m_ref, l_ref, acc_ref, *, tq):
    i = pl.program_id(1)
    j = pl.program_id(2)

    def head_slices(h):
        return slice(h * LANES, (h + 1) * LANES), slice(h * V_HEAD_DIM, (h + 1) * V_HEAD_DIM)

    @pl.when(j == 0)
    def _meta():
        for h in range(N_HEADS):
            ks, vs = head_slices(h)
            s = _dot_t(q_ref[0, :, ks], km_ref[:, ks])
            m = jnp.max(s, axis=-1, keepdims=True)
            p = jnp.exp2(s - m)
            m_ref[h] = m
            l_ref[h] = jnp.sum(p, axis=-1, keepdims=True)
            acc_ref[h] = _dot(p.astype(BF16), vm_ref[:, vs])

    def step(diagonal):
        for h in range(N_HEADS):
            ks, vs = head_slices(h)
            s = _dot_t(q_ref[0, :, ks], k_ref[0, :, ks])
            if diagonal:
                r = lax.broadcasted_iota(jnp.int32, s.shape, 0)
                c = lax.broadcasted_iota(jnp.int32, s.shape, 1)
                s = jnp.where(c <= r, s, NEG_INF)
            m_prev = m_ref[h]
            m_new = jnp.maximum(m_prev, jnp.max(s, axis=-1, keepdims=True))
            alpha = jnp.exp2(m_prev - m_new)
            p = jnp.exp2(s - m_new)
            l_new = alpha * l_ref[h] + jnp.sum(p, axis=-1, keepdims=True)
            acc = alpha * acc_ref[h] + _dot(p.astype(BF16), v_ref[0, :, vs])
            if diagonal:
                o_ref[0, :, vs] = (acc / l_new).astype(o_ref.dtype)
            else:
                m_ref[h] = m_new
                l_ref[h] = l_new
                acc_ref[h] = acc

    @pl.when(j < i)
    def _full():
        step(False)

    @pl.when(j == i)
    def _diag():
        step(True)


def _attn_call(q, k, v, km, vm, *, tq):
    b, s, _ = q.shape
    nq = s // tq
    kv_map = lambda bi, i, j: (bi, jnp.minimum(i, j), 0)
    return pl.pallas_call(
        functools.partial(_attn_body, tq=tq),
        grid=(b, nq, nq),
        in_specs=[pl.BlockSpec((1, tq, N_HEADS * LANES), lambda bi, i, j: (bi, i, 0)),
                  pl.BlockSpec((1, tq, N_HEADS * LANES), kv_map),
                  pl.BlockSpec((1, tq, N_HEADS * V_HEAD_DIM), kv_map),
                  pl.BlockSpec(km.shape, lambda bi, i, j: (0, 0)),
                  pl.BlockSpec(vm.shape, lambda bi, i, j: (0, 0))],
        out_specs=pl.BlockSpec((1, tq, N_HEADS * V_HEAD_DIM), lambda bi, i, j: (bi, i, 0)),
        out_shape=jax.ShapeDtypeStruct((b, s, N_HEADS * V_HEAD_DIM), BF16),
        scratch_shapes=[pltpu.VMEM((N_HEADS, tq, 1), F32), pltpu.VMEM((N_HEADS, tq, 1), F32),
                        pltpu.VMEM((N_HEADS, tq, V_HEAD_DIM), F32)],
        compiler_params=_cparams(("parallel", "parallel", "arbitrary")),
        name="attn",
    )(q, k, v, km, vm)


def _mix_body(glu_ref, halo_ref, glum_ref, o_ref, g_ref, x_ref, dww_ref, dwb_ref, lnw_ref, lnb_ref,
              pw2_ref, wo_ref, wout_ref, fnw_ref, h_ref, hn_ref, xpad_ref, *, tm, tiles_per_seq):
    i = pl.program_id(0)
    first = (i % tiles_per_seq) == 0
    meta_ctx = jnp.concatenate([jnp.zeros((HALO - N_META, CONV_DIM), F32), glum_ref[...]], axis=0)
    xpad_ref[0:HALO, :] = jnp.where(first, meta_ctx, halo_ref[...])
    xpad_ref[HALO:HALO + tm, :] = glu_ref[...]

    off = HALO - (CONV_WIDTH - 1)
    acc = jnp.zeros((tm, CONV_DIM), F32) + dwb_ref[...]
    for t in range(CONV_WIDTH):
        acc = acc + dww_ref[t:t + 1, :] * xpad_ref[off + t:off + t + tm, :]

    mu = jnp.mean(acc, axis=-1, keepdims=True)
    xc = acc - mu
    y = xc * lax.rsqrt(jnp.mean(xc * xc, axis=-1, keepdims=True) + NORM_EPS) * lnw_ref[...] + lnb_ref[...]
    y = y * _sigmoid(y)
    y_conv = _dot(y.astype(BF16), pw2_ref[...])
    y_mla = _dot(o_ref[...], wo_ref[...])
    g = g_ref[...].astype(F32)
    mixed = g[:, :1024] * y_conv + g[:, 1024:] * y_mla
    h = x_ref[...] + _dot(mixed.astype(BF16), wout_ref[...])
    h_ref[...] = h
    hn_ref[...] = _rms(h, fnw_ref[...])


def _mix_call(glu, glum, o, g, x2, weights, *, tm, tiles_per_seq):
    n, d = x2.shape
    full = lambda a: pl.BlockSpec(a.shape, lambda i: (0,) * a.ndim)
    row = lambda c: pl.BlockSpec((tm, c), lambda i: (i, 0))
    halo = pl.BlockSpec((HALO, CONV_DIM), lambda i: (jnp.maximum(i * (tm // HALO) - 1, 0), 0))
    return pl.pallas_call(
        functools.partial(_mix_body, tm=tm, tiles_per_seq=tiles_per_seq),
        grid=(n // tm,),
        in_specs=[row(CONV_DIM), halo, full(glum), row(N_HEADS * V_HEAD_DIM), row(C_GATE), row(d)]
                 + [full(w) for w in weights],
        out_specs=[row(d), row(d)],
        out_shape=[jax.ShapeDtypeStruct((n, d), F32), jax.ShapeDtypeStruct((n, d), F32)],
        scratch_shapes=[pltpu.VMEM((HALO + tm, CONV_DIM), F32)],
        compiler_params=_cparams(("parallel",)),
        name="mix",
    )(glu, glu, glum, o, g, x2, *weights)


def _route_body(hn_ref, whi_ref, wlo_ref, rb_ref, idx_ref, gate_ref):
    x = hn_ref[...]
    x_hi = x.astype(BF16)
    x_lo = (x - x_hi.astype(F32)).astype(BF16)
    w_hi = whi_ref[...]
    logits = _dot_t(w_hi, x_hi) + _dot_t(w_hi, x_lo) + _dot_t(wlo_ref[...], x_hi) + rb_ref[...]

    e_iota = lax.broadcasted_iota(jnp.int32, logits.shape, 0).astype(F32)
    vals, idxs = [], []
    cur = logits
    for _ in range(TOP_K):
        m = jnp.max(cur, axis=0, keepdims=True)
        idx = jnp.min(jnp.where(cur == m, e_iota, float(N_EXPERTS)), axis=0, keepdims=True)
        vals.append(m)
        idxs.append(idx)
        cur = jnp.where(e_iota == idx, -jnp.inf, cur)
    exps = [jnp.exp(v - vals[0]) for v in vals]
    denom = exps[0] + exps[1] + exps[2] + exps[3]
    idx_ref[...] = jnp.concatenate(idxs, axis=0).astype(jnp.int32)
    gate_ref[...] = jnp.concatenate([e / denom for e in exps] + [jnp.zeros_like(denom)] * 4, axis=0)


def _route_call(hn, w_hi, w_lo, rb, *, tm):
    n, d = hn.shape
    full = lambda a: pl.BlockSpec(a.shape, lambda i: (0,) * a.ndim)
    return pl.pallas_call(
        _route_body,
        grid=(n // tm,),
        in_specs=[pl.BlockSpec((tm, d), lambda i: (i, 0)), full(w_hi), full(w_lo), full(rb)],
        out_specs=[pl.BlockSpec((TOP_K, tm), lambda i: (0, i)), pl.BlockSpec((2 * TOP_K, tm), lambda i: (0, i))],
        out_shape=[jax.ShapeDtypeStruct((TOP_K, n), jnp.int32), jax.ShapeDtypeStruct((2 * TOP_K, n), F32)],
        compiler_params=_cparams(("parallel",)),
        name="route",
    )(hn, w_hi, w_lo, rb)


def _rank_body(idx_ref, rank_ref, cnt_ref, carry_ref, *, t):
    i = pl.program_id(0)

    @pl.when(i == 0)
    def _init():
        carry_ref[...] = jnp.zeros_like(carry_ref)

    idx = idx_ref[...]
    e_iota = lax.broadcasted_iota(jnp.int32, (N_EXPERTS, t), 0)
    hits = [e_iota == idx[k:k + 1, :] for k in range(TOP_K)]
    onehot = jnp.zeros((N_EXPERTS, t), F32)
    for hk in hits:
        onehot = onehot + jnp.where(hk, 1.0, 0.0)
    r = lax.broadcasted_iota(jnp.int32, (t, t), 0)
    c = lax.broadcasted_iota(jnp.int32, (t, t), 1)
    before = jnp.where(r < c, 1.0, 0.0).astype(BF16)
    val = _dot(onehot.astype(BF16), before) + carry_ref[...]
    ranks = [jnp.sum(jnp.where(hk, val, 0.0), axis=0, keepdims=True) for hk in hits]
    rank_ref[...] = jnp.concatenate(ranks, axis=0).astype(jnp.int32)
    total = carry_ref[...] + jnp.sum(onehot, axis=1, keepdims=True)
    carry_ref[...] = total
    cnt_ref[...] = jnp.broadcast_to(total, cnt_ref.shape).astype(jnp.int32)


def _rank_call(idx, *, t):
    n = idx.shape[1]
    return pl.pallas_call(
        functools.partial(_rank_body, t=t),
        grid=(n // t,),
        in_specs=[pl.BlockSpec((TOP_K, t), lambda i: (0, i))],
        out_specs=[pl.BlockSpec((TOP_K, t), lambda i: (0, i)), pl.BlockSpec((N_EXPERTS, LANES), lambda i: (0, 0))],
        out_shape=[jax.ShapeDtypeStruct((TOP_K, n), jnp.int32), jax.ShapeDtypeStruct((N_EXPERTS, LANES), jnp.int32)],
        scratch_shapes=[pltpu.VMEM((N_EXPERTS, 1), F32)],
        compiler_params=_cparams(("arbitrary",)),
        name="rank",
    )(idx)


def _plan_body(cnt_ref, start_ref, bexp_ref, nact_ref, *, bm, n_blocks):
    def per_expert(e, carry):
        blk, last = carry
        c = cnt_ref[e]
        nb = (c + (bm - 1)) // bm
        start_ref[e] = blk * bm

        def fill(b, _):
            bexp_ref[b] = e
            return 0

        lax.fori_loop(blk, blk + nb, fill, 0)
        return blk + nb, jnp.where(nb > 0, e, last)

    n_act, last = lax.fori_loop(0, N_EXPERTS, per_expert, (jnp.int32(0), jnp.int32(0)))
    nact_ref[0] = n_act

    def tail(b, _):
        bexp_ref[b] = last
        return 0

    lax.fori_loop(n_act, n_blocks, tail, 0)


def _plan_call(cnt, *, bm, n_blocks):
    smem = lambda: pl.BlockSpec(memory_space=pltpu.SMEM)
    return pl.pallas_call(
        functools.partial(_plan_body, bm=bm, n_blocks=n_blocks),
        in_specs=[smem()],
        out_specs=[smem(), smem(), smem()],
        out_shape=[jax.ShapeDtypeStruct((N_EXPERTS,), jnp.int32), jax.ShapeDtypeStruct((n_blocks,), jnp.int32),
                   jax.ShapeDtypeStruct((1,), jnp.int32)],
        name="plan",
    )(cnt)


def _dest_body(start_ref, idx_ref, rank_ref, dest_ref):
    idx = idx_ref[...]
    base = jnp.zeros(idx.shape, jnp.int32)
    for e in range(N_EXPERTS):
        base = jnp.where(idx == e, start_ref[e], base)
    dest_ref[...] = base + rank_ref[...]


def _dest_call(start, idx, rank, *, t):
    n = idx.shape[1]
    blk = pl.BlockSpec((TOP_K, t), lambda i, s: (0, i))
    return pl.pallas_call(
        _dest_body,
        grid_spec=pltpu.PrefetchScalarGridSpec(num_scalar_prefetch=1, grid=(n // t,), in_specs=[blk, blk],
                                               out_specs=blk),
        out_shape=jax.ShapeDtypeStruct((TOP_K, n), jnp.int32),
        compiler_params=_cparams(("parallel",)),
        name="dest",
    )(start, idx, rank)


def _row_copy(src, src_row, dst, dst_row, sem):
    return pltpu.make_async_copy(src.at[pl.ds(src_row, 1)], dst.at[pl.ds(dst_row, 1)], sem)


def _scatter_body(dest_ref, x_ref, xs_ref, sem, *, t):
    def issue(r, _):
        for k in range(TOP_K):
            _row_copy(x_ref, r, xs_ref, dest_ref[k, r], sem).start()
        return 0

    lax.fori_loop(0, t, issue, 0)

    def drain(r, _):
        for k in range(TOP_K):
            _row_copy(x_ref, r, xs_ref, dest_ref[k, r], sem).wait()
        return 0

    lax.fori_loop(0, t, drain, 0)


def _scatter_call(dest, x, *, t, p_rows):
    n, d = x.shape
    return pl.pallas_call(
        functools.partial(_scatter_body, t=t),
        grid=(n // t,),
        in_specs=[pl.BlockSpec((TOP_K, t), lambda i: (0, i), memory_space=pltpu.SMEM),
                  pl.BlockSpec((t, d), lambda i: (i, 0))],
        out_specs=pl.BlockSpec(memory_space=pl.ANY),
        out_shape=jax.ShapeDtypeStruct((p_rows, d), F32),
        scratch_shapes=[pltpu.SemaphoreType.DMA],
        compiler_params=_cparams(("arbitrary",)),
        name="scatter",
    )(dest, x)


def _expert_body(bexp_ref, nact_ref, xs_ref, wgu_ref, bgu_ref, wd_ref, bd_ref, ys_ref):
    b = pl.program_id(0)

    @pl.when(b < nact_ref[0])
    def _():
        x = xs_ref[...].astype(BF16)
        gu = _dot(x, wgu_ref[0]) + bgu_ref[0]
        g = jnp.minimum(gu[:, :D_FF], SWIGLU_LIMIT)
        u = jnp.clip(gu[:, D_FF:], -SWIGLU_LIMIT, SWIGLU_LIMIT)
        act = (u + 1.0) * (g * _sigmoid(SWIGLU_ALPHA * g))
        ys_ref[...] = _dot(act.astype(BF16), wd_ref[0]) + bd_ref[0]


def _expert_call(bexp, nact, xs, wgu, bgu, wd, bd, *, bm):
    p_rows, d = xs.shape
    n_blocks = p_rows // bm
    rows = lambda b, be, na: (jnp.minimum(b, na[0] - 1), 0)
    wsel = lambda b, be, na: (be[b], 0, 0)
    return pl.pallas_call(
        _expert_body,
        grid_spec=pltpu.PrefetchScalarGridSpec(
            num_scalar_prefetch=2, grid=(n_blocks,),
            in_specs=[pl.BlockSpec((bm, d), rows),
                      pl.BlockSpec((1, d, 2 * D_FF), wsel), pl.BlockSpec((1, 1, 2 * D_FF), wsel),
                      pl.BlockSpec((1, D_FF, d), wsel), pl.BlockSpec((1, 1, d), wsel)],
            out_specs=pl.BlockSpec((bm, d), rows)),
        out_shape=jax.ShapeDtypeStruct((p_rows, d), F32),
        compiler_params=_cparams(("arbitrary",)),
        name="experts",
    )(bexp, nact, xs, wgu, bgu, wd, bd)


def _combine_body(dest_ref, ys_ref, gate_ref, h_ref, out_ref, buf_ref, sem, *, t):
    def issue(r, _):
        for k in range(TOP_K):
            _row_copy(ys_ref, dest_ref[k, r], buf_ref.at[k], r, sem).start()
        return 0

    lax.fori_loop(0, t, issue, 0)

    def drain(r, _):
        for k in range(TOP_K):
            _row_copy(ys_ref, dest_ref[k, r], buf_ref.at[k], r, sem).wait()
        return 0

    lax.fori_loop(0, t, drain, 0)

    gates = jnp.transpose(gate_ref[...])
    out = h_ref[...]
    for k in range(TOP_K):
        out = out + gates[:, k:k + 1] * buf_ref[k]
    out_ref[...] = out


def _combine_call(dest, ys, gates, h, *, t):
    n, d = h.shape
    return pl.pallas_call(
        functools.partial(_combine_body, t=t),
        grid=(n // t,),
        in_specs=[pl.BlockSpec((TOP_K, t), lambda i: (0, i), memory_space=pltpu.SMEM),
                  pl.BlockSpec(memory_space=pl.ANY),
                  pl.BlockSpec((2 * TOP_K, t), lambda i: (0, i)),
                  pl.BlockSpec((t, d), lambda i: (i, 0))],
        out_specs=pl.BlockSpec((t, d), lambda i: (i, 0)),
        out_shape=jax.ShapeDtypeStruct((n, d), F32),
        scratch_shapes=[pltpu.VMEM((TOP_K, t, d), F32), pltpu.SemaphoreType.DMA],
        compiler_params=_cparams(("arbitrary",)),
        name="combine",
    )(dest, ys, gates, h)


def _rot_half(z):
    half = QK_ROPE_DIM // 2
    return jnp.concatenate([-z[:, half:], z[:, :half]], axis=1)


def _pe_slot(z):
    rows = z.shape[0]
    return jnp.concatenate([jnp.zeros((rows, QK_NOPE_DIM), F32), z,
                            jnp.zeros((rows, LANES - QK_HEAD_DIM), F32)], axis=1)


def _prep_proj_weights(attn_norm_w, w_in, q_a_norm_w, w_q_b, kv_a_norm_w, w_kv_b, q_norm_w, k_norm_w, gate_b):
    c0 = 2 * CONV_DIM
    c1 = c0 + Q_LORA_RANK
    c2 = c1 + KV_LORA_RANK
    c3 = c2 + QK_ROPE_DIM
    kpe = w_in[:, c2:c3]
    win = jnp.concatenate([w_in[:, :c2], _pe_slot(kpe), _pe_slot(_rot_half(kpe * k_norm_w[None, QK_NOPE_DIM:])),
                           w_in[:, c3:]], axis=1).astype(BF16)

    pad = jnp.zeros((Q_LORA_RANK, LANES - QK_HEAD_DIM), F32)
    q1, q2 = [], []
    for h in range(N_HEADS):
        cols = w_q_b[:, h * QK_HEAD_DIM:(h + 1) * QK_HEAD_DIM]
        q1.append(jnp.concatenate([cols, pad], axis=1))
        q2.append(_pe_slot(_rot_half(cols[:, QK_NOPE_DIM:] * q_norm_w[None, QK_NOPE_DIM:])))
    wq = jnp.concatenate(q1 + q2, axis=1).astype(BF16)

    kpad = jnp.zeros((KV_LORA_RANK, LANES - QK_NOPE_DIM), F32)
    ks, vs = [], []
    per_head = QK_NOPE_DIM + V_HEAD_DIM
    for h in range(N_HEADS):
        cols = w_kv_b[:, h * per_head:(h + 1) * per_head]
        ks.append(jnp.concatenate([cols[:, :QK_NOPE_DIM], kpad], axis=1))
        vs.append(cols[:, QK_NOPE_DIM:])
    wkv = jnp.concatenate(ks + vs, axis=1).astype(BF16)

    lane_pad = jnp.zeros((LANES - QK_HEAD_DIM,), F32)
    qlane = jnp.concatenate([q_norm_w, lane_pad])[None, :]
    klane = jnp.concatenate([k_norm_w, lane_pad])[None, :]
    return (attn_norm_w[None, :], win, q_a_norm_w[None, :], wq, kv_a_norm_w[None, :], wkv, qlane, klane,
            gate_b[None, :])


def _rope_tables(length):
    half = QK_ROPE_DIM // 2
    inv_freq = ROPE_THETA ** (-jnp.arange(half, dtype=F32) / half)
    ang = jnp.arange(length, dtype=F32)[:, None] * inv_freq[None, :]
    cos, sin = jnp.cos(ang), jnp.sin(ang)
    ones = jnp.ones((length, QK_NOPE_DIM), F32)
    tail = LANES - QK_HEAD_DIM
    cos_t = jnp.concatenate([ones, cos, cos, jnp.ones((length, tail), F32)], axis=1)
    sin_t = jnp.concatenate([0.0 * ones, sin, sin, jnp.zeros((length, tail), F32)], axis=1)
    return cos_t, sin_t


def _tile(n, pref):
    t = pref
    while n % t:
        t //= 2
    return t


def kernel(x, meta_tokens, attn_norm_w, w_in, conv_dw_w, conv_dw_b, conv_ln_w, conv_ln_b, conv_pw2_w, q_a_norm_w, w_q_b, kv_a_norm_w, w_kv_b, q_norm_w, k_norm_w, w_o_mla, gate_b, w_out, ffn_norm_w, router_w, router_b, w_gate_up, b_gate_up, w_down, b_down):
    assert attn_norm_w.shape[0] == 1, "one layer: rows of meta tokens never feed a later layer"
    b, s, d = x.shape
    n = b * s
    x2 = x.reshape(n, d)

    tm = _tile(s, 256)
    tq = _tile(s, 512)
    cos_t, sin_t = _rope_tables(N_META + s)
    pw = _prep_proj_weights(attn_norm_w[0], w_in[0], q_a_norm_w[0], w_q_b[0], kv_a_norm_w[0], w_kv_b[0],
                            q_norm_w[0], k_norm_w[0], gate_b[0])

    glu, q, k, v, g = _proj_call(x2, (cos_t[N_META:], sin_t[N_META:]), pw, tm=tm, tiles_per_seq=s // tm)
    glum, _, km, vm, _ = _proj_call(meta_tokens.astype(F32), (cos_t[:N_META], sin_t[:N_META]), pw, tm=N_META,
                                    tiles_per_seq=1)

    o = _attn_call(q.reshape(b, s, -1), k.reshape(b, s, -1), v.reshape(b, s, -1), km, vm, tq=tq)

    mix_w = (conv_dw_w[0], conv_dw_b[0][None, :], conv_ln_w[0][None, :], conv_ln_b[0][None, :],
             conv_pw2_w[0].astype(BF16), w_o_mla[0].astype(BF16), w_out[0].astype(BF16), ffn_norm_w[0][None, :])
    h, hn = _mix_call(glu, glum, o.reshape(n, -1), g, x2, mix_w, tm=tm, tiles_per_seq=s // tm)

    rw = router_w[0].T
    rw_hi = rw.astype(BF16)
    rw_lo = (rw - rw_hi.astype(F32)).astype(BF16)
    idx, gates = _route_call(hn, rw_hi, rw_lo, router_b[0][:, None], tm=_tile(n, 512))

    bm = 256
    n_blocks = (n * TOP_K) // bm + N_EXPERTS
    rank, cnt = _rank_call(idx, t=_tile(n, 512))
    start, bexp, nact = _plan_call(cnt[:, 0], bm=bm, n_blocks=n_blocks)
    dest = _dest_call(start, idx, rank, t=_tile(n, 2048))
    xs = _scatter_call(dest, hn, t=_tile(n, 256), p_rows=n_blocks * bm)
    ys = _expert_call(bexp, nact, xs, w_gate_up[0].astype(BF16), b_gate_up[0][:, None, :],
                      w_down[0].astype(BF16), b_down[0][:, None, :], bm=bm)
    out = _combine_call(dest, ys, gates, h, t=_tile(n, 128))
    return out.reshape(b, s, d)
```

```python
import functools
import math

import jax
import jax.numpy as jnp
from jax import lax
from jax.experimental import pallas as pl
from jax.experimental.pallas import tpu as pltpu

N_META = 16
CONV_DIM = 512
CONV_WIDTH = 31
N_HEADS = 8
QK_NOPE_DIM = 64
QK_ROPE_DIM = 32
QK_HEAD_DIM = QK_NOPE_DIM + QK_ROPE_DIM
V_HEAD_DIM = 64
Q_LORA_RANK = 256
KV_LORA_RANK = 128
ROPE_THETA = 10000.0
N_EXPERTS = 32
TOP_K = 4
D_FF = 1024
SWIGLU_LIMIT = 7.0
SWIGLU_ALPHA = 1.702
NORM_EPS = 1e-6
NEG_INF = -1e30

LANES = 128
SUBLANES = 8
HALO = 32
VMEM_LIMIT = 56 * 1024 * 1024

F32 = jnp.float32
BF16 = jnp.bfloat16


def _cparams(sem):
    return pltpu.CompilerParams(dimension_semantics=sem, vmem_limit_bytes=VMEM_LIMIT)


def _dot(a, b):
    return jnp.dot(a, b, preferred_element_type=F32)


def _dot_t(a, b):
    return lax.dot_general(a, b, (((1,), (1,)), ((), ())), preferred_element_type=F32)


def _sigmoid(x):
    return 1.0 / (1.0 + jnp.exp(-x))


def _rms(x, w):
    return x * lax.rsqrt(jnp.mean(x * x, axis=-1, keepdims=True) + NORM_EPS) * w


C_CONV = 2 * CONV_DIM
C_LAT = Q_LORA_RANK + KV_LORA_RANK + 2 * LANES
C_GATE = 2 * 1024


def _proj_body(x_ref, anw_ref, win_ref, qaw_ref, wq_ref, kvaw_ref, wk_ref, wvt_ref, qlane_ref, klane_ref,
               cos_ref, sin_ref, gb_ref, glu_ref, q_ref, k_ref, vt_ref, g_ref, *, q_scale):
    x = x_ref[...]
    xn = _rms(x, anw_ref[...]).astype(BF16)

    u = _dot(xn, win_ref[:, 0:C_CONV])
    glu_ref[...] = u[:, :CONV_DIM] * _sigmoid(u[:, CONV_DIM:])

    gl = _dot(xn, win_ref[:, C_CONV + C_LAT:]) + gb_ref[...]
    g_ref[...] = _sigmoid(gl).astype(g_ref.dtype)

    lat = _dot(xn, win_ref[:, C_CONV:C_CONV + C_LAT])
    q_lat = lat[:, :Q_LORA_RANK]
    c_kv = lat[:, Q_LORA_RANK:Q_LORA_RANK + KV_LORA_RANK]
    kpe = lat[:, Q_LORA_RANK + KV_LORA_RANK:Q_LORA_RANK + KV_LORA_RANK + LANES]
    kpe_rot = lat[:, Q_LORA_RANK + KV_LORA_RANK + LANES:]

    cos = cos_ref[...]
    sin = sin_ref[...]

    qn = _rms(q_lat, qaw_ref[...]).astype(BF16)
    qq = _dot(qn, wq_ref[...])
    q_cos = qlane_ref[...] * cos
    for h in range(N_HEADS):
        q1 = qq[:, h * LANES:(h + 1) * LANES]
        q2 = qq[:, (N_HEADS + h) * LANES:(N_HEADS + h + 1) * LANES]
        ss = jnp.sum(q1 * q1, axis=-1, keepdims=True)
        s = lax.rsqrt(ss * (1.0 / QK_HEAD_DIM) + NORM_EPS) * q_scale
        q_ref[:, h * LANES:(h + 1) * LANES] = (s * (q1 * q_cos + q2 * sin)).astype(q_ref.dtype)

    cn = _rms(c_kv, kvaw_ref[...]).astype(BF16)
    kv = _dot(cn, wk_ref[...])
    vt_ref[...] = _dot_t(wvt_ref[...], cn).astype(vt_ref.dtype)
    k_cos = klane_ref[...] * cos
    ss_pe = jnp.sum(kpe * kpe, axis=-1, keepdims=True)
    k_pe_roped = kpe * k_cos + kpe_rot * sin
    for h in range(N_HEADS):
        kn = kv[:, h * LANES:(h + 1) * LANES]
        ss = jnp.sum(kn * kn, axis=-1, keepdims=True) + ss_pe
        s = lax.rsqrt(ss * (1.0 / QK_HEAD_DIM) + NORM_EPS)
        k_ref[:, h * LANES:(h + 1) * LANES] = (s * (kn * k_cos + k_pe_roped)).astype(k_ref.dtype)


def _proj_call(x2, tables, weights, *, tm, tiles_per_seq):
    n = x2.shape[0]
    d = x2.shape[1]
    cos_t, sin_t = tables
    (anw, win, qaw, wq, kvaw, wk, wvt, qlane, klane, gb) = weights
    full = lambda a: pl.BlockSpec(a.shape, lambda i: (0,) * a.ndim)
    row = lambda c: pl.BlockSpec((tm, c), lambda i: (i, 0))
    pos = pl.BlockSpec((tm, LANES), lambda i: (i % tiles_per_seq, 0))
    q_scale = (QK_HEAD_DIM ** -0.5) * math.log2(math.e)
    return pl.pallas_call(
        functools.partial(_proj_body, q_scale=q_scale),
        grid=(n // tm,),
        in_specs=[row(d), full(anw), full(win), full(qaw), full(wq), full(kvaw), full(wk), full(wvt),
                  full(qlane), full(klane), pos, pos, full(gb)],
        out_specs=[row(CONV_DIM), row(N_HEADS * LANES), row(N_HEADS * LANES),
                   pl.BlockSpec((N_HEADS * V_HEAD_DIM, tm), lambda i: (0, i)), row(C_GATE)],
        out_shape=[jax.ShapeDtypeStruct((n, CONV_DIM), F32),
                   jax.ShapeDtypeStruct((n, N_HEADS * LANES), BF16),
                   jax.ShapeDtypeStruct((n, N_HEADS * LANES), BF16),
                   jax.ShapeDtypeStruct((N_HEADS * V_HEAD_DIM, n), BF16),
                   jax.ShapeDtypeStruct((n, C_GATE), BF16)],
        compiler_params=_cparams(("parallel",)),
        name="proj",
    )(x2, anw, win, qaw, wq, kvaw, wk, wvt, qlane, klane, cos_t, sin_t, gb)


def _attn_body(qi_ref, kj_ref, q_ref, k_ref, vt_ref, km_ref, vmt_ref, o_ref, m_ref, l_ref, acc_ref):
    p_id = pl.program_id(1)
    i = qi_ref[p_id]
    j = kj_ref[p_id]

    def head_slices(h):
        return slice(h * LANES, (h + 1) * LANES), slice(h * V_HEAD_DIM, (h + 1) * V_HEAD_DIM)

    @pl.when(j == 0)
    def _meta():
        for h in range(N_HEADS):
            ks, vs = head_slices(h)
            st = _dot_t(km_ref[:, ks], q_ref[:, ks])
            m = jnp.max(st, axis=0, keepdims=True)
            p = jnp.exp2(st - m)
            m_ref[h:h + 1, :] = m
            l_ref[h:h + 1, :] = jnp.sum(p, axis=0, keepdims=True)
            acc_ref[h] = _dot(vmt_ref[vs, :], p.astype(BF16))

    def step(diagonal):
        for h in range(N_HEADS):
            ks, vs = head_slices(h)
            st = _dot_t(k_ref[:, ks], q_ref[:, ks])
            if diagonal:
                key = lax.broadcasted_iota(jnp.int32, st.shape, 0)
                qry = lax.broadcasted_iota(jnp.int32, st.shape, 1)
                st = jnp.where(key <= qry, st, NEG_INF)
            m_prev = m_ref[h:h + 1, :]
            m_new = jnp.maximum(m_prev, jnp.max(st, axis=0, keepdims=True))
            alpha = jnp.exp2(m_prev - m_new)
            p = jnp.exp2(st - m_new)
            l_new = alpha * l_ref[h:h + 1, :] + jnp.sum(p, axis=0, keepdims=True)
            acc = alpha * acc_ref[h] + _dot(vt_ref[vs, :], p.astype(BF16))
            if diagonal:
                acc_ref[h] = acc / l_new
            else:
                m_ref[h:h + 1, :] = m_new
                l_ref[h:h + 1, :] = l_new
                acc_ref[h] = acc

    @pl.when(j < i)
    def _full():
        step(False)

    @pl.when(j == i)
    def _diag():
        step(True)
        ot = acc_ref[...].reshape(N_HEADS * V_HEAD_DIM, acc_ref.shape[-1])
        o_ref[...] = jnp.transpose(ot).astype(o_ref.dtype)


def _attn_call(q, k, vt, km, vmt, *, tq, batch):
    n = q.shape[0]
    nq = n // batch // tq
    pairs = [(i, j) for i in range(nq) for j in range(i + 1)]
    qi = jnp.asarray([p[0] for p in pairs], jnp.int32)
    kj = jnp.asarray([p[1] for p in pairs], jnp.int32)
    q_map = lambda bi, p, qi, kj: (bi * nq + qi[p], 0)
    k_map = lambda bi, p, qi, kj: (bi * nq + kj[p], 0)
    vt_map = lambda bi, p, qi, kj: (0, bi * nq + kj[p])
    const = lambda bi, p, qi, kj: (0, 0)
    return pl.pallas_call(
        _attn_body,
        grid_spec=pltpu.PrefetchScalarGridSpec(
            num_scalar_prefetch=2, grid=(batch, len(pairs)),
            in_specs=[pl.BlockSpec((tq, N_HEADS * LANES), q_map),
                      pl.BlockSpec((tq, N_HEADS * LANES), k_map),
                      pl.BlockSpec((N_HEADS * V_HEAD_DIM, tq), vt_map),
                      pl.BlockSpec(km.shape, const), pl.BlockSpec(vmt.shape, const)],
            out_specs=pl.BlockSpec((tq, N_HEADS * V_HEAD_DIM), q_map),
            scratch_shapes=[pltpu.VMEM((N_HEADS, tq), F32), pltpu.VMEM((N_HEADS, tq), F32),
                            pltpu.VMEM((N_HEADS, V_HEAD_DIM, tq), F32)]),
        out_shape=jax.ShapeDtypeStruct((n, N_HEADS * V_HEAD_DIM), BF16),
        compiler_params=_cparams(("parallel", "arbitrary")),
        name="attn",
    )(qi, kj, q, k, vt, km, vmt)


def _mix_body(glu_ref, halo_ref, glum_ref, o_ref, g_ref, x_ref, dww_ref, dwb_ref, lnw_ref, lnb_ref,
              pw2_ref, wo_ref, wout_ref, fnw_ref, h_ref, hn_ref, xpad_ref, shift_ref, *, tm, tiles_per_seq):
    i = pl.program_id(0)
    first = (i % tiles_per_seq) == 0
    meta_ctx = jnp.concatenate([jnp.zeros((HALO - N_META, CONV_DIM), F32), glum_ref[...]], axis=0)
    xpad_ref[0:HALO, :] = jnp.where(first, meta_ctx, halo_ref[...])
    xpad_ref[HALO:HALO + tm, :] = glu_ref[...]

    span = tm + HALO - SUBLANES
    for rho in range(1, SUBLANES):
        shift_ref[rho, 0:span, :] = xpad_ref[rho:rho + span, :]
    off = HALO - (CONV_WIDTH - 1)
    acc = jnp.zeros((tm, CONV_DIM), F32) + dwb_ref[...]
    for t in range(CONV_WIDTH):
        rho, base = (off + t) % SUBLANES, (off + t) // SUBLANES * SUBLANES
        if rho == 0:
            win = xpad_ref[base:base + tm, :]
        else:
            win = shift_ref[rho, base:base + tm, :]
        acc = acc + dww_ref[t:t + 1, :] * win

    mu = jnp.mean(acc, axis=-1, keepdims=True)
    xc = acc - mu
    y = xc * lax.rsqrt(jnp.mean(xc * xc, axis=-1, keepdims=True) + NORM_EPS) * lnw_ref[...] + lnb_ref[...]
    y = y * _sigmoid(y)
    y_conv = _dot(y.astype(BF16), pw2_ref[...])
    y_mla = _dot(o_ref[...], wo_ref[...])
    g = g_ref[...].astype(F32)
    mixed = g[:, :1024] * y_conv + g[:, 1024:] * y_mla
    h = x_ref[...] + _dot(mixed.astype(BF16), wout_ref[...])
    h_ref[...] = h
    hn_ref[...] = _rms(h, fnw_ref[...])


def _mix_call(glu, glum, o, g, x2, weights, *, tm, tiles_per_seq):
    n, d = x2.shape
    full = lambda a: pl.BlockSpec(a.shape, lambda i: (0,) * a.ndim)
    row = lambda c: pl.BlockSpec((tm, c), lambda i: (i, 0))
    halo = pl.BlockSpec((HALO, CONV_DIM), lambda i: (jnp.maximum(i * (tm // HALO) - 1, 0), 0))
    return pl.pallas_call(
        functools.partial(_mix_body, tm=tm, tiles_per_seq=tiles_per_seq),
        grid=(n // tm,),
        in_specs=[row(CONV_DIM), halo, full(glum), row(N_HEADS * V_HEAD_DIM), row(C_GATE), row(d)]
                 + [full(w) for w in weights],
        out_specs=[row(d), row(d)],
        out_shape=[jax.ShapeDtypeStruct((n, d), F32), jax.ShapeDtypeStruct((n, d), F32)],
        scratch_shapes=[pltpu.VMEM((HALO + tm, CONV_DIM), F32),
                        pltpu.VMEM((SUBLANES, HALO + tm, CONV_DIM), F32)],
        compiler_params=_cparams(("parallel",)),
        name="mix",
    )(glu, glu, glum, o, g, x2, *weights)


def _route_body(hn_ref, whi_ref, wlo_ref, rb_ref, idx_ref, gate_ref):
    x = hn_ref[...]
    x_hi = x.astype(BF16)
    x_lo = (x - x_hi.astype(F32)).astype(BF16)
    w_hi = whi_ref[...]
    logits = _dot_t(w_hi, x_hi) + _dot_t(w_hi, x_lo) + _dot_t(wlo_ref[...], x_hi) + rb_ref[...]

    e_iota = lax.broadcasted_iota(jnp.int32, logits.shape, 0).astype(F32)
    vals, idxs = [], []
    cur = logits
    for _ in range(TOP_K):
        m = jnp.max(cur, axis=0, keepdims=True)
        idx = jnp.min(jnp.where(cur == m, e_iota, float(N_EXPERTS)), axis=0, keepdims=True)
        vals.append(m)
        idxs.append(idx)
        cur = jnp.where(e_iota == idx, -jnp.inf, cur)
    exps = [jnp.exp(v - vals[0]) for v in vals]
    denom = exps[0] + exps[1] + exps[2] + exps[3]
    idx_ref[...] = jnp.concatenate(idxs, axis=0).astype(jnp.int32)
    gate_ref[...] = jnp.concatenate([e / denom for e in exps] + [jnp.zeros_like(denom)] * 4, axis=0)


def _route_call(hn, w_hi, w_lo, rb, *, tm):
    n, d = hn.shape
    full = lambda a: pl.BlockSpec(a.shape, lambda i: (0,) * a.ndim)
    return pl.pallas_call(
        _route_body,
        grid=(n // tm,),
        in_specs=[pl.BlockSpec((tm, d), lambda i: (i, 0)), full(w_hi), full(w_lo), full(rb)],
        out_specs=[pl.BlockSpec((TOP_K, tm), lambda i: (0, i)), pl.BlockSpec((2 * TOP_K, tm), lambda i: (0, i))],
        out_shape=[jax.ShapeDtypeStruct((TOP_K, n), jnp.int32), jax.ShapeDtypeStruct((2 * TOP_K, n), F32)],
        compiler_params=_cparams(("parallel",)),
        name="route",
    )(hn, w_hi, w_lo, rb)


def _rank_body(idx_ref, rank_ref, cnt_ref, carry_ref, *, t):
    i = pl.program_id(0)

    @pl.when(i == 0)
    def _init():
        carry_ref[...] = jnp.zeros_like(carry_ref)

    idx = idx_ref[...]
    e_iota = lax.broadcasted_iota(jnp.int32, (N_EXPERTS, t), 0)
    hits = [e_iota == idx[k:k + 1, :] for k in range(TOP_K)]
    onehot = jnp.zeros((N_EXPERTS, t), F32)
    for hk in hits:
        onehot = onehot + jnp.where(hk, 1.0, 0.0)
    r = lax.broadcasted_iota(jnp.int32, (t, t), 0)
    c = lax.broadcasted_iota(jnp.int32, (t, t), 1)
    before = jnp.where(r < c, 1.0, 0.0).astype(BF16)
    val = _dot(onehot.astype(BF16), before) + carry_ref[...]
    ranks = [jnp.sum(jnp.where(hk, val, 0.0), axis=0, keepdims=True) for hk in hits]
    rank_ref[...] = jnp.concatenate(ranks, axis=0).astype(jnp.int32)
    total = carry_ref[...] + jnp.sum(onehot, axis=1, keepdims=True)
    carry_ref[...] = total
    cnt_ref[...] = jnp.broadcast_to(total, cnt_ref.shape).astype(jnp.int32)


def _rank_call(idx, *, t):
    n = idx.shape[1]
    return pl.pallas_call(
        functools.partial(_rank_body, t=t),
        grid=(n // t,),
        in_specs=[pl.BlockSpec((TOP_K, t), lambda i: (0, i))],
        out_specs=[pl.BlockSpec((TOP_K, t), lambda i: (0, i)), pl.BlockSpec((N_EXPERTS, LANES), lambda i: (0, 0))],
        out_shape=[jax.ShapeDtypeStruct((TOP_K, n), jnp.int32), jax.ShapeDtypeStruct((N_EXPERTS, LANES), jnp.int32)],
        scratch_shapes=[pltpu.VMEM((N_EXPERTS, 1), F32)],
        compiler_params=_cparams(("arbitrary",)),
        name="rank",
    )(idx)


def _plan_body(cnt_ref, start_ref, bexp_ref, nact_ref, *, bm, n_blocks):
    def per_expert(e, carry):
        blk, last = carry
        c = cnt_ref[e]
        nb = (c + (bm - 1)) // bm
        start_ref[e] = blk * bm

        def fill(b, _):
            bexp_ref[b] = e
            return 0

        lax.fori_loop(blk, blk + nb, fill, 0)
        return blk + nb, jnp.where(nb > 0, e, last)

    n_act, last = lax.fori_loop(0, N_EXPERTS, per_expert, (jnp.int32(0), jnp.int32(0)))
    nact_ref[0] = n_act

    def tail(b, _):
        bexp_ref[b] = last
        return 0

    lax.fori_loop(n_act, n_blocks, tail, 0)


def _plan_call(cnt, *, bm, n_blocks):
    smem = lambda: pl.BlockSpec(memory_space=pltpu.SMEM)
    return pl.pallas_call(
        functools.partial(_plan_body, bm=bm, n_blocks=n_blocks),
        in_specs=[smem()],
        out_specs=[smem(), smem(), smem()],
        out_shape=[jax.ShapeDtypeStruct((N_EXPERTS,), jnp.int32), jax.ShapeDtypeStruct((n_blocks,), jnp.int32),
                   jax.ShapeDtypeStruct((1,), jnp.int32)],
        name="plan",
    )(cnt)


def _dest_body(start_ref, idx_ref, rank_ref, dest_ref):
    idx = idx_ref[...]
    base = jnp.zeros(idx.shape, jnp.int32)
    for e in range(N_EXPERTS):
        base = jnp.where(idx == e, start_ref[e], base)
    dest_ref[...] = base + rank_ref[...]


def _dest_call(start, idx, rank, *, t):
    n = idx.shape[1]
    blk = pl.BlockSpec((TOP_K, t), lambda i, s: (0, i))
    return pl.pallas_call(
        _dest_body,
        grid_spec=pltpu.PrefetchScalarGridSpec(num_scalar_prefetch=1, grid=(n // t,), in_specs=[blk, blk],
                                               out_specs=blk),
        out_shape=jax.ShapeDtypeStruct((TOP_K, n), jnp.int32),
        compiler_params=_cparams(("parallel",)),
        name="dest",
    )(start, idx, rank)


def _row_copy(src, src_row, dst, dst_row, sem):
    return pltpu.make_async_copy(src.at[pl.ds(src_row, 1)], dst.at[pl.ds(dst_row, 1)], sem)


def _scatter_body(dest_ref, x_ref, xs_ref, sem, *, t):
    def issue(r, _):
        for k in range(TOP_K):
            _row_copy(x_ref, r, xs_ref, dest_ref[k, r], sem).start(priority=k % 2)
        return 0

    lax.fori_loop(0, t, issue, 0)

    def drain(r, _):
        for k in range(TOP_K):
            _row_copy(x_ref, r, xs_ref, dest_ref[k, r], sem).wait()
        return 0

    lax.fori_loop(0, t, drain, 0)


def _scatter_call(dest, x, *, t, p_rows):
    n, d = x.shape
    return pl.pallas_call(
        functools.partial(_scatter_body, t=t),
        grid=(n // t,),
        in_specs=[pl.BlockSpec((TOP_K, t), lambda i: (0, i), memory_space=pltpu.SMEM),
                  pl.BlockSpec((t, d), lambda i: (i, 0))],
        out_specs=pl.BlockSpec(memory_space=pl.ANY),
        out_shape=jax.ShapeDtypeStruct((p_rows, d), F32),
        scratch_shapes=[pltpu.SemaphoreType.DMA],
        compiler_params=_cparams(("arbitrary",)),
        name="scatter",
    )(dest, x)


def _expert_body(bexp_ref, nact_ref, xs_ref, wgu_ref, bgu_ref, wd_ref, bd_ref, ys_ref):
    b = pl.program_id(0)

    @pl.when(b < nact_ref[0])
    def _():
        x = xs_ref[...].astype(BF16)
        gu = _dot(x, wgu_ref[0]) + bgu_ref[0]
        g = jnp.minimum(gu[:, :D_FF], SWIGLU_LIMIT)
        u = jnp.clip(gu[:, D_FF:], -SWIGLU_LIMIT, SWIGLU_LIMIT)
        act = (u + 1.0) * (g * _sigmoid(SWIGLU_ALPHA * g))
        ys_ref[...] = _dot(act.astype(BF16), wd_ref[0]) + bd_ref[0]


def _expert_call(bexp, nact, xs, wgu, bgu, wd, bd, *, bm):
    p_rows, d = xs.shape
    n_blocks = p_rows // bm
    rows = lambda b, be, na: (jnp.minimum(b, na[0] - 1), 0)
    wsel = lambda b, be, na: (be[b], 0, 0)
    return pl.pallas_call(
        _expert_body,
        grid_spec=pltpu.PrefetchScalarGridSpec(
            num_scalar_prefetch=2, grid=(n_blocks,),
            in_specs=[pl.BlockSpec((bm, d), rows),
                      pl.BlockSpec((1, d, 2 * D_FF), wsel), pl.BlockSpec((1, 1, 2 * D_FF), wsel),
                      pl.BlockSpec((1, D_FF, d), wsel), pl.BlockSpec((1, 1, d), wsel)],
            out_specs=pl.BlockSpec((bm, d), rows)),
        out_shape=jax.ShapeDtypeStruct((p_rows, d), F32),
        compiler_params=_cparams(("arbitrary",)),
        name="experts",
    )(bexp, nact, xs, wgu, bgu, wd, bd)


def _combine_body(dest_ref, ys_ref, gate_ref, h_ref, out_ref, buf_ref, sem, *, t):
    def issue(r, _):
        for k in range(TOP_K):
            _row_copy(ys_ref, dest_ref[k, r], buf_ref.at[k], r, sem).start(priority=k % 2)
        return 0

    lax.fori_loop(0, t, issue, 0)

    def drain(r, _):
        for k in range(TOP_K):
            _row_copy(ys_ref, dest_ref[k, r], buf_ref.at[k], r, sem).wait()
        return 0

    lax.fori_loop(0, t, drain, 0)

    gates = jnp.transpose(gate_ref[...])
    out = h_ref[...]
    for k in range(TOP_K):
        out = out + gates[:, k:k + 1] * buf_ref[k]
    out_ref[...] = out


def _combine_call(dest, ys, gates, h, *, t):
    n, d = h.shape
    return pl.pallas_call(
        functools.partial(_combine_body, t=t),
        grid=(n // t,),
        in_specs=[pl.BlockSpec((TOP_K, t), lambda i: (0, i), memory_space=pltpu.SMEM),
                  pl.BlockSpec(memory_space=pl.ANY),
                  pl.BlockSpec((2 * TOP_K, t), lambda i: (0, i)),
                  pl.BlockSpec((t, d), lambda i: (i, 0))],
        out_specs=pl.BlockSpec((t, d), lambda i: (i, 0)),
        out_shape=jax.ShapeDtypeStruct((n, d), F32),
        scratch_shapes=[pltpu.VMEM((TOP_K, t, d), F32), pltpu.SemaphoreType.DMA],
        compiler_params=_cparams(("arbitrary",)),
        name="combine",
    )(dest, ys, gates, h)


def _rot_half(z):
    half = QK_ROPE_DIM // 2
    return jnp.concatenate([-z[:, half:], z[:, :half]], axis=1)


def _pe_slot(z):
    rows = z.shape[0]
    return jnp.concatenate([jnp.zeros((rows, QK_NOPE_DIM), F32), z,
                            jnp.zeros((rows, LANES - QK_HEAD_DIM), F32)], axis=1)


def _prep_proj_weights(attn_norm_w, w_in, q_a_norm_w, w_q_b, kv_a_norm_w, w_kv_b, q_norm_w, k_norm_w, gate_b):
    c0 = 2 * CONV_DIM
    c1 = c0 + Q_LORA_RANK
    c2 = c1 + KV_LORA_RANK
    c3 = c2 + QK_ROPE_DIM
    kpe = w_in[:, c2:c3]
    win = jnp.concatenate([w_in[:, :c2], _pe_slot(kpe), _pe_slot(_rot_half(kpe * k_norm_w[None, QK_NOPE_DIM:])),
                           w_in[:, c3:]], axis=1).astype(BF16)

    pad = jnp.zeros((Q_LORA_RANK, LANES - QK_HEAD_DIM), F32)
    q1, q2 = [], []
    for h in range(N_HEADS):
        cols = w_q_b[:, h * QK_HEAD_DIM:(h + 1) * QK_HEAD_DIM]
        q1.append(jnp.concatenate([cols, pad], axis=1))
        q2.append(_pe_slot(_rot_half(cols[:, QK_NOPE_DIM:] * q_norm_w[None, QK_NOPE_DIM:])))
    wq = jnp.concatenate(q1 + q2, axis=1).astype(BF16)

    kpad = jnp.zeros((KV_LORA_RANK, LANES - QK_NOPE_DIM), F32)
    ks, vs = [], []
    per_head = QK_NOPE_DIM + V_HEAD_DIM
    for h in range(N_HEADS):
        cols = w_kv_b[:, h * per_head:(h + 1) * per_head]
        ks.append(jnp.concatenate([cols[:, :QK_NOPE_DIM], kpad], axis=1))
        vs.append(cols[:, QK_NOPE_DIM:])
    wk = jnp.concatenate(ks, axis=1).astype(BF16)
    wvt = jnp.concatenate(vs, axis=1).T.astype(BF16)

    lane_pad = jnp.zeros((LANES - QK_HEAD_DIM,), F32)
    qlane = jnp.concatenate([q_norm_w, lane_pad])[None, :]
    klane = jnp.concatenate([k_norm_w, lane_pad])[None, :]
    return (attn_norm_w[None, :], win, q_a_norm_w[None, :], wq, kv_a_norm_w[None, :], wk, wvt, qlane, klane,
            gate_b[None, :])


def _rope_tables(length):
    half = QK_ROPE_DIM // 2
    inv_freq = ROPE_THETA ** (-jnp.arange(half, dtype=F32) / half)
    ang = jnp.arange(length, dtype=F32)[:, None] * inv_freq[None, :]
    cos, sin = jnp.cos(ang), jnp.sin(ang)
    ones = jnp.ones((length, QK_NOPE_DIM), F32)
    tail = LANES - QK_HEAD_DIM
    cos_t = jnp.concatenate([ones, cos, cos, jnp.ones((length, tail), F32)], axis=1)
    sin_t = jnp.concatenate([0.0 * ones, sin, sin, jnp.zeros((length, tail), F32)], axis=1)
    return cos_t, sin_t


def _tile(n, pref):
    t = pref
    while n % t:
        t //= 2
    return t


def kernel(x, meta_tokens, attn_norm_w, w_in, conv_dw_w, conv_dw_b, conv_ln_w, conv_ln_b, conv_pw2_w, q_a_norm_w, w_q_b, kv_a_norm_w, w_kv_b, q_norm_w, k_norm_w, w_o_mla, gate_b, w_out, ffn_norm_w, router_w, router_b, w_gate_up, b_gate_up, w_down, b_down):
    assert attn_norm_w.shape[0] == 1, "one layer: rows of meta tokens never feed a later layer"
    b, s, d = x.shape
    n = b * s
    x2 = x.reshape(n, d)

    tm = _tile(s, 256)
    tq = _tile(s, 512)
    cos_t, sin_t = _rope_tables(N_META + s)
    pw = _prep_proj_weights(attn_norm_w[0], w_in[0], q_a_norm_w[0], w_q_b[0], kv_a_norm_w[0], w_kv_b[0],
                            q_norm_w[0], k_norm_w[0], gate_b[0])

    tp = _tile(s, 512)
    glu, q, k, vt, g = _proj_call(x2, (cos_t[N_META:], sin_t[N_META:]), pw, tm=tp, tiles_per_seq=s // tp)
    glum, _, km, vmt, _ = _proj_call(meta_tokens.astype(F32), (cos_t[:N_META], sin_t[:N_META]), pw, tm=N_META,
                                     tiles_per_seq=1)

    o = _attn_call(q, k, vt, km, vmt, tq=tq, batch=b)

    mix_w = (conv_dw_w[0], conv_dw_b[0][None, :], conv_ln_w[0][None, :], conv_ln_b[0][None, :],
             conv_pw2_w[0].astype(BF16), w_o_mla[0].astype(BF16), w_out[0].astype(BF16), ffn_norm_w[0][None, :])
    h, hn = _mix_call(glu, glum, o.reshape(n, -1), g, x2, mix_w, tm=tm, tiles_per_seq=s // tm)

    rw = router_w[0].T
    rw_hi = rw.astype(BF16)
    rw_lo = (rw - rw_hi.astype(F32)).astype(BF16)
    idx, gates = _route_call(hn, rw_hi, rw_lo, router_b[0][:, None], tm=_tile(n, 512))

    bm = 512
    n_blocks = (n * TOP_K) // bm + N_EXPERTS
    rank, cnt = _rank_call(idx, t=_tile(n, 512))
    start, bexp, nact = _plan_call(cnt[:, 0], bm=bm, n_blocks=n_blocks)
    dest = _dest_call(start, idx, rank, t=_tile(n, 2048))
    xs = _scatter_call(dest, hn, t=_tile(n, 256), p_rows=n_blocks * bm)
    ys = _expert_call(bexp, nact, xs, w_gate_up[0].astype(BF16), b_gate_up[0][:, None, :],
                      w_down[0].astype(BF16), b_down[0][:, None, :], bm=bm)
    out = _combine_call(dest, ys, gates, h, t=_tile(n, 128))
    return out.reshape(b, s, d)
```

```python
import functools
import math

import jax
import jax.numpy as jnp
from jax import lax
from jax.experimental import pallas as pl
from jax.experimental.pallas import tpu as pltpu

N_META = 16
CONV_DIM = 512
CONV_WIDTH = 31
N_HEADS = 8
QK_NOPE_DIM = 64
QK_ROPE_DIM = 32
QK_HEAD_DIM = QK_NOPE_DIM + QK_ROPE_DIM
V_HEAD_DIM = 64
Q_LORA_RANK = 256
KV_LORA_RANK = 128
ROPE_THETA = 10000.0
N_EXPERTS = 32
TOP_K = 4
D_FF = 1024
SWIGLU_LIMIT = 7.0
SWIGLU_ALPHA = 1.702
NORM_EPS = 1e-6
NEG_INF = -1e30

LANES = 128
SUBLANES = 8
ROW_CHUNKS = 8
HALO = 32
VMEM_LIMIT = 56 * 1024 * 1024

F32 = jnp.float32
BF16 = jnp.bfloat16


def _cparams(sem):
    return pltpu.CompilerParams(dimension_semantics=sem, vmem_limit_bytes=VMEM_LIMIT)


def _dot(a, b):
    return jnp.dot(a, b, preferred_element_type=F32)


def _dot_t(a, b):
    return lax.dot_general(a, b, (((1,), (1,)), ((), ())), preferred_element_type=F32)


def _sigmoid(x):
    return 1.0 / (1.0 + jnp.exp(-x))


def _rms(x, w):
    return x * lax.rsqrt(jnp.mean(x * x, axis=-1, keepdims=True) + NORM_EPS) * w


C_CONV = 2 * CONV_DIM
C_LAT = Q_LORA_RANK + KV_LORA_RANK + 2 * LANES
C_GATE = 2 * 1024


def _proj_body(x_ref, anw_ref, win_ref, qaw_ref, wq_ref, kvaw_ref, wk_ref, wvt_ref, qlane_ref, klane_ref,
               cos_ref, sin_ref, gb_ref, glu_ref, q_ref, k_ref, vt_ref, g_ref, *, q_scale):
    x = x_ref[...]
    xn = _rms(x, anw_ref[...]).astype(BF16)

    u = _dot(xn, win_ref[:, 0:C_CONV])
    glu_ref[...] = u[:, :CONV_DIM] * _sigmoid(u[:, CONV_DIM:])

    gl = _dot(xn, win_ref[:, C_CONV + C_LAT:]) + gb_ref[...]
    g_ref[...] = _sigmoid(gl).astype(g_ref.dtype)

    lat = _dot(xn, win_ref[:, C_CONV:C_CONV + C_LAT])
    q_lat = lat[:, :Q_LORA_RANK]
    c_kv = lat[:, Q_LORA_RANK:Q_LORA_RANK + KV_LORA_RANK]
    kpe = lat[:, Q_LORA_RANK + KV_LORA_RANK:Q_LORA_RANK + KV_LORA_RANK + LANES]
    kpe_rot = lat[:, Q_LORA_RANK + KV_LORA_RANK + LANES:]

    cos = cos_ref[...]
    sin = sin_ref[...]

    qn = _rms(q_lat, qaw_ref[...]).astype(BF16)
    qq = _dot(qn, wq_ref[...])
    q_cos = qlane_ref[...] * cos
    for h in range(N_HEADS):
        q1 = qq[:, h * LANES:(h + 1) * LANES]
        q2 = qq[:, (N_HEADS + h) * LANES:(N_HEADS + h + 1) * LANES]
        ss = jnp.sum(q1 * q1, axis=-1, keepdims=True)
        s = lax.rsqrt(ss * (1.0 / QK_HEAD_DIM) + NORM_EPS) * q_scale
        q_ref[:, h * LANES:(h + 1) * LANES] = (s * (q1 * q_cos + q2 * sin)).astype(q_ref.dtype)

    cn = _rms(c_kv, kvaw_ref[...]).astype(BF16)
    kv = _dot(cn, wk_ref[...])
    vt_ref[...] = _dot_t(wvt_ref[...], cn).astype(vt_ref.dtype)
    k_cos = klane_ref[...] * cos
    ss_pe = jnp.sum(kpe * kpe, axis=-1, keepdims=True)
    k_pe_roped = kpe * k_cos + kpe_rot * sin
    for h in range(N_HEADS):
        kn = kv[:, h * LANES:(h + 1) * LANES]
        ss = jnp.sum(kn * kn, axis=-1, keepdims=True) + ss_pe
        s = lax.rsqrt(ss * (1.0 / QK_HEAD_DIM) + NORM_EPS)
        k_ref[:, h * LANES:(h + 1) * LANES] = (s * (kn * k_cos + k_pe_roped)).astype(k_ref.dtype)


def _proj_call(x2, tables, weights, *, tm, tiles_per_seq):
    n = x2.shape[0]
    d = x2.shape[1]
    cos_t, sin_t = tables
    (anw, win, qaw, wq, kvaw, wk, wvt, qlane, klane, gb) = weights
    full = lambda a: pl.BlockSpec(a.shape, lambda i: (0,) * a.ndim)
    row = lambda c: pl.BlockSpec((tm, c), lambda i: (i, 0))
    pos = pl.BlockSpec((tm, LANES), lambda i: (i % tiles_per_seq, 0))
    q_scale = (QK_HEAD_DIM ** -0.5) * math.log2(math.e)
    return pl.pallas_call(
        functools.partial(_proj_body, q_scale=q_scale),
        grid=(n // tm,),
        in_specs=[row(d), full(anw), full(win), full(qaw), full(wq), full(kvaw), full(wk), full(wvt),
                  full(qlane), full(klane), pos, pos, full(gb)],
        out_specs=[row(CONV_DIM), row(N_HEADS * LANES), row(N_HEADS * LANES),
                   pl.BlockSpec((N_HEADS * V_HEAD_DIM, tm), lambda i: (0, i)), row(C_GATE)],
        out_shape=[jax.ShapeDtypeStruct((n, CONV_DIM), F32),
                   jax.ShapeDtypeStruct((n, N_HEADS * LANES), BF16),
                   jax.ShapeDtypeStruct((n, N_HEADS * LANES), BF16),
                   jax.ShapeDtypeStruct((N_HEADS * V_HEAD_DIM, n), BF16),
                   jax.ShapeDtypeStruct((n, C_GATE), BF16)],
        compiler_params=_cparams(("parallel",)),
        name="proj",
    )(x2, anw, win, qaw, wq, kvaw, wk, wvt, qlane, klane, cos_t, sin_t, gb)


def _attn_body(qi_ref, kj_ref, q_ref, k_ref, vt_ref, km_ref, vmt_ref, o_ref, m_ref, l_ref, acc_ref):
    p_id = pl.program_id(1)
    i = qi_ref[p_id]
    j = kj_ref[p_id]

    def head_slices(h):
        return slice(h * LANES, (h + 1) * LANES), slice(h * V_HEAD_DIM, (h + 1) * V_HEAD_DIM)

    @pl.when(j == 0)
    def _meta():
        for h in range(N_HEADS):
            ks, vs = head_slices(h)
            st = _dot_t(km_ref[:, ks], q_ref[:, ks])
            m = jnp.max(st, axis=0, keepdims=True)
            p = jnp.exp2(st - m)
            m_ref[h:h + 1, :] = m
            l_ref[h:h + 1, :] = jnp.sum(p, axis=0, keepdims=True)
            acc_ref[h] = _dot(vmt_ref[vs, :], p.astype(BF16))

    def step(diagonal):
        for h in range(N_HEADS):
            ks, vs = head_slices(h)
            st = _dot_t(k_ref[:, ks], q_ref[:, ks])
            if diagonal:
                key = lax.broadcasted_iota(jnp.int32, st.shape, 0)
                qry = lax.broadcasted_iota(jnp.int32, st.shape, 1)
                st = jnp.where(key <= qry, st, NEG_INF)
            m_prev = m_ref[h:h + 1, :]
            m_new = jnp.maximum(m_prev, jnp.max(st, axis=0, keepdims=True))
            alpha = jnp.exp2(m_prev - m_new)
            p = jnp.exp2(st - m_new)
            l_new = alpha * l_ref[h:h + 1, :] + jnp.sum(p, axis=0, keepdims=True)
            acc = alpha * acc_ref[h] + _dot(vt_ref[vs, :], p.astype(BF16))
            if diagonal:
                acc_ref[h] = acc / l_new
            else:
                m_ref[h:h + 1, :] = m_new
                l_ref[h:h + 1, :] = l_new
                acc_ref[h] = acc

    @pl.when(j < i)
    def _full():
        step(False)

    @pl.when(j == i)
    def _diag():
        step(True)
        ot = acc_ref[...].reshape(N_HEADS * V_HEAD_DIM, acc_ref.shape[-1])
        o_ref[...] = jnp.transpose(ot).astype(o_ref.dtype)


def _attn_call(q, k, vt, km, vmt, *, tq, batch):
    n = q.shape[0]
    nq = n // batch // tq
    pairs = [(i, j) for i in range(nq) for j in range(i + 1)]
    qi = jnp.asarray([p[0] for p in pairs], jnp.int32)
    kj = jnp.asarray([p[1] for p in pairs], jnp.int32)
    q_map = lambda bi, p, qi, kj: (bi * nq + qi[p], 0)
    k_map = lambda bi, p, qi, kj: (bi * nq + kj[p], 0)
    vt_map = lambda bi, p, qi, kj: (0, bi * nq + kj[p])
    const = lambda bi, p, qi, kj: (0, 0)
    return pl.pallas_call(
        _attn_body,
        grid_spec=pltpu.PrefetchScalarGridSpec(
            num_scalar_prefetch=2, grid=(batch, len(pairs)),
            in_specs=[pl.BlockSpec((tq, N_HEADS * LANES), q_map),
                      pl.BlockSpec((tq, N_HEADS * LANES), k_map),
                      pl.BlockSpec((N_HEADS * V_HEAD_DIM, tq), vt_map),
                      pl.BlockSpec(km.shape, const), pl.BlockSpec(vmt.shape, const)],
            out_specs=pl.BlockSpec((tq, N_HEADS * V_HEAD_DIM), q_map),
            scratch_shapes=[pltpu.VMEM((N_HEADS, tq), F32), pltpu.VMEM((N_HEADS, tq), F32),
                            pltpu.VMEM((N_HEADS, V_HEAD_DIM, tq), F32)]),
        out_shape=jax.ShapeDtypeStruct((n, N_HEADS * V_HEAD_DIM), BF16),
        compiler_params=_cparams(("parallel", "arbitrary")),
        name="attn",
    )(qi, kj, q, k, vt, km, vmt)


def _tile_view_shape(rows):
    return (rows // SUBLANES, ROW_CHUNKS, SUBLANES, LANES)


def _to_row_tiles(ref, val):
    groups = val.shape[0] // SUBLANES
    for c in range(ROW_CHUNKS):
        ref[:, c] = val[:, c * LANES:(c + 1) * LANES].reshape(groups, SUBLANES, LANES)


def _from_row_tiles(ref):
    rows = ref.shape[0] * SUBLANES
    return jnp.concatenate([ref[:, c].reshape(rows, LANES) for c in range(ROW_CHUNKS)], axis=1)


def _row_of(ref, row_group, sublane):
    return ref.at[row_group, :, sublane, :]


def _route(hn, whi_ref, wlo_ref, rb_ref, idx_ref, gate_ref):
    x_hi = hn.astype(BF16)
    x_lo = (hn - x_hi.astype(F32)).astype(BF16)
    w_hi = whi_ref[...]
    logits = _dot_t(w_hi, x_hi) + _dot_t(w_hi, x_lo) + _dot_t(wlo_ref[...], x_hi) + rb_ref[...]

    e_iota = lax.broadcasted_iota(jnp.int32, logits.shape, 0).astype(F32)
    vals, idxs = [], []
    cur = logits
    for _ in range(TOP_K):
        m = jnp.max(cur, axis=0, keepdims=True)
        idx = jnp.min(jnp.where(cur == m, e_iota, float(N_EXPERTS)), axis=0, keepdims=True)
        vals.append(m)
        idxs.append(idx)
        cur = jnp.where(e_iota == idx, -jnp.inf, cur)
    exps = [jnp.exp(v - vals[0]) for v in vals]
    denom = exps[0] + exps[1] + exps[2] + exps[3]
    idx_ref[...] = jnp.concatenate(idxs, axis=0).astype(jnp.int32)
    gate_ref[...] = jnp.concatenate([e / denom for e in exps] + [jnp.zeros_like(denom)] * 4, axis=0)


def _mix_body(glu_ref, halo_ref, glum_ref, o_ref, g_ref, x_ref, dww_ref, dwb_ref, lnw_ref, lnb_ref,
              pw2_ref, wo_ref, wout_ref, fnw_ref, whi_ref, wlo_ref, rb_ref,
              h_ref, hn_ref, idx_ref, gate_ref, xpad_ref, shift_ref, *, tm, tiles_per_seq):
    i = pl.program_id(0)
    first = (i % tiles_per_seq) == 0
    meta_ctx = jnp.concatenate([jnp.zeros((HALO - N_META, CONV_DIM), F32), glum_ref[...]], axis=0)
    xpad_ref[0:HALO, :] = jnp.where(first, meta_ctx, halo_ref[...])
    xpad_ref[HALO:HALO + tm, :] = glu_ref[...]

    span = tm + HALO - SUBLANES
    for rho in range(1, SUBLANES):
        shift_ref[rho, 0:span, :] = xpad_ref[rho:rho + span, :]
    off = HALO - (CONV_WIDTH - 1)
    acc = jnp.zeros((tm, CONV_DIM), F32) + dwb_ref[...]
    for t in range(CONV_WIDTH):
        rho, base = (off + t) % SUBLANES, (off + t) // SUBLANES * SUBLANES
        if rho == 0:
            win = xpad_ref[base:base + tm, :]
        else:
            win = shift_ref[rho, base:base + tm, :]
        acc = acc + dww_ref[t:t + 1, :] * win

    mu = jnp.mean(acc, axis=-1, keepdims=True)
    xc = acc - mu
    y = xc * lax.rsqrt(jnp.mean(xc * xc, axis=-1, keepdims=True) + NORM_EPS) * lnw_ref[...] + lnb_ref[...]
    y = y * _sigmoid(y)
    y_conv = _dot(y.astype(BF16), pw2_ref[...])
    y_mla = _dot(o_ref[...], wo_ref[...])
    g = g_ref[...].astype(F32)
    mixed = g[:, :1024] * y_conv + g[:, 1024:] * y_mla
    h = x_ref[...] + _dot(mixed.astype(BF16), wout_ref[...])
    h_ref[...] = h
    hn = _rms(h, fnw_ref[...])
    _to_row_tiles(hn_ref, hn)
    _route(hn, whi_ref, wlo_ref, rb_ref, idx_ref, gate_ref)


def _mix_call(glu, glum, o, g, x2, weights, *, tm, tiles_per_seq):
    n, d = x2.shape
    full = lambda a: pl.BlockSpec(a.shape, lambda i: (0,) * a.ndim)
    row = lambda c: pl.BlockSpec((tm, c), lambda i: (i, 0))
    halo = pl.BlockSpec((HALO, CONV_DIM), lambda i: (jnp.maximum(i * (tm // HALO) - 1, 0), 0))
    return pl.pallas_call(
        functools.partial(_mix_body, tm=tm, tiles_per_seq=tiles_per_seq),
        grid=(n // tm,),
        in_specs=[row(CONV_DIM), halo, full(glum), row(N_HEADS * V_HEAD_DIM), row(C_GATE), row(d)]
                 + [full(w) for w in weights],
        out_specs=[row(d), pl.BlockSpec(_tile_view_shape(tm), lambda i: (i, 0, 0, 0)),
                   pl.BlockSpec((TOP_K, tm), lambda i: (0, i)), pl.BlockSpec((2 * TOP_K, tm), lambda i: (0, i))],
        out_shape=[jax.ShapeDtypeStruct((n, d), F32), jax.ShapeDtypeStruct(_tile_view_shape(n), F32),
                   jax.ShapeDtypeStruct((TOP_K, n), jnp.int32), jax.ShapeDtypeStruct((2 * TOP_K, n), F32)],
        scratch_shapes=[pltpu.VMEM((HALO + tm, CONV_DIM), F32),
                        pltpu.VMEM((SUBLANES, HALO + tm, CONV_DIM), F32)],
        compiler_params=_cparams(("parallel",)),
        name="mix",
    )(glu, glu, glum, o, g, x2, *weights)


def _rank_body(idx_ref, rank_ref, cnt_ref, carry_ref, *, t):
    i = pl.program_id(0)

    @pl.when(i == 0)
    def _init():
        carry_ref[...] = jnp.zeros_like(carry_ref)

    idx = idx_ref[...]
    e_iota = lax.broadcasted_iota(jnp.int32, (N_EXPERTS, t), 0)
    hits = [e_iota == idx[k:k + 1, :] for k in range(TOP_K)]
    onehot = jnp.zeros((N_EXPERTS, t), F32)
    for hk in hits:
        onehot = onehot + jnp.where(hk, 1.0, 0.0)
    r = lax.broadcasted_iota(jnp.int32, (t, t), 0)
    c = lax.broadcasted_iota(jnp.int32, (t, t), 1)
    before = jnp.where(r < c, 1.0, 0.0).astype(BF16)
    val = _dot(onehot.astype(BF16), before) + carry_ref[...]
    ranks = [jnp.sum(jnp.where(hk, val, 0.0), axis=0, keepdims=True) for hk in hits]
    rank_ref[...] = jnp.concatenate(ranks, axis=0).astype(jnp.int32)
    total = carry_ref[...] + jnp.sum(onehot, axis=1, keepdims=True)
    carry_ref[...] = total
    cnt_ref[...] = jnp.broadcast_to(total, cnt_ref.shape).astype(jnp.int32)


def _rank_call(idx, *, t):
    n = idx.shape[1]
    return pl.pallas_call(
        functools.partial(_rank_body, t=t),
        grid=(n // t,),
        in_specs=[pl.BlockSpec((TOP_K, t), lambda i: (0, i))],
        out_specs=[pl.BlockSpec((TOP_K, t), lambda i: (0, i)), pl.BlockSpec((N_EXPERTS, LANES), lambda i: (0, 0))],
        out_shape=[jax.ShapeDtypeStruct((TOP_K, n), jnp.int32), jax.ShapeDtypeStruct((N_EXPERTS, LANES), jnp.int32)],
        scratch_shapes=[pltpu.VMEM((N_EXPERTS, 1), F32)],
        compiler_params=_cparams(("arbitrary",)),
        name="rank",
    )(idx)


def _plan_body(cnt_ref, start_ref, bexp_ref, nact_ref, *, bm, n_blocks):
    def per_expert(e, carry):
        blk, last = carry
        c = cnt_ref[e]
        nb = (c + (bm - 1)) // bm
        start_ref[e] = blk * bm

        def fill(b, _):
            bexp_ref[b] = e
            return 0

        lax.fori_loop(blk, blk + nb, fill, 0)
        return blk + nb, jnp.where(nb > 0, e, last)

    n_act, last = lax.fori_loop(0, N_EXPERTS, per_expert, (jnp.int32(0), jnp.int32(0)))
    nact_ref[0] = n_act

    def tail(b, _):
        bexp_ref[b] = last
        return 0

    lax.fori_loop(n_act, n_blocks, tail, 0)


def _plan_call(cnt, *, bm, n_blocks):
    smem = lambda: pl.BlockSpec(memory_space=pltpu.SMEM)
    return pl.pallas_call(
        functools.partial(_plan_body, bm=bm, n_blocks=n_blocks),
        in_specs=[smem()],
        out_specs=[smem(), smem(), smem()],
        out_shape=[jax.ShapeDtypeStruct((N_EXPERTS,), jnp.int32), jax.ShapeDtypeStruct((n_blocks,), jnp.int32),
                   jax.ShapeDtypeStruct((1,), jnp.int32)],
        name="plan",
    )(cnt)


def _dest_body(start_ref, idx_ref, rank_ref, dest_ref, *, t):
    idx = idx_ref[...]
    base = jnp.zeros(idx.shape, jnp.int32)
    for e in range(N_EXPERTS):
        base = jnp.where(idx == e, start_ref[e], base)
    dest = base + rank_ref[...]
    for s in range(dest_ref.shape[0]):
        dest_ref[s] = jnp.concatenate([dest[k:k + 1, s * t:(s + 1) * t] for k in range(TOP_K)], axis=1)


def _dest_call(start, idx, rank, *, t, tb):
    n = idx.shape[1]
    blk = pl.BlockSpec((TOP_K, tb), lambda i, s: (0, i))
    return pl.pallas_call(
        functools.partial(_dest_body, t=t),
        grid_spec=pltpu.PrefetchScalarGridSpec(
            num_scalar_prefetch=1, grid=(n // tb,), in_specs=[blk, blk],
            out_specs=pl.BlockSpec((tb // t, 1, TOP_K * t), lambda i, s: (i, 0, 0))),
        out_shape=jax.ShapeDtypeStruct((n // t, 1, TOP_K * t), jnp.int32),
        compiler_params=_cparams(("parallel",)),
        name="dest",
    )(start, idx, rank)


WAIT_GROUP = 32


def _drain(copy, count):
    def group(_, carry):
        for _ in range(WAIT_GROUP):
            copy.wait()
        return carry

    lax.fori_loop(0, count // WAIT_GROUP, group, 0)


def _scatter_body(dest_ref, x_ref, xs_ref, sem, *, t):
    def issue(grp, carry):
        for s in range(SUBLANES):
            for k in range(TOP_K):
                d = dest_ref[0, 0, k * t + grp * SUBLANES + s]
                pltpu.make_async_copy(_row_of(x_ref, grp, s), _row_of(xs_ref, d >> 3, d & (SUBLANES - 1)),
                                      sem).start(priority=k % 2)
        return carry

    lax.fori_loop(0, t // SUBLANES, issue, 0)
    _drain(pltpu.make_async_copy(_row_of(x_ref, 0, 0), _row_of(xs_ref, 0, 0), sem), TOP_K * t)


def _scatter_call(dest, x, *, t, p_rows):
    n = x.shape[0] * SUBLANES
    return pl.pallas_call(
        functools.partial(_scatter_body, t=t),
        grid=(n // t,),
        in_specs=[pl.BlockSpec((1, 1, TOP_K * t), lambda i: (i, 0, 0), memory_space=pltpu.SMEM),
                  pl.BlockSpec(_tile_view_shape(t), lambda i: (i, 0, 0, 0))],
        out_specs=pl.BlockSpec(memory_space=pl.ANY),
        out_shape=jax.ShapeDtypeStruct(_tile_view_shape(p_rows), F32),
        scratch_shapes=[pltpu.SemaphoreType.DMA],
        compiler_params=_cparams(("arbitrary",)),
        name="scatter",
    )(dest, x)


def _expert_body(bexp_ref, nact_ref, xs_ref, wgu_ref, bgu_ref, wd_ref, bd_ref, ys_ref):
    b = pl.program_id(0)

    @pl.when(b < nact_ref[0])
    def _():
        x = _from_row_tiles(xs_ref).astype(BF16)
        gu = _dot(x, wgu_ref[0]) + bgu_ref[0]
        g = jnp.minimum(gu[:, :D_FF], SWIGLU_LIMIT)
        u = jnp.clip(gu[:, D_FF:], -SWIGLU_LIMIT, SWIGLU_LIMIT)
        act = (u + 1.0) * (g * _sigmoid(SWIGLU_ALPHA * g))
        _to_row_tiles(ys_ref, _dot(act.astype(BF16), wd_ref[0]) + bd_ref[0])


def _expert_call(bexp, nact, xs, wgu, bgu, wd, bd, *, bm):
    p_rows = xs.shape[0] * SUBLANES
    d = ROW_CHUNKS * LANES
    n_blocks = p_rows // bm
    rows = lambda b, be, na: (jnp.minimum(b, na[0] - 1), 0, 0, 0)
    wsel = lambda b, be, na: (be[b], 0, 0)
    return pl.pallas_call(
        _expert_body,
        grid_spec=pltpu.PrefetchScalarGridSpec(
            num_scalar_prefetch=2, grid=(n_blocks,),
            in_specs=[pl.BlockSpec(_tile_view_shape(bm), rows),
                      pl.BlockSpec((1, d, 2 * D_FF), wsel), pl.BlockSpec((1, 1, 2 * D_FF), wsel),
                      pl.BlockSpec((1, D_FF, d), wsel), pl.BlockSpec((1, 1, d), wsel)],
            out_specs=pl.BlockSpec(_tile_view_shape(bm), rows)),
        out_shape=jax.ShapeDtypeStruct(_tile_view_shape(p_rows), F32),
        compiler_params=_cparams(("arbitrary",)),
        name="experts",
    )(bexp, nact, xs, wgu, bgu, wd, bd)


def _combine_body(dest_ref, ys_ref, gate_ref, h_ref, out_ref, buf_ref, sem, *, t):
    def issue(grp, carry):
        for s in range(SUBLANES):
            for k in range(TOP_K):
                d = dest_ref[0, 0, k * t + grp * SUBLANES + s]
                pltpu.make_async_copy(_row_of(ys_ref, d >> 3, d & (SUBLANES - 1)), _row_of(buf_ref.at[k], grp, s),
                                      sem).start(priority=k % 2)
        return carry

    lax.fori_loop(0, t // SUBLANES, issue, 0)
    _drain(pltpu.make_async_copy(_row_of(ys_ref, 0, 0), _row_of(buf_ref.at[0], 0, 0), sem), TOP_K * t)

    gates = jnp.transpose(gate_ref[...])
    out = h_ref[...]
    for k in range(TOP_K):
        out = out + gates[:, k:k + 1] * _from_row_tiles(buf_ref.at[k])
    out_ref[...] = out


def _combine_call(dest, ys, gates, h, *, t):
    n, d = h.shape
    return pl.pallas_call(
        functools.partial(_combine_body, t=t),
        grid=(n // t,),
        in_specs=[pl.BlockSpec((1, 1, TOP_K * t), lambda i: (i, 0, 0), memory_space=pltpu.SMEM),
                  pl.BlockSpec(memory_space=pl.ANY),
                  pl.BlockSpec((2 * TOP_K, t), lambda i: (0, i)),
                  pl.BlockSpec((t, d), lambda i: (i, 0))],
        out_specs=pl.BlockSpec((t, d), lambda i: (i, 0)),
        out_shape=jax.ShapeDtypeStruct((n, d), F32),
        scratch_shapes=[pltpu.VMEM((TOP_K,) + _tile_view_shape(t), F32), pltpu.SemaphoreType.DMA],
        compiler_params=_cparams(("arbitrary",)),
        name="combine",
    )(dest, ys, gates, h)


def _rot_half(z):
    half = QK_ROPE_DIM // 2
    return jnp.concatenate([-z[:, half:], z[:, :half]], axis=1)


def _pe_slot(z):
    rows = z.shape[0]
    return jnp.concatenate([jnp.zeros((rows, QK_NOPE_DIM), F32), z,
                            jnp.zeros((rows, LANES - QK_HEAD_DIM), F32)], axis=1)


def _prep_proj_weights(attn_norm_w, w_in, q_a_norm_w, w_q_b, kv_a_norm_w, w_kv_b, q_norm_w, k_norm_w, gate_b):
    c0 = 2 * CONV_DIM
    c1 = c0 + Q_LORA_RANK
    c2 = c1 + KV_LORA_RANK
    c3 = c2 + QK_ROPE_DIM
    kpe = w_in[:, c2:c3]
    win = jnp.concatenate([w_in[:, :c2], _pe_slot(kpe), _pe_slot(_rot_half(kpe * k_norm_w[None, QK_NOPE_DIM:])),
                           w_in[:, c3:]], axis=1).astype(BF16)

    pad = jnp.zeros((Q_LORA_RANK, LANES - QK_HEAD_DIM), F32)
    q1, q2 = [], []
    for h in range(N_HEADS):
        cols = w_q_b[:, h * QK_HEAD_DIM:(h + 1) * QK_HEAD_DIM]
        q1.append(jnp.concatenate([cols, pad], axis=1))
        q2.append(_pe_slot(_rot_half(cols[:, QK_NOPE_DIM:] * q_norm_w[None, QK_NOPE_DIM:])))
    wq = jnp.concatenate(q1 + q2, axis=1).astype(BF16)

    kpad = jnp.zeros((KV_LORA_RANK, LANES - QK_NOPE_DIM), F32)
    ks, vs = [], []
    per_head = QK_NOPE_DIM + V_HEAD_DIM
    for h in range(N_HEADS):
        cols = w_kv_b[:, h * per_head:(h + 1) * per_head]
        ks.append(jnp.concatenate([cols[:, :QK_NOPE_DIM], kpad], axis=1))
        vs.append(cols[:, QK_NOPE_DIM:])
    wk = jnp.concatenate(ks, axis=1).astype(BF16)
    wvt = jnp.concatenate(vs, axis=1).T.astype(BF16)

    lane_pad = jnp.zeros((LANES - QK_HEAD_DIM,), F32)
    qlane = jnp.concatenate([q_norm_w, lane_pad])[None, :]
    klane = jnp.concatenate([k_norm_w, lane_pad])[None, :]
    return (attn_norm_w[None, :], win, q_a_norm_w[None, :], wq, kv_a_norm_w[None, :], wk, wvt, qlane, klane,
            gate_b[None, :])


def _rope_tables(length):
    half = QK_ROPE_DIM // 2
    inv_freq = ROPE_THETA ** (-jnp.arange(half, dtype=F32) / half)
    ang = jnp.arange(length, dtype=F32)[:, None] * inv_freq[None, :]
    cos, sin = jnp.cos(ang), jnp.sin(ang)
    ones = jnp.ones((length, QK_NOPE_DIM), F32)
    tail = LANES - QK_HEAD_DIM
    cos_t = jnp.concatenate([ones, cos, cos, jnp.ones((length, tail), F32)], axis=1)
    sin_t = jnp.concatenate([0.0 * ones, sin, sin, jnp.zeros((length, tail), F32)], axis=1)
    return cos_t, sin_t


def _tile(n, pref):
    t = pref
    while n % t:
        t //= 2
    return t


def kernel(x, meta_tokens, attn_norm_w, w_in, conv_dw_w, conv_dw_b, conv_ln_w, conv_ln_b, conv_pw2_w, q_a_norm_w, w_q_b, kv_a_norm_w, w_kv_b, q_norm_w, k_norm_w, w_o_mla, gate_b, w_out, ffn_norm_w, router_w, router_b, w_gate_up, b_gate_up, w_down, b_down):
    assert attn_norm_w.shape[0] == 1, "one layer: rows of meta tokens never feed a later layer"
    b, s, d = x.shape
    n = b * s
    x2 = x.reshape(n, d)

    tm = _tile(s, 256)
    tq = _tile(s, 512)
    cos_t, sin_t = _rope_tables(N_META + s)
    pw = _prep_proj_weights(attn_norm_w[0], w_in[0], q_a_norm_w[0], w_q_b[0], kv_a_norm_w[0], w_kv_b[0],
                            q_norm_w[0], k_norm_w[0], gate_b[0])

    tp = _tile(s, 512)
    glu, q, k, vt, g = _proj_call(x2, (cos_t[N_META:], sin_t[N_META:]), pw, tm=tp, tiles_per_seq=s // tp)
    glum, _, km, vmt, _ = _proj_call(meta_tokens.astype(F32), (cos_t[:N_META], sin_t[:N_META]), pw, tm=N_META,
                                     tiles_per_seq=1)

    o = _attn_call(q, k, vt, km, vmt, tq=tq, batch=b)

    rw = router_w[0].T
    rw_hi = rw.astype(BF16)
    rw_lo = (rw - rw_hi.astype(F32)).astype(BF16)
    mix_w = (conv_dw_w[0], conv_dw_b[0][None, :], conv_ln_w[0][None, :], conv_ln_b[0][None, :],
             conv_pw2_w[0].astype(BF16), w_o_mla[0].astype(BF16), w_out[0].astype(BF16), ffn_norm_w[0][None, :],
             rw_hi, rw_lo, router_b[0][:, None])
    h, hn, idx, gates = _mix_call(glu, glum, o, g, x2, mix_w, tm=tm, tiles_per_seq=s // tm)

    bm = 512
    n_blocks = (n * TOP_K) // bm + N_EXPERTS
    t_dma = _tile(n, 256)
    rank, cnt = _rank_call(idx, t=_tile(n, 512))
    start, bexp, nact = _plan_call(cnt[:, 0], bm=bm, n_blocks=n_blocks)
    dest = _dest_call(start, idx, rank, t=t_dma, tb=_tile(n, 2048))
    xs = _scatter_call(dest, hn, t=t_dma, p_rows=n_blocks * bm)
    ys = _expert_call(bexp, nact, xs, w_gate_up[0].astype(BF16), b_gate_up[0][:, None, :],
                      w_down[0].astype(BF16), b_down[0][:, None, :], bm=bm)
    out = _combine_call(dest, ys, gates, h, t=t_dma)
    return out.reshape(b, s, d)
```

```python
import functools
import math

import jax
import jax.numpy as jnp
from jax import lax
from jax.experimental import pallas as pl
from jax.experimental.pallas import tpu as pltpu

N_META = 16
CONV_DIM = 512
CONV_WIDTH = 31
N_HEADS = 8
QK_NOPE_DIM = 64
QK_ROPE_DIM = 32
QK_HEAD_DIM = QK_NOPE_DIM + QK_ROPE_DIM
V_HEAD_DIM = 64
Q_LORA_RANK = 256
KV_LORA_RANK = 128
ROPE_THETA = 10000.0
N_EXPERTS = 32
TOP_K = 4
D_FF = 1024
SWIGLU_LIMIT = 7.0
SWIGLU_ALPHA = 1.702
NORM_EPS = 1e-6
NEG_INF = -1e30

LANES = 128
SUBLANES = 8
ROW_CHUNKS = 8
SCORE_LOOKAHEAD = 2
HALO = 32
VMEM_LIMIT = 56 * 1024 * 1024

F32 = jnp.float32
BF16 = jnp.bfloat16


def _cparams(sem):
    return pltpu.CompilerParams(dimension_semantics=sem, vmem_limit_bytes=VMEM_LIMIT)


def _dot(a, b):
    return jnp.dot(a, b, preferred_element_type=F32)


def _dot_t(a, b):
    return lax.dot_general(a, b, (((1,), (1,)), ((), ())), preferred_element_type=F32)


def _sigmoid(x):
    return 1.0 / (1.0 + jnp.exp(-x))


def _rms(x, w):
    return x * lax.rsqrt(jnp.mean(x * x, axis=-1, keepdims=True) + NORM_EPS) * w


C_CONV = 2 * CONV_DIM
C_LAT = Q_LORA_RANK + KV_LORA_RANK + 2 * LANES
C_GATE = 2 * 1024


def _proj_body(x_ref, anw_ref, win_ref, qaw_ref, wq_ref, kvaw_ref, wk_ref, wvt_ref, qlane_ref, klane_ref,
               cos_ref, sin_ref, gb_ref, glu_ref, q_ref, k_ref, vt_ref, g_ref, *, q_scale):
    x = x_ref[...]
    xn = _rms(x, anw_ref[...]).astype(BF16)

    lat = _dot(xn, win_ref[:, C_CONV:C_CONV + C_LAT])
    q_lat = lat[:, :Q_LORA_RANK]
    c_kv = lat[:, Q_LORA_RANK:Q_LORA_RANK + KV_LORA_RANK]
    kpe = lat[:, Q_LORA_RANK + KV_LORA_RANK:Q_LORA_RANK + KV_LORA_RANK + LANES]
    kpe_rot = lat[:, Q_LORA_RANK + KV_LORA_RANK + LANES:]

    qn = _rms(q_lat, qaw_ref[...]).astype(BF16)
    cn = _rms(c_kv, kvaw_ref[...]).astype(BF16)
    qq = _dot(qn, wq_ref[...])
    kv = _dot(cn, wk_ref[...])
    vt_ref[...] = _dot_t(wvt_ref[...], cn).astype(vt_ref.dtype)
    u = _dot(xn, win_ref[:, 0:C_CONV])

    cos = cos_ref[...]
    sin = sin_ref[...]
    q_cos = qlane_ref[...] * cos
    for h in range(N_HEADS):
        q1 = qq[:, h * LANES:(h + 1) * LANES]
        q2 = qq[:, (N_HEADS + h) * LANES:(N_HEADS + h + 1) * LANES]
        ss = jnp.sum(q1 * q1, axis=-1, keepdims=True)
        s = lax.rsqrt(ss * (1.0 / QK_HEAD_DIM) + NORM_EPS) * q_scale
        q_ref[:, h * LANES:(h + 1) * LANES] = (s * (q1 * q_cos + q2 * sin)).astype(q_ref.dtype)

    gl = _dot(xn, win_ref[:, C_CONV + C_LAT:]) + gb_ref[...]

    k_cos = klane_ref[...] * cos
    ss_pe = jnp.sum(kpe * kpe, axis=-1, keepdims=True)
    k_pe_roped = kpe * k_cos + kpe_rot * sin
    for h in range(N_HEADS):
        kn = kv[:, h * LANES:(h + 1) * LANES]
        ss = jnp.sum(kn * kn, axis=-1, keepdims=True) + ss_pe
        s = lax.rsqrt(ss * (1.0 / QK_HEAD_DIM) + NORM_EPS)
        k_ref[:, h * LANES:(h + 1) * LANES] = (s * (kn * k_cos + k_pe_roped)).astype(k_ref.dtype)

    glu_ref[...] = u[:, :CONV_DIM] * _sigmoid(u[:, CONV_DIM:])
    g_ref[...] = _sigmoid(gl).astype(g_ref.dtype)


def _proj_call(x2, tables, weights, *, tm, tiles_per_seq):
    n = x2.shape[0]
    d = x2.shape[1]
    cos_t, sin_t = tables
    (anw, win, qaw, wq, kvaw, wk, wvt, qlane, klane, gb) = weights
    full = lambda a: pl.BlockSpec(a.shape, lambda i: (0,) * a.ndim)
    row = lambda c: pl.BlockSpec((tm, c), lambda i: (i, 0))
    pos = pl.BlockSpec((tm, LANES), lambda i: (i % tiles_per_seq, 0))
    q_scale = (QK_HEAD_DIM ** -0.5) * math.log2(math.e)
    return pl.pallas_call(
        functools.partial(_proj_body, q_scale=q_scale),
        grid=(n // tm,),
        in_specs=[row(d), full(anw), full(win), full(qaw), full(wq), full(kvaw), full(wk), full(wvt),
                  full(qlane), full(klane), pos, pos, full(gb)],
        out_specs=[row(CONV_DIM), row(N_HEADS * LANES), row(N_HEADS * LANES),
                   pl.BlockSpec((N_HEADS * V_HEAD_DIM, tm), lambda i: (0, i)), row(C_GATE)],
        out_shape=[jax.ShapeDtypeStruct((n, CONV_DIM), F32),
                   jax.ShapeDtypeStruct((n, N_HEADS * LANES), BF16),
                   jax.ShapeDtypeStruct((n, N_HEADS * LANES), BF16),
                   jax.ShapeDtypeStruct((N_HEADS * V_HEAD_DIM, n), BF16),
                   jax.ShapeDtypeStruct((n, C_GATE), BF16)],
        compiler_params=_cparams(("parallel",)),
        name="proj",
    )(x2, anw, win, qaw, wq, kvaw, wk, wvt, qlane, klane, cos_t, sin_t, gb)


def _attn_body(qi_ref, kj_ref, q_ref, k_ref, vt_ref, km_ref, vmt_ref, o_ref, m_ref, l_ref, acc_ref):
    p_id = pl.program_id(1)
    i = qi_ref[p_id]
    j = kj_ref[p_id]

    def head_slices(h):
        return slice(h * LANES, (h + 1) * LANES), slice(h * V_HEAD_DIM, (h + 1) * V_HEAD_DIM)

    @pl.when(j == 0)
    def _meta():
        ms, ls = [], []
        sts = [_dot_t(km_ref[:, head_slices(h)[0]], q_ref[:, head_slices(h)[0]]) for h in range(N_HEADS)]
        for h in range(N_HEADS):
            ks, vs = head_slices(h)
            st = sts[h]
            m = jnp.max(st, axis=0, keepdims=True)
            p = jnp.exp2(st - m)
            ms.append(m)
            ls.append(jnp.sum(p, axis=0, keepdims=True))
            acc_ref[h] = _dot(vmt_ref[vs, :], p.astype(BF16))
        m_ref[...] = jnp.concatenate(ms, axis=0)
        l_ref[...] = jnp.concatenate(ls, axis=0)

    def scores(h):
        ks, _ = head_slices(h)
        return _dot_t(k_ref[:, ks], q_ref[:, ks])

    def step(diagonal):
        m_all = m_ref[...]
        l_all = l_ref[...]
        ms, ls = [], []
        pending =[scores(h) for h in range(SCORE_LOOKAHEAD)]
        for h in range(N_HEADS):
            _, vs = head_slices(h)
            st = pending.pop(0)
            if h + SCORE_LOOKAHEAD < N_HEADS:
                pending.append(scores(h + SCORE_LOOKAHEAD))
            if diagonal:
                key = lax.broadcasted_iota(jnp.int32, st.shape, 0)
                qry = lax.broadcasted_iota(jnp.int32, st.shape, 1)
                st = jnp.where(key <= qry, st, NEG_INF)
            m_prev = m_all[h:h + 1, :]
            m_new = jnp.maximum(m_prev, jnp.max(st, axis=0, keepdims=True))
            alpha = jnp.exp2(m_prev - m_new)
            p = jnp.exp2(st - m_new)
            l_new = alpha * l_all[h:h + 1, :] + jnp.sum(p, axis=0, keepdims=True)
            acc = alpha * acc_ref[h] + _dot(vt_ref[vs, :], p.astype(BF16))
            if diagonal:
                acc_ref[h] = acc / l_new
            else:
                acc_ref[h] = acc
                ms.append(m_new)
                ls.append(l_new)
        if not diagonal:
            m_ref[...] = jnp.concatenate(ms, axis=0)
            l_ref[...] = jnp.concatenate(ls, axis=0)

    @pl.when(j < i)
    def _full():
        step(False)

    @pl.when(j == i)
    def _diag():
        step(True)
        ot = acc_ref[...].reshape(N_HEADS * V_HEAD_DIM, acc_ref.shape[-1])
        o_ref[...] = jnp.transpose(ot).astype(o_ref.dtype)


def _attn_call(q, k, vt, km, vmt, *, tq, batch):
    n = q.shape[0]
    nq = n // batch // tq
    pairs = [(i, j) for i in range(nq) for j in range(i + 1)]
    qi = jnp.asarray([p[0] for p in pairs], jnp.int32)
    kj = jnp.asarray([p[1] for p in pairs], jnp.int32)
    q_map = lambda bi, p, qi, kj: (bi * nq + qi[p], 0)
    k_map = lambda bi, p, qi, kj: (bi * nq + kj[p], 0)
    vt_map = lambda bi, p, qi, kj: (0, bi * nq + kj[p])
    const = lambda bi, p, qi, kj: (0, 0)
    return pl.pallas_call(
        _attn_body,
        grid_spec=pltpu.PrefetchScalarGridSpec(
            num_scalar_prefetch=2, grid=(batch, len(pairs)),
            in_specs=[pl.BlockSpec((tq, N_HEADS * LANES), q_map),
                      pl.BlockSpec((tq, N_HEADS * LANES), k_map),
                      pl.BlockSpec((N_HEADS * V_HEAD_DIM, tq), vt_map),
                      pl.BlockSpec(km.shape, const), pl.BlockSpec(vmt.shape, const)],
            out_specs=pl.BlockSpec((tq, N_HEADS * V_HEAD_DIM), q_map),
            scratch_shapes=[pltpu.VMEM((N_HEADS, tq), F32), pltpu.VMEM((N_HEADS, tq), F32),
                            pltpu.VMEM((N_HEADS, V_HEAD_DIM, tq), F32)]),
        out_shape=jax.ShapeDtypeStruct((n, N_HEADS * V_HEAD_DIM), BF16),
        compiler_params=_cparams(("parallel", "arbitrary")),
        name="attn",
    )(qi, kj, q, k, vt, km, vmt)


def _tile_view_shape(rows):
    return (rows // SUBLANES, ROW_CHUNKS, SUBLANES, LANES)


def _to_row_tiles(ref, val):
    groups = val.shape[0] // SUBLANES
    for c in range(ROW_CHUNKS):
        ref[:, c] = val[:, c * LANES:(c + 1) * LANES].reshape(groups, SUBLANES, LANES)


def _from_row_tiles(ref):
    rows = ref.shape[0] * SUBLANES
    return jnp.concatenate([ref[:, c].reshape(rows, LANES) for c in range(ROW_CHUNKS)], axis=1)


def _row_of(ref, row_group, sublane):
    return ref.at[row_group, :, sublane, :]


def _route(hn, whi_ref, wlo_ref, rb_ref, idx_ref, gate_ref):
    x_hi = hn.astype(BF16)
    x_lo = (hn - x_hi.astype(F32)).astype(BF16)
    w_hi = whi_ref[...]
    logits = _dot_t(w_hi, x_hi) + _dot_t(w_hi, x_lo) + _dot_t(wlo_ref[...], x_hi) + rb_ref[...]

    e_iota = lax.broadcasted_iota(jnp.int32, logits.shape, 0).astype(F32)
    vals, idxs = [], []
    cur = logits
    for _ in range(TOP_K):
        m = jnp.max(cur, axis=0, keepdims=True)
        idx = jnp.min(jnp.where(cur == m, e_iota, float(N_EXPERTS)), axis=0, keepdims=True)
        vals.append(m)
        idxs.append(idx)
        cur = jnp.where(e_iota == idx, -jnp.inf, cur)
    exps = [jnp.exp(v - vals[0]) for v in vals]
    denom = exps[0] + exps[1] + exps[2] + exps[3]
    idx_ref[...] = jnp.concatenate(idxs, axis=0).astype(jnp.int32)
    gate_ref[...] = jnp.concatenate([e / denom for e in exps] + [jnp.zeros_like(denom)] * 4, axis=0)


def _mix_body(glu_ref, halo_ref, glum_ref, o_ref, g_ref, x_ref, dww_ref, dwb_ref, lnw_ref, lnb_ref,
              pw2_ref, wo_ref, wout_ref, fnw_ref, whi_ref, wlo_ref, rb_ref,
              h_ref, hn_ref, idx_ref, gate_ref, xpad_ref, shift_ref, *, tm, tiles_per_seq):
    i = pl.program_id(0)
    y_mla = _dot(o_ref[...], wo_ref[...])
    first = (i % tiles_per_seq) == 0
    meta_ctx = jnp.concatenate([jnp.zeros((HALO - N_META, CONV_DIM), F32), glum_ref[...]], axis=0)
    xpad_ref[0:HALO, :] = jnp.where(first, meta_ctx, halo_ref[...])
    xpad_ref[HALO:HALO + tm, :] = glu_ref[...]

    span = tm + HALO - SUBLANES
    for rho in range(1, SUBLANES):
        shift_ref[rho, 0:span, :] = xpad_ref[rho:rho + span, :]
    off = HALO - (CONV_WIDTH - 1)
    acc = jnp.zeros((tm, CONV_DIM), F32) + dwb_ref[...]
    for t in range(CONV_WIDTH):
        rho, base = (off + t) % SUBLANES, (off + t) // SUBLANES * SUBLANES
        if rho == 0:
            win = xpad_ref[base:base + tm, :]
        else:
            win = shift_ref[rho, base:base + tm, :]
        acc = acc + dww_ref[t:t + 1, :] * win

    mu = jnp.mean(acc, axis=-1, keepdims=True)
    xc = acc - mu
    y = xc * lax.rsqrt(jnp.mean(xc * xc, axis=-1, keepdims=True) + NORM_EPS) * lnw_ref[...] + lnb_ref[...]
    y = y * _sigmoid(y)
    y_conv = _dot(y.astype(BF16), pw2_ref[...])
    g = g_ref[...].astype(F32)
    mixed = g[:, :1024] * y_conv + g[:, 1024:] * y_mla
    h = x_ref[...] + _dot(mixed.astype(BF16), wout_ref[...])
    h_ref[...] = h
    hn = _rms(h, fnw_ref[...])
    _to_row_tiles(hn_ref, hn)
    _route(hn, whi_ref, wlo_ref, rb_ref, idx_ref, gate_ref)


def _mix_call(glu, glum, o, g, x2, weights, *, tm, tiles_per_seq):
    n, d = x2.shape
    full = lambda a: pl.BlockSpec(a.shape, lambda i: (0,) * a.ndim)
    row = lambda c: pl.BlockSpec((tm, c), lambda i: (i, 0))
    halo = pl.BlockSpec((HALO, CONV_DIM), lambda i: (jnp.maximum(i * (tm // HALO) - 1, 0), 0))
    return pl.pallas_call(
        functools.partial(_mix_body, tm=tm, tiles_per_seq=tiles_per_seq),
        grid=(n // tm,),
        in_specs=[row(CONV_DIM), halo, full(glum), row(N_HEADS * V_HEAD_DIM), row(C_GATE), row(d)]
                 + [full(w) for w in weights],
        out_specs=[row(d), pl.BlockSpec(_tile_view_shape(tm), lambda i: (i, 0, 0, 0)),
                   pl.BlockSpec((TOP_K, tm), lambda i: (0, i)), pl.BlockSpec((2 * TOP_K, tm), lambda i: (0, i))],
        out_shape=[jax.ShapeDtypeStruct((n, d), F32), jax.ShapeDtypeStruct(_tile_view_shape(n), F32),
                   jax.ShapeDtypeStruct((TOP_K, n), jnp.int32), jax.ShapeDtypeStruct((2 * TOP_K, n), F32)],
        scratch_shapes=[pltpu.VMEM((HALO + tm, CONV_DIM), F32),
                        pltpu.VMEM((SUBLANES, HALO + tm, CONV_DIM), F32)],
        compiler_params=_cparams(("parallel",)),
        name="mix",
    )(glu, glu, glum, o, g, x2, *weights)


def _rank_body(idx_ref, rank_ref, cnt_ref, carry_ref, *, t):
    i = pl.program_id(0)

    @pl.when(i == 0)
    def _init():
        carry_ref[...] = jnp.zeros_like(carry_ref)

    idx = idx_ref[...]
    e_iota = lax.broadcasted_iota(jnp.int32, (N_EXPERTS, t), 0)
    hits = [e_iota == idx[k:k + 1, :] for k in range(TOP_K)]
    onehot = jnp.zeros((N_EXPERTS, t), F32)
    for hk in hits:
        onehot = onehot + jnp.where(hk, 1.0, 0.0)
    r = lax.broadcasted_iota(jnp.int32, (t, t), 0)
    c = lax.broadcasted_iota(jnp.int32, (t, t), 1)
    before = jnp.where(r < c, 1.0, 0.0).astype(BF16)
    val = _dot(onehot.astype(BF16), before) + carry_ref[...]
    ranks = [jnp.sum(jnp.where(hk, val, 0.0), axis=0, keepdims=True) for hk in hits]
    rank_ref[...] = jnp.concatenate(ranks, axis=0).astype(jnp.int32)
    total = carry_ref[...] + jnp.sum(onehot, axis=1, keepdims=True)
    carry_ref[...] = total
    cnt_ref[...] = jnp.broadcast_to(total, cnt_ref.shape).astype(jnp.int32)


def _rank_call(idx, *, t):
    n = idx.shape[1]
    return pl.pallas_call(
        functools.partial(_rank_body, t=t),
        grid=(n // t,),
        in_specs=[pl.BlockSpec((TOP_K, t), lambda i: (0, i))],
        out_specs=[pl.BlockSpec((TOP_K, t), lambda i: (0, i)), pl.BlockSpec((N_EXPERTS, LANES), lambda i: (0, 0))],
        out_shape=[jax.ShapeDtypeStruct((TOP_K, n), jnp.int32), jax.ShapeDtypeStruct((N_EXPERTS, LANES), jnp.int32)],
        scratch_shapes=[pltpu.VMEM((N_EXPERTS, 1), F32)],
        compiler_params=_cparams(("arbitrary",)),
        name="rank",
    )(idx)


def _plan_body(cnt_ref, start_ref, bexp_ref, nact_ref, *, bm, n_blocks):
    def per_expert(e, carry):
        blk, last = carry
        c = cnt_ref[e]
        nb = (c + (bm - 1)) // bm
        start_ref[e] = blk * bm

        def fill(b, _):
            bexp_ref[b] = e
            return 0

        lax.fori_loop(blk, blk + nb, fill, 0)
        return blk + nb, jnp.where(nb > 0, e, last)

    n_act, last = lax.fori_loop(0, N_EXPERTS, per_expert, (jnp.int32(0), jnp.int32(0)))
    nact_ref[0] = n_act

    def tail(b, _):
        bexp_ref[b] = last
        return 0

    lax.fori_loop(n_act, n_blocks, tail, 0)


def _plan_call(cnt, *, bm, n_blocks):
    smem = lambda: pl.BlockSpec(memory_space=pltpu.SMEM)
    return pl.pallas_call(
        functools.partial(_plan_body, bm=bm, n_blocks=n_blocks),
        in_specs=[smem()],
        out_specs=[smem(), smem(), smem()],
        out_shape=[jax.ShapeDtypeStruct((N_EXPERTS,), jnp.int32), jax.ShapeDtypeStruct((n_blocks,), jnp.int32),
                   jax.ShapeDtypeStruct((1,), jnp.int32)],
        name="plan",
    )(cnt)


def _dest_body(start_ref, idx_ref, rank_ref, dest_ref, *, t):
    idx = idx_ref[...]
    base = jnp.zeros(idx.shape, jnp.int32)
    for e in range(N_EXPERTS):
        base = jnp.where(idx == e, start_ref[e], base)
    dest = base + rank_ref[...]
    for s in range(dest_ref.shape[0]):
        dest_ref[s] = jnp.concatenate([dest[k:k + 1, s * t:(s + 1) * t] for k in range(TOP_K)], axis=1)


def _dest_call(start, idx, rank, *, t, tb):
    n = idx.shape[1]
    blk = pl.BlockSpec((TOP_K, tb), lambda i, s: (0, i))
    return pl.pallas_call(
        functools.partial(_dest_body, t=t),
        grid_spec=pltpu.PrefetchScalarGridSpec(
            num_scalar_prefetch=1, grid=(n // tb,), in_specs=[blk, blk],
            out_specs=pl.BlockSpec((tb // t, 1, TOP_K * t), lambda i, s: (i, 0, 0))),
        out_shape=jax.ShapeDtypeStruct((n // t, 1, TOP_K * t), jnp.int32),
        compiler_params=_cparams(("parallel",)),
        name="dest",
    )(start, idx, rank)


WAIT_GROUP = 32


def _drain(copy, count):
    def group(_, carry):
        for _ in range(WAIT_GROUP):
            copy.wait()
        return carry

    lax.fori_loop(0, count // WAIT_GROUP, group, 0)


def _scatter_body(dest_ref, x_ref, xs_ref, sem, *, t):
    def issue(grp, carry):
        for s in range(SUBLANES):
            for k in range(TOP_K):
                d = dest_ref[0, 0, k * t + grp * SUBLANES + s]
                pltpu.make_async_copy(_row_of(x_ref, grp, s), _row_of(xs_ref, d >> 3, d & (SUBLANES - 1)),
                                      sem).start(priority=k % 2)
        return carry

    lax.fori_loop(0, t // SUBLANES, issue, 0)
    _drain(pltpu.make_async_copy(_row_of(x_ref, 0, 0), _row_of(xs_ref, 0, 0), sem), TOP_K * t)


def _scatter_call(dest, x, *, t, p_rows):
    n = x.shape[0] * SUBLANES
    return pl.pallas_call(
        functools.partial(_scatter_body, t=t),
        grid=(n // t,),
        in_specs=[pl.BlockSpec((1, 1, TOP_K * t), lambda i: (i, 0, 0), memory_space=pltpu.SMEM),
                  pl.BlockSpec(_tile_view_shape(t), lambda i: (i, 0, 0, 0))],
        out_specs=pl.BlockSpec(memory_space=pl.ANY),
        out_shape=jax.ShapeDtypeStruct(_tile_view_shape(p_rows), F32),
        scratch_shapes=[pltpu.SemaphoreType.DMA],
        compiler_params=_cparams(("arbitrary",)),
        name="scatter",
    )(dest, x)


def _expert_body(bexp_ref, nact_ref, xs_ref, wgu_ref, bgu_ref, wd_ref, bd_ref, ys_ref):
    b = pl.program_id(0)

    @pl.when(b < nact_ref[0])
    def _():
        x = _from_row_tiles(xs_ref).astype(BF16)
        gu = _dot(x, wgu_ref[0]) + bgu_ref[0]
        g = jnp.minimum(gu[:, :D_FF], SWIGLU_LIMIT)
        u = jnp.clip(gu[:, D_FF:], -SWIGLU_LIMIT, SWIGLU_LIMIT)
        act = (u + 1.0) * (g * _sigmoid(SWIGLU_ALPHA * g))
        _to_row_tiles(ys_ref, _dot(act.astype(BF16), wd_ref[0]) + bd_ref[0])


def _expert_call(bexp, nact, xs, wgu, bgu, wd, bd, *, bm):
    p_rows = xs.shape[0] * SUBLANES
    d = ROW_CHUNKS * LANES
    n_blocks = p_rows // bm
    rows = lambda b, be, na: (jnp.minimum(b, na[0] - 1), 0, 0, 0)
    wsel = lambda b, be, na: (be[b], 0, 0)
    return pl.pallas_call(
        _expert_body,
        grid_spec=pltpu.PrefetchScalarGridSpec(
            num_scalar_prefetch=2, grid=(n_blocks,),
            in_specs=[pl.BlockSpec(_tile_view_shape(bm), rows),
                      pl.BlockSpec((1, d, 2 * D_FF), wsel), pl.BlockSpec((1, 1, 2 * D_FF), wsel),
                      pl.BlockSpec((1, D_FF, d), wsel), pl.BlockSpec((1, 1, d), wsel)],
            out_specs=pl.BlockSpec(_tile_view_shape(bm), rows)),
        out_shape=jax.ShapeDtypeStruct(_tile_view_shape(p_rows), F32),
        compiler_params=_cparams(("arbitrary",)),
        name="experts",
    )(bexp, nact, xs, wgu, bgu, wd, bd)


def _combine_body(dest_ref, ys_ref, gate_ref, h_ref, out_ref, buf_ref, sem, *, t):
    def issue(grp, carry):
        for s in range(SUBLANES):
            for k in range(TOP_K):
                d = dest_ref[0, 0, k * t + grp * SUBLANES + s]
                pltpu.make_async_copy(_row_of(ys_ref, d >> 3, d & (SUBLANES - 1)), _row_of(buf_ref.at[k], grp, s),
                                      sem).start(priority=k % 2)
        return carry

    lax.fori_loop(0, t // SUBLANES, issue, 0)
    _drain(pltpu.make_async_copy(_row_of(ys_ref, 0, 0), _row_of(buf_ref.at[0], 0, 0), sem), TOP_K * t)

    gates = jnp.transpose(gate_ref[...])
    out = h_ref[...]
    for k in range(TOP_K):
        out = out + gates[:, k:k + 1] * _from_row_tiles(buf_ref.at[k])
    out_ref[...] = out


def _combine_call(dest, ys, gates, h, *, t):
    n, d = h.shape
    return pl.pallas_call(
        functools.partial(_combine_body, t=t),
        grid=(n // t,),
        in_specs=[pl.BlockSpec((1, 1, TOP_K * t), lambda i: (i, 0, 0), memory_space=pltpu.SMEM),
                  pl.BlockSpec(memory_space=pl.ANY),
                  pl.BlockSpec((2 * TOP_K, t), lambda i: (0, i)),
                  pl.BlockSpec((t, d), lambda i: (i, 0))],
        out_specs=pl.BlockSpec((t, d), lambda i: (i, 0)),
        out_shape=jax.ShapeDtypeStruct((n, d), F32),
        scratch_shapes=[pltpu.VMEM((TOP_K,) + _tile_view_shape(t), F32), pltpu.SemaphoreType.DMA],
        compiler_params=_cparams(("arbitrary",)),
        name="combine",
    )(dest, ys, gates, h)


def _rot_half(z):
    half = QK_ROPE_DIM // 2
    return jnp.concatenate([-z[:, half:], z[:, :half]], axis=1)


def _pe_slot(z):
    rows = z.shape[0]
    return jnp.concatenate([jnp.zeros((rows, QK_NOPE_DIM), F32), z,
                            jnp.zeros((rows, LANES - QK_HEAD_DIM), F32)], axis=1)


def _prep_proj_weights(attn_norm_w, w_in, q_a_norm_w, w_q_b, kv_a_norm_w, w_kv_b, q_norm_w, k_norm_w, gate_b):
    c0 = 2 * CONV_DIM
    c1 = c0 + Q_LORA_RANK
    c2 = c1 + KV_LORA_RANK
    c3 = c2 + QK_ROPE_DIM
    kpe = w_in[:, c2:c3]
    win = jnp.concatenate([w_in[:, :c2], _pe_slot(kpe), _pe_slot(_rot_half(kpe * k_norm_w[None, QK_NOPE_DIM:])),
                           w_in[:, c3:]], axis=1).astype(BF16)

    pad = jnp.zeros((Q_LORA_RANK, LANES - QK_HEAD_DIM), F32)
    q1, q2 = [], []
    for h in range(N_HEADS):
        cols = w_q_b[:, h * QK_HEAD_DIM:(h + 1) * QK_HEAD_DIM]
        q1.append(jnp.concatenate([cols, pad], axis=1))
        q2.append(_pe_slot(_rot_half(cols[:, QK_NOPE_DIM:] * q_norm_w[None, QK_NOPE_DIM:])))
    wq = jnp.concatenate(q1 + q2, axis=1).astype(BF16)

    kpad = jnp.zeros((KV_LORA_RANK, LANES - QK_NOPE_DIM), F32)
    ks, vs = [], []
    per_head = QK_NOPE_DIM + V_HEAD_DIM
    for h in range(N_HEADS):
        cols = w_kv_b[:, h * per_head:(h + 1) * per_head]
        ks.append(jnp.concatenate([cols[:, :QK_NOPE_DIM], kpad], axis=1))
        vs.append(cols[:, QK_NOPE_DIM:])
    wk = jnp.concatenate(ks, axis=1).astype(BF16)
    wvt = jnp.concatenate(vs, axis=1).T.astype(BF16)

    lane_pad = jnp.zeros((LANES - QK_HEAD_DIM,), F32)
    qlane = jnp.concatenate([q_norm_w, lane_pad])[None, :]
    klane = jnp.concatenate([k_norm_w, lane_pad])[None, :]
    return (attn_norm_w[None, :], win, q_a_norm_w[None, :], wq, kv_a_norm_w[None, :], wk, wvt, qlane, klane,
            gate_b[None, :])


def _rope_tables(length):
    half = QK_ROPE_DIM // 2
    inv_freq = ROPE_THETA ** (-jnp.arange(half, dtype=F32) / half)
    ang = jnp.arange(length, dtype=F32)[:, None] * inv_freq[None, :]
    cos, sin = jnp.cos(ang), jnp.sin(ang)
    ones = jnp.ones((length, QK_NOPE_DIM), F32)
    tail = LANES - QK_HEAD_DIM
    cos_t = jnp.concatenate([ones, cos, cos, jnp.ones((length, tail), F32)], axis=1)
    sin_t = jnp.concatenate([0.0 * ones, sin, sin, jnp.zeros((length, tail), F32)], axis=1)
    return cos_t, sin_t


def _tile(n, pref):
    t = pref
    while n % t:
        t //= 2
    return t


def kernel(x, meta_tokens, attn_norm_w, w_in, conv_dw_w, conv_dw_b, conv_ln_w, conv_ln_b, conv_pw2_w, q_a_norm_w, w_q_b, kv_a_norm_w, w_kv_b, q_norm_w, k_norm_w, w_o_mla, gate_b, w_out, ffn_norm_w, router_w, router_b, w_gate_up, b_gate_up, w_down, b_down):
    assert attn_norm_w.shape[0] == 1, "one layer: rows of meta tokens never feed a later layer"
    b, s, d = x.shape
    n = b * s
    x2 = x.reshape(n, d)

    tm = _tile(s, 256)
    tq = _tile(s, 512)
    cos_t, sin_t = _rope_tables(N_META + s)
    pw = _prep_proj_weights(attn_norm_w[0], w_in[0], q_a_norm_w[0], w_q_b[0], kv_a_norm_w[0], w_kv_b[0],
                            q_norm_w[0], k_norm_w[0], gate_b[0])

    tp = _tile(s, 512)
    glu, q, k, vt, g = _proj_call(x2, (cos_t[N_META:], sin_t[N_META:]), pw, tm=tp, tiles_per_seq=s // tp)
    glum, _, km, vmt, _ = _proj_call(meta_tokens.astype(F32), (cos_t[:N_META], sin_t[:N_META]), pw, tm=N_META,
                                     tiles_per_seq=1)

    o = _attn_call(q, k, vt, km, vmt, tq=tq, batch=b)

    rw = router_w[0].T
    rw_hi = rw.astype(BF16)
    rw_lo = (rw - rw_hi.astype(F32)).astype(BF16)
    mix_w = (conv_dw_w[0], conv_dw_b[0][None, :], conv_ln_w[0][None, :], conv_ln_b[0][None, :],
             conv_pw2_w[0].astype(BF16), w_o_mla[0].astype(BF16), w_out[0].astype(BF16), ffn_norm_w[0][None, :],
             rw_hi, rw_lo, router_b[0][:, None])
    h, hn, idx, gates = _mix_call(glu, glum, o, g, x2, mix_w, tm=tm, tiles_per_seq=s // tm)

    bm = 512
    n_blocks = (n * TOP_K) // bm + N_EXPERTS
    t_dma = _tile(n, 256)
    rank, cnt = _rank_call(idx, t=_tile(n, 512))
    start, bexp, nact = _plan_call(cnt[:, 0], bm=bm, n_blocks=n_blocks)
    dest = _dest_call(start, idx, rank, t=t_dma, tb=_tile(n, 2048))
    xs = _scatter_call(dest, hn, t=t_dma, p_rows=n_blocks * bm)
    ys = _expert_call(bexp, nact, xs, w_gate_up[0].astype(BF16), b_gate_up[0][:, None, :],
                      w_down[0].astype(BF16), b_down[0][:, None, :], bm=bm)
    out = _combine_call(dest, ys, gates, h, t=t_dma)
    return out.reshape(b, s, d)
```

```python
import functools
import math

import jax
import jax.numpy as jnp
from jax import lax
from jax.experimental import pallas as pl
from jax.experimental.pallas import tpu as pltpu

N_META = 16
CONV_DIM = 512
CONV_WIDTH = 31
N_HEADS = 8
QK_NOPE_DIM = 64
QK_ROPE_DIM = 32
QK_HEAD_DIM = QK_NOPE_DIM + QK_ROPE_DIM
V_HEAD_DIM = 64
Q_LORA_RANK = 256
KV_LORA_RANK = 128
ROPE_THETA = 10000.0
N_EXPERTS = 32
TOP_K = 4
D_FF = 1024
SWIGLU_LIMIT = 7.0
SWIGLU_ALPHA = 1.702
NORM_EPS = 1e-6
NEG_INF = -1e30

LANES = 128
SUBLANES = 8
BF16_ROWS = 16
ROW_CHUNKS = 8
SCORE_LOOKAHEAD = 2
HALO = 32
VMEM_LIMIT = 56 * 1024 * 1024

F32 = jnp.float32
BF16 = jnp.bfloat16


def _cparams(sem):
    return pltpu.CompilerParams(dimension_semantics=sem, vmem_limit_bytes=VMEM_LIMIT)


def _dot(a, b):
    return jnp.dot(a, b, preferred_element_type=F32)


def _dot_t(a, b):
    return lax.dot_general(a, b, (((1,), (1,)), ((), ())), preferred_element_type=F32)


def _sigmoid(x):
    return 1.0 / (1.0 + jnp.exp(-x))


def _rms(x, w):
    return x * lax.rsqrt(jnp.mean(x * x, axis=-1, keepdims=True) + NORM_EPS) * w


C_CONV = 2 * CONV_DIM
C_LAT = Q_LORA_RANK + KV_LORA_RANK + 2 * LANES
C_GATE = 2 * 1024


def _proj_body(x_ref, anw_ref, win_ref, qaw_ref, wq_ref, kvaw_ref, wk_ref, wvt_ref, qlane_ref, klane_ref,
               cos_ref, sin_ref, gb_ref, glu_ref, q_ref, k_ref, vt_ref, g_ref, *, q_scale):
    x = x_ref[...]
    xn = _rms(x, anw_ref[...]).astype(BF16)

    lat = _dot(xn, win_ref[:, C_CONV:C_CONV + C_LAT])
    q_lat = lat[:, :Q_LORA_RANK]
    c_kv = lat[:, Q_LORA_RANK:Q_LORA_RANK + KV_LORA_RANK]
    kpe = lat[:, Q_LORA_RANK + KV_LORA_RANK:Q_LORA_RANK + KV_LORA_RANK + LANES]
    kpe_rot = lat[:, Q_LORA_RANK + KV_LORA_RANK + LANES:]

    qn = _rms(q_lat, qaw_ref[...]).astype(BF16)
    cn = _rms(c_kv, kvaw_ref[...]).astype(BF16)
    qq = _dot(qn, wq_ref[...])
    kv = _dot(cn, wk_ref[...])
    vt_ref[...] = _dot_t(wvt_ref[...], cn).astype(vt_ref.dtype)
    u = _dot(xn, win_ref[:, 0:C_CONV])

    cos = cos_ref[...]
    sin = sin_ref[...]
    q_cos = qlane_ref[...] * cos
    for h in range(N_HEADS):
        q1 = qq[:, h * LANES:(h + 1) * LANES]
        q2 = qq[:, (N_HEADS + h) * LANES:(N_HEADS + h + 1) * LANES]
        ss = jnp.sum(q1 * q1, axis=-1, keepdims=True)
        s = lax.rsqrt(ss * (1.0 / QK_HEAD_DIM) + NORM_EPS) * q_scale
        q_ref[:, h * LANES:(h + 1) * LANES] = (s * (q1 * q_cos + q2 * sin)).astype(q_ref.dtype)

    gl = _dot(xn, win_ref[:, C_CONV + C_LAT:]) + gb_ref[...]

    k_cos = klane_ref[...] * cos
    ss_pe = jnp.sum(kpe * kpe, axis=-1, keepdims=True)
    k_pe_roped = kpe * k_cos + kpe_rot * sin
    for h in range(N_HEADS):
        kn = kv[:, h * LANES:(h + 1) * LANES]
        ss = jnp.sum(kn * kn, axis=-1, keepdims=True) + ss_pe
        s = lax.rsqrt(ss * (1.0 / QK_HEAD_DIM) + NORM_EPS)
        k_ref[:, h * LANES:(h + 1) * LANES] = (s * (kn * k_cos + k_pe_roped)).astype(k_ref.dtype)

    glu_ref[...] = u[:, :CONV_DIM] * _sigmoid(u[:, CONV_DIM:])
    g_ref[...] = _sigmoid(gl).astype(g_ref.dtype)


def _proj_call(x2, tables, weights, *, tm, tiles_per_seq):
    n = x2.shape[0]
    d = x2.shape[1]
    cos_t, sin_t = tables
    (anw, win, qaw, wq, kvaw, wk, wvt, qlane, klane, gb) = weights
    full = lambda a: pl.BlockSpec(a.shape, lambda i: (0,) * a.ndim)
    row = lambda c: pl.BlockSpec((tm, c), lambda i: (i, 0))
    pos = pl.BlockSpec((tm, LANES), lambda i: (i % tiles_per_seq, 0))
    q_scale = (QK_HEAD_DIM ** -0.5) * math.log2(math.e)
    return pl.pallas_call(
        functools.partial(_proj_body, q_scale=q_scale),
        grid=(n // tm,),
        in_specs=[row(d), full(anw), full(win), full(qaw), full(wq), full(kvaw), full(wk), full(wvt),
                  full(qlane), full(klane), pos, pos, full(gb)],
        out_specs=[row(CONV_DIM), row(N_HEADS * LANES), row(N_HEADS * LANES),
                   pl.BlockSpec((N_HEADS * V_HEAD_DIM, tm), lambda i: (0, i)), row(C_GATE)],
        out_shape=[jax.ShapeDtypeStruct((n, CONV_DIM), F32),
                   jax.ShapeDtypeStruct((n, N_HEADS * LANES), BF16),
                   jax.ShapeDtypeStruct((n, N_HEADS * LANES), BF16),
                   jax.ShapeDtypeStruct((N_HEADS * V_HEAD_DIM, n), BF16),
                   jax.ShapeDtypeStruct((n, C_GATE), BF16)],
        compiler_params=_cparams(("parallel",)),
        name="proj",
    )(x2, anw, win, qaw, wq, kvaw, wk, wvt, qlane, klane, cos_t, sin_t, gb)


def _attn_body(q_ref, k_ref, vt_ref, km_ref, vmt_ref, o_ref, m_ref, l_ref, acc_ref):
    i = pl.program_id(1)
    tq = q_ref.shape[0]

    def head_slices(h):
        return slice(h * LANES, (h + 1) * LANES), slice(h * V_HEAD_DIM, (h + 1) * V_HEAD_DIM)

    def _meta():
        ms, ls = [], []
        sts = [_dot_t(km_ref[:, head_slices(h)[0]], q_ref[:, head_slices(h)[0]]) for h in range(N_HEADS)]
        for h in range(N_HEADS):
            ks, vs = head_slices(h)
            st = sts[h]
            m = jnp.max(st, axis=0, keepdims=True)
            p = jnp.exp2(st - m)
            ms.append(m)
            ls.append(jnp.sum(p, axis=0, keepdims=True))
            acc_ref[h] = _dot(vmt_ref[vs, :], p.astype(BF16))
        m_ref[...] = jnp.concatenate(ms, axis=0)
        l_ref[...] = jnp.concatenate(ls, axis=0)

    def step(j, diagonal):
        keys = pl.ds(pl.multiple_of(j * tq, tq), tq)

        def scores(h):
            ks, _ = head_slices(h)
            return _dot_t(k_ref[keys, ks], q_ref[:, ks])

        m_all = m_ref[...]
        l_all = l_ref[...]
        ms, ls = [], []
        pending = [scores(h) for h in range(SCORE_LOOKAHEAD)]
        for h in range(N_HEADS):
            _, vs = head_slices(h)
            st = pending.pop(0)
            if h + SCORE_LOOKAHEAD < N_HEADS:
                pending.append(scores(h + SCORE_LOOKAHEAD))
            if diagonal:
                key = lax.broadcasted_iota(jnp.int32, st.shape, 0)
                qry = lax.broadcasted_iota(jnp.int32, st.shape, 1)
                st = jnp.where(key <= qry, st, NEG_INF)
            m_prev = m_all[h:h + 1, :]
            m_new = jnp.maximum(m_prev, jnp.max(st, axis=0, keepdims=True))
            alpha = jnp.exp2(m_prev - m_new)
            p = jnp.exp2(st - m_new)
            l_new = alpha * l_all[h:h + 1, :] + jnp.sum(p, axis=0, keepdims=True)
            acc = alpha * acc_ref[h] + _dot(vt_ref[vs, keys], p.astype(BF16))
            if diagonal:
                acc_ref[h] = acc / l_new
            else:
                acc_ref[h] = acc
                ms.append(m_new)
                ls.append(l_new)
        if not diagonal:
            m_ref[...] = jnp.concatenate(ms, axis=0)
            l_ref[...] = jnp.concatenate(ls, axis=0)

    def full_step(j, carry):
        step(j, False)
        return carry

    _meta()
    lax.fori_loop(0, i, full_step, 0)
    step(i, True)
    ot = acc_ref[...].reshape(N_HEADS * V_HEAD_DIM, tq)
    o_ref[...] = jnp.transpose(ot).astype(o_ref.dtype)


def _attn_call(q, k, vt, km, vmt, *, tq, batch):
    n = q.shape[0]
    s = n // batch
    nq = s // tq
    const = lambda bi, i: (0, 0)
    return pl.pallas_call(
        _attn_body,
        grid=(batch, nq),
        in_specs=[pl.BlockSpec((tq, N_HEADS * LANES), lambda bi, i: (bi * nq + i, 0)),
                  pl.BlockSpec((s, N_HEADS * LANES), lambda bi, i: (bi, 0)),
                  pl.BlockSpec((N_HEADS * V_HEAD_DIM, s), lambda bi, i: (0, bi)),
                  pl.BlockSpec(km.shape, const), pl.BlockSpec(vmt.shape, const)],
        out_specs=pl.BlockSpec((tq, N_HEADS * V_HEAD_DIM), lambda bi, i: (bi * nq + i, 0)),
        out_shape=jax.ShapeDtypeStruct((n, N_HEADS * V_HEAD_DIM), BF16),
        scratch_shapes=[pltpu.VMEM((N_HEADS, tq), F32), pltpu.VMEM((N_HEADS, tq), F32),
                        pltpu.VMEM((N_HEADS, V_HEAD_DIM, tq), F32)],
        compiler_params=_cparams(("parallel", "arbitrary")),
        name="attn",
    )(q, k, vt, km, vmt)


def _tile_view_shape(rows):
    return (rows // SUBLANES, ROW_CHUNKS, SUBLANES, LANES)


def _to_row_tiles(ref, val):
    groups = val.shape[0] // SUBLANES
    for c in range(ROW_CHUNKS):
        ref[:, c] = val[:, c * LANES:(c + 1) * LANES].reshape(groups, SUBLANES, LANES)


def _from_row_tiles(ref):
    rows = ref.shape[0] * SUBLANES
    return jnp.concatenate([ref[:, c].reshape(rows, LANES) for c in range(ROW_CHUNKS)], axis=1)


def _row_of(ref, row_group, sublane):
    return ref.at[row_group, :, sublane, :]


def _route(hn, whi_ref, wlo_ref, rb_ref, idx_ref, gate_ref):
    x_hi = hn.astype(BF16)
    x_lo = (hn - x_hi.astype(F32)).astype(BF16)
    w_hi = whi_ref[...]
    logits = _dot_t(w_hi, x_hi) + _dot_t(w_hi, x_lo) + _dot_t(wlo_ref[...], x_hi) + rb_ref[...]

    e_iota = lax.broadcasted_iota(jnp.int32, logits.shape, 0).astype(F32)
    vals, idxs = [], []
    cur = logits
    for _ in range(TOP_K):
        m = jnp.max(cur, axis=0, keepdims=True)
        idx = jnp.min(jnp.where(cur == m, e_iota, float(N_EXPERTS)), axis=0, keepdims=True)
        vals.append(m)
        idxs.append(idx)
        cur = jnp.where(e_iota == idx, -jnp.inf, cur)
    exps = [jnp.exp(v - vals[0]) for v in vals]
    denom = exps[0] + exps[1] + exps[2] + exps[3]
    idx_ref[...] = jnp.concatenate(idxs, axis=0).astype(jnp.int32)
    gate_ref[...] = jnp.concatenate([e / denom for e in exps] + [jnp.zeros_like(denom)] * 4, axis=0)


def _mix_body(glu_ref, halo_ref, glum_ref, o_ref, g_ref, x_ref, dww_ref, dwb_ref, lnw_ref, lnb_ref,
              pw2_ref, wo_ref, wout_ref, fnw_ref, whi_ref, wlo_ref, rb_ref,
              h_ref, hn_ref, idx_ref, gate_ref, xpad_ref, shift_ref, *, tm, tiles_per_seq):
    i = pl.program_id(0)
    y_mla = _dot(o_ref[...], wo_ref[...])
    first = (i % tiles_per_seq) == 0
    meta_ctx = jnp.concatenate([jnp.zeros((HALO - N_META, CONV_DIM), F32), glum_ref[...]], axis=0)
    xpad_ref[0:HALO, :] = jnp.where(first, meta_ctx, halo_ref[...])
    xpad_ref[HALO:HALO + tm, :] = glu_ref[...]

    span = tm + HALO - SUBLANES
    for rho in range(1, SUBLANES):
        shift_ref[rho, 0:span, :] = xpad_ref[rho:rho + span, :]
    off = HALO - (CONV_WIDTH - 1)
    acc = jnp.zeros((tm, CONV_DIM), F32) + dwb_ref[...]
    for t in range(CONV_WIDTH):
        rho, base = (off + t) % SUBLANES, (off + t) // SUBLANES * SUBLANES
        if rho == 0:
            win = xpad_ref[base:base + tm, :]
        else:
            win = shift_ref[rho, base:base + tm, :]
        acc = acc + dww_ref[t:t + 1, :] * win

    mu = jnp.mean(acc, axis=-1, keepdims=True)
    xc = acc - mu
    y = xc * lax.rsqrt(jnp.mean(xc * xc, axis=-1, keepdims=True) + NORM_EPS) * lnw_ref[...] + lnb_ref[...]
    y = y * _sigmoid(y)
    y_conv = _dot(y.astype(BF16), pw2_ref[...])
    g = g_ref[...].astype(F32)
    mixed = g[:, :1024] * y_conv + g[:, 1024:] * y_mla
    h = x_ref[...] + _dot(mixed.astype(BF16), wout_ref[...])
    h_ref[...] = h
    hn = _rms(h, fnw_ref[...])
    _to_row_tiles(hn_ref, hn)
    _route(hn, whi_ref, wlo_ref, rb_ref, idx_ref, gate_ref)


def _mix_call(glu, glum, o, g, x2, weights, *, tm, tiles_per_seq):
    n, d = x2.shape
    full = lambda a: pl.BlockSpec(a.shape, lambda i: (0,) * a.ndim)
    row = lambda c: pl.BlockSpec((tm, c), lambda i: (i, 0))
    halo = pl.BlockSpec((HALO, CONV_DIM), lambda i: (jnp.maximum(i * (tm // HALO) - 1, 0), 0))
    return pl.pallas_call(
        functools.partial(_mix_body, tm=tm, tiles_per_seq=tiles_per_seq),
        grid=(n // tm,),
        in_specs=[row(CONV_DIM), halo, full(glum), row(N_HEADS * V_HEAD_DIM), row(C_GATE), row(d)]
                 + [full(w) for w in weights],
        out_specs=[row(d), pl.BlockSpec(_tile_view_shape(tm), lambda i: (i, 0, 0, 0)),
                   pl.BlockSpec((TOP_K, tm), lambda i: (0, i)), pl.BlockSpec((2 * TOP_K, tm), lambda i: (0, i))],
        out_shape=[jax.ShapeDtypeStruct((n, d), F32), jax.ShapeDtypeStruct(_tile_view_shape(n), F32),
                   jax.ShapeDtypeStruct((TOP_K, n), jnp.int32), jax.ShapeDtypeStruct((2 * TOP_K, n), F32)],
        scratch_shapes=[pltpu.VMEM((HALO + tm, CONV_DIM), F32),
                        pltpu.VMEM((SUBLANES, HALO + tm, CONV_DIM), F32)],
        compiler_params=_cparams(("parallel",)),
        name="mix",
    )(glu, glu, glum, o, g, x2, *weights)


def _rank_body(idx_ref, rank_ref, cnt_ref, carry_ref, *, t):
    i = pl.program_id(0)

    @pl.when(i == 0)
    def _init():
        carry_ref[...] = jnp.zeros_like(carry_ref)

    idx = idx_ref[...]
    e_iota = lax.broadcasted_iota(jnp.int32, (N_EXPERTS, t), 0)
    hits = [e_iota == idx[k:k + 1, :] for k in range(TOP_K)]
    onehot = jnp.zeros((N_EXPERTS, t), F32)
    for hk in hits:
        onehot = onehot + jnp.where(hk, 1.0, 0.0)
    r = lax.broadcasted_iota(jnp.int32, (t, t), 0)
    c = lax.broadcasted_iota(jnp.int32, (t, t), 1)
    before = jnp.where(r < c, 1.0, 0.0).astype(BF16)
    val = _dot(onehot.astype(BF16), before) + carry_ref[...]
    ranks = [jnp.sum(jnp.where(hk, val, 0.0), axis=0, keepdims=True) for hk in hits]
    rank_ref[...] = jnp.concatenate(ranks, axis=0).astype(jnp.int32)
    total = carry_ref[...] + jnp.sum(onehot, axis=1, keepdims=True)
    carry_ref[...] = total
    cnt_ref[...] = jnp.broadcast_to(total, cnt_ref.shape).astype(jnp.int32)


def _rank_call(idx, *, t):
    n = idx.shape[1]
    return pl.pallas_call(
        functools.partial(_rank_body, t=t),
        grid=(n // t,),
        in_specs=[pl.BlockSpec((TOP_K, t), lambda i: (0, i))],
        out_specs=[pl.BlockSpec((TOP_K, t), lambda i: (0, i)), pl.BlockSpec((N_EXPERTS, LANES), lambda i: (0, 0))],
        out_shape=[jax.ShapeDtypeStruct((TOP_K, n), jnp.int32), jax.ShapeDtypeStruct((N_EXPERTS, LANES), jnp.int32)],
        scratch_shapes=[pltpu.VMEM((N_EXPERTS, 1), F32)],
        compiler_params=_cparams(("arbitrary",)),
        name="rank",
    )(idx)


def _plan_body(cnt_ref, start_ref, bexp_ref, nact_ref, *, bm, n_blocks):
    def per_expert(e, carry):
        blk, last = carry
        c = cnt_ref[e]
        nb = (c + (bm - 1)) // bm
        start_ref[e] = blk * bm

        def fill(b, _):
            bexp_ref[b] = e
            return 0

        lax.fori_loop(blk, blk + nb, fill, 0)
        return blk + nb, jnp.where(nb > 0, e, last)

    n_act, last = lax.fori_loop(0, N_EXPERTS, per_expert, (jnp.int32(0), jnp.int32(0)))
    nact_ref[0] = n_act

    def tail(b, _):
        bexp_ref[b] = last
        return 0

    lax.fori_loop(n_act, n_blocks, tail, 0)


def _plan_call(cnt, *, bm, n_blocks):
    smem = lambda: pl.BlockSpec(memory_space=pltpu.SMEM)
    return pl.pallas_call(
        functools.partial(_plan_body, bm=bm, n_blocks=n_blocks),
        in_specs=[smem()],
        out_specs=[smem(), smem(), smem()],
        out_shape=[jax.ShapeDtypeStruct((N_EXPERTS,), jnp.int32), jax.ShapeDtypeStruct((n_blocks,), jnp.int32),
                   jax.ShapeDtypeStruct((1,), jnp.int32)],
        name="plan",
    )(cnt)


def _dest_body(start_ref, idx_ref, rank_ref, dest_ref, *, t):
    idx = idx_ref[...]
    base = jnp.zeros(idx.shape, jnp.int32)
    for e in range(N_EXPERTS):
        base = jnp.where(idx == e, start_ref[e], base)
    dest = base + rank_ref[...]
    for s in range(dest_ref.shape[0]):
        dest_ref[s] = jnp.concatenate([dest[k:k + 1, s * t:(s + 1) * t] for k in range(TOP_K)], axis=1)


def _dest_call(start, idx, rank, *, t, tb):
    n = idx.shape[1]
    blk = pl.BlockSpec((TOP_K, tb), lambda i, s: (0, i))
    return pl.pallas_call(
        functools.partial(_dest_body, t=t),
        grid_spec=pltpu.PrefetchScalarGridSpec(
            num_scalar_prefetch=1, grid=(n // tb,), in_specs=[blk, blk],
            out_specs=pl.BlockSpec((tb // t, 1, TOP_K * t), lambda i, s: (i, 0, 0))),
        out_shape=jax.ShapeDtypeStruct((n // t, 1, TOP_K * t), jnp.int32),
        compiler_params=_cparams(("parallel",)),
        name="dest",
    )(start, idx, rank)


DMA_STEPS = 128
WAIT_GROUP = 32


def _drain(copy, count):
    def group(_, carry):
        for _ in range(WAIT_GROUP):
            copy.wait()
        return carry

    lax.fori_loop(0, count // WAIT_GROUP, group, 0)


def _scatter_body(dest_ref, x_ref, wgu_ref, wd_ref, xs_ref, wgu_out, wd_out, sem, *, t, iters):
    groups = t // SUBLANES // iters
    cast_rows = wgu_ref.shape[0] // iters

    def issue(it, carry):
        for g in range(groups):
            grp = it * groups + g
            for s in range(SUBLANES):
                for k in range(TOP_K):
                    d = dest_ref[0, 0, k * t + grp * SUBLANES + s]
                    pltpu.make_async_copy(_row_of(x_ref, grp, s), _row_of(xs_ref, d >> 3, d & (SUBLANES - 1)),
                                          sem).start(priority=k % 2)
        rows = pl.ds(pl.multiple_of(it * cast_rows, cast_rows), cast_rows)
        wgu_out[rows, :] = wgu_ref[rows, :].astype(BF16)
        wd_out[rows, :] = wd_ref[rows, :].astype(BF16)
        return carry

    lax.fori_loop(0, iters, issue, 0)
    _drain(pltpu.make_async_copy(_row_of(x_ref, 0, 0), _row_of(xs_ref, 0, 0), sem), TOP_K * t)


def _scatter_call(dest, x, wgu, wd, *, t, p_rows):
    n = x.shape[0] * SUBLANES
    steps = n // t
    wgu2 = wgu.reshape(-1, wgu.shape[-1])
    wd2 = wd.reshape(-1, wd.shape[-1])
    w_rows = wgu2.shape[0] // steps
    assert wgu2.shape[0] == wd2.shape[0] == w_rows * steps
    iters = max(t // SUBLANES // 2, 1)
    assert (w_rows // iters) % BF16_ROWS == 0 and w_rows % iters == 0
    wblk = lambda a: pl.BlockSpec((w_rows, a.shape[1]), lambda i: (i, 0))
    xs, wgu_b, wd_b = pl.pallas_call(
        functools.partial(_scatter_body, t=t, iters=iters),
        grid=(steps,),
        in_specs=[pl.BlockSpec((1, 1, TOP_K * t), lambda i: (i, 0, 0), memory_space=pltpu.SMEM),
                  pl.BlockSpec(_tile_view_shape(t), lambda i: (i, 0, 0, 0)), wblk(wgu2), wblk(wd2)],
        out_specs=[pl.BlockSpec(memory_space=pl.ANY), wblk(wgu2), wblk(wd2)],
        out_shape=[jax.ShapeDtypeStruct(_tile_view_shape(p_rows), F32),
                   jax.ShapeDtypeStruct(wgu2.shape, BF16), jax.ShapeDtypeStruct(wd2.shape, BF16)],
        scratch_shapes=[pltpu.SemaphoreType.DMA],
        compiler_params=_cparams(("arbitrary",)),
        name="scatter",
    )(dest, x, wgu2, wd2)
    return xs, wgu_b.reshape(wgu.shape), wd_b.reshape(wd.shape)


def _expert_body(bexp_ref, nact_ref, xs_ref, wgu_ref, bgu_ref, wd_ref, bd_ref, ys_ref):
    b = pl.program_id(0)

    @pl.when(b < nact_ref[0])
    def _():
        x = _from_row_tiles(xs_ref).astype(BF16)
        gu = _dot(x, wgu_ref[0]) + bgu_ref[0]
        g = jnp.minimum(gu[:, :D_FF], SWIGLU_LIMIT)
        u = jnp.clip(gu[:, D_FF:], -SWIGLU_LIMIT, SWIGLU_LIMIT)
        act = (u + 1.0) * (g * _sigmoid(SWIGLU_ALPHA * g))
        _to_row_tiles(ys_ref, _dot(act.astype(BF16), wd_ref[0]) + bd_ref[0])


def _expert_call(bexp, nact, xs, wgu, bgu, wd, bd, *, bm):
    p_rows = xs.shape[0] * SUBLANES
    d = ROW_CHUNKS * LANES
    n_blocks = p_rows // bm
    rows = lambda b, be, na: (jnp.minimum(b, na[0] - 1), 0, 0, 0)
    wsel = lambda b, be, na: (be[b], 0, 0)
    return pl.pallas_call(
        _expert_body,
        grid_spec=pltpu.PrefetchScalarGridSpec(
            num_scalar_prefetch=2, grid=(n_blocks,),
            in_specs=[pl.BlockSpec(_tile_view_shape(bm), rows),
                      pl.BlockSpec((1, d, 2 * D_FF), wsel), pl.BlockSpec((1, 1, 2 * D_FF), wsel),
                      pl.BlockSpec((1, D_FF, d), wsel), pl.BlockSpec((1, 1, d), wsel)],
            out_specs=pl.BlockSpec(_tile_view_shape(bm), rows)),
        out_shape=jax.ShapeDtypeStruct(_tile_view_shape(p_rows), F32),
        compiler_params=_cparams(("arbitrary",)),
        name="experts",
    )(bexp, nact, xs, wgu, bgu, wd, bd)


def _combine_body(dest_ref, ys_ref, gate_ref, h_ref, out_ref, buf_ref, gt_ref, sems, *, t, steps):
    i = pl.program_id(0)
    slot = i % 2
    other = 1 - slot

    def row_gather(d, k, grp, s, sl):
        return pltpu.make_async_copy(_row_of(ys_ref, d >> 3, d & (SUBLANES - 1)),
                                     _row_of(buf_ref.at[sl, k], grp, s), sems.at[sl])

    @pl.when(i == 0)
    def _first():
        buf_ref[...] = jnp.zeros_like(buf_ref)

    @pl.when(i > 0)
    def _wait_previous_tile():
        _drain(row_gather(0, 0, 0, 0, other), TOP_K * t)

    gt_ref[...] = jnp.transpose(gate_ref[...])

    def body(grp, carry):
        for s in range(SUBLANES):
            for k in range(TOP_K):
                d = dest_ref[0, 0, k * t + grp * SUBLANES + s]
                row_gather(d, k, grp, s, slot).start(priority=k % 2)
        rows = pl.ds(pl.multiple_of(grp * SUBLANES, SUBLANES), SUBLANES)
        acc = h_ref[rows, :]
        g = gt_ref[rows, :]
        for k in range(TOP_K):
            tiles = buf_ref[other, k, grp]
            acc = acc + g[:, k:k + 1] * jnp.concatenate([tiles[c] for c in range(ROW_CHUNKS)], axis=1)
        out_ref[rows, :] = acc
        return carry

    lax.fori_loop(0, t // SUBLANES, body, 0)

    @pl.when(i == steps)
    def _drain_extra_gather():
        _drain(row_gather(0, 0, 0, 0, slot), TOP_K * t)


def _combine_call(dest, ys, gates, h, *, t):
    n, d = h.shape
    steps = n // t
    prev = lambda i: jnp.maximum(i - 1, 0)
    return pl.pallas_call(
        functools.partial(_combine_body, t=t, steps=steps),
        grid=(steps + 1,),
        in_specs=[pl.BlockSpec((1, 1, TOP_K * t), lambda i: (jnp.minimum(i, steps - 1), 0, 0),
                               memory_space=pltpu.SMEM),
                  pl.BlockSpec(memory_space=pl.ANY),
                  pl.BlockSpec((2 * TOP_K, t), lambda i: (0, prev(i))),
                  pl.BlockSpec((t, d), lambda i: (prev(i), 0))],
        out_specs=pl.BlockSpec((t, d), lambda i: (prev(i), 0)),
        out_shape=jax.ShapeDtypeStruct((n, d), F32),
        scratch_shapes=[pltpu.VMEM((2, TOP_K) + _tile_view_shape(t), F32), pltpu.VMEM((t, 2 * TOP_K), F32),
                        pltpu.SemaphoreType.DMA((2,))],
        compiler_params=_cparams(("arbitrary",)),
        name="combine",
    )(dest, ys, gates, h)


def _rot_half(z):
    half = QK_ROPE_DIM // 2
    return jnp.concatenate([-z[:, half:], z[:, :half]], axis=1)


def _pe_slot(z):
    rows = z.shape[0]
    return jnp.concatenate([jnp.zeros((rows, QK_NOPE_DIM), F32), z,
                            jnp.zeros((rows, LANES - QK_HEAD_DIM), F32)], axis=1)


def _prep_proj_weights(attn_norm_w, w_in, q_a_norm_w, w_q_b, kv_a_norm_w, w_kv_b, q_norm_w, k_norm_w, gate_b):
    c0 = 2 * CONV_DIM
    c1 = c0 + Q_LORA_RANK
    c2 = c1 + KV_LORA_RANK
    c3 = c2 + QK_ROPE_DIM
    kpe = w_in[:, c2:c3]
    win = jnp.concatenate([w_in[:, :c2], _pe_slot(kpe), _pe_slot(_rot_half(kpe * k_norm_w[None, QK_NOPE_DIM:])),
                           w_in[:, c3:]], axis=1).astype(BF16)

    pad = jnp.zeros((Q_LORA_RANK, LANES - QK_HEAD_DIM), F32)
    q1, q2 = [], []
    for h in range(N_HEADS):
        cols = w_q_b[:, h * QK_HEAD_DIM:(h + 1) * QK_HEAD_DIM]
        q1.append(jnp.concatenate([cols, pad], axis=1))
        q2.append(_pe_slot(_rot_half(cols[:, QK_NOPE_DIM:] * q_norm_w[None, QK_NOPE_DIM:])))
    wq = jnp.concatenate(q1 + q2, axis=1).astype(BF16)

    kpad = jnp.zeros((KV_LORA_RANK, LANES - QK_NOPE_DIM), F32)
    ks, vs = [], []
    per_head = QK_NOPE_DIM + V_HEAD_DIM
    for h in range(N_HEADS):
        cols = w_kv_b[:, h * per_head:(h + 1) * per_head]
        ks.append(jnp.concatenate([cols[:, :QK_NOPE_DIM], kpad], axis=1))
        vs.append(cols[:, QK_NOPE_DIM:])
    wk = jnp.concatenate(ks, axis=1).astype(BF16)
    wvt = jnp.concatenate(vs, axis=1).T.astype(BF16)

    lane_pad = jnp.zeros((LANES - QK_HEAD_DIM,), F32)
    qlane = jnp.concatenate([q_norm_w, lane_pad])[None, :]
    klane = jnp.concatenate([k_norm_w, lane_pad])[None, :]
    return (attn_norm_w[None, :], win, q_a_norm_w[None, :], wq, kv_a_norm_w[None, :], wk, wvt, qlane, klane,
            gate_b[None, :])


def _rope_tables(length):
    half = QK_ROPE_DIM // 2
    inv_freq = ROPE_THETA ** (-jnp.arange(half, dtype=F32) / half)
    ang = jnp.arange(length, dtype=F32)[:, None] * inv_freq[None, :]
    cos, sin = jnp.cos(ang), jnp.sin(ang)
    ones = jnp.ones((length, QK_NOPE_DIM), F32)
    tail = LANES - QK_HEAD_DIM
    cos_t = jnp.concatenate([ones, cos, cos, jnp.ones((length, tail), F32)], axis=1)
    sin_t = jnp.concatenate([0.0 * ones, sin, sin, jnp.zeros((length, tail), F32)], axis=1)
    return cos_t, sin_t


def _tile(n, pref):
    t = pref
    while n % t:
        t //= 2
    return t


def kernel(x, meta_tokens, attn_norm_w, w_in, conv_dw_w, conv_dw_b, conv_ln_w, conv_ln_b, conv_pw2_w, q_a_norm_w, w_q_b, kv_a_norm_w, w_kv_b, q_norm_w, k_norm_w, w_o_mla, gate_b, w_out, ffn_norm_w, router_w, router_b, w_gate_up, b_gate_up, w_down, b_down):
    assert attn_norm_w.shape[0] == 1, "one layer: rows of meta tokens never feed a later layer"
    b, s, d = x.shape
    n = b * s
    x2 = x.reshape(n, d)

    tm = _tile(s, 256)
    tq = _tile(s, 512)
    cos_t, sin_t = _rope_tables(N_META + s)
    pw = _prep_proj_weights(attn_norm_w[0], w_in[0], q_a_norm_w[0], w_q_b[0], kv_a_norm_w[0], w_kv_b[0],
                            q_norm_w[0], k_norm_w[0], gate_b[0])

    tp = _tile(s, 512)
    glu, q, k, vt, g = _proj_call(x2, (cos_t[N_META:], sin_t[N_META:]), pw, tm=tp, tiles_per_seq=s // tp)
    glum, _, km, vmt, _ = _proj_call(meta_tokens.astype(F32), (cos_t[:N_META], sin_t[:N_META]), pw, tm=N_META,
                                     tiles_per_seq=1)

    o = _attn_call(q, k, vt, km, vmt, tq=tq, batch=b)

    rw = router_w[0].T
    rw_hi = rw.astype(BF16)
    rw_lo = (rw - rw_hi.astype(F32)).astype(BF16)
    mix_w = (conv_dw_w[0], conv_dw_b[0][None, :], conv_ln_w[0][None, :], conv_ln_b[0][None, :],
             conv_pw2_w[0].astype(BF16), w_o_mla[0].astype(BF16), w_out[0].astype(BF16), ffn_norm_w[0][None, :],
             rw_hi, rw_lo, router_b[0][:, None])
    h, hn, idx, gates = _mix_call(glu, glum, o, g, x2, mix_w, tm=tm, tiles_per_seq=s // tm)

    bm = 512
    n_blocks = (n * TOP_K) // bm + N_EXPERTS
    t_dma = max(min(256, n // DMA_STEPS), SUBLANES)
    rank, cnt = _rank_call(idx, t=_tile(n, 512))
    start, bexp, nact = _plan_call(cnt[:, 0], bm=bm, n_blocks=n_blocks)
    dest = _dest_call(start, idx, rank, t=t_dma, tb=_tile(n, 2048))
    xs, wgu_b, wd_b = _scatter_call(dest, hn, w_gate_up[0], w_down[0], t=t_dma, p_rows=n_blocks * bm)
    ys = _expert_call(bexp, nact, xs, wgu_b, b_gate_up[0][:, None, :], wd_b, b_down[0][:, None, :], bm=bm)
    out = _combine_call(dest, ys, gates, h, t=t_dma)
    return out.reshape(b, s, d)
```

```python
import functools
import math

import jax
import jax.numpy as jnp
from jax import lax
from jax.experimental import pallas as pl
from jax.experimental.pallas import tpu as pltpu

N_META = 16
CONV_DIM = 512
CONV_WIDTH = 31
N_HEADS = 8
QK_NOPE_DIM = 64
QK_ROPE_DIM = 32
QK_HEAD_DIM = QK_NOPE_DIM + QK_ROPE_DIM
V_HEAD_DIM = 64
Q_LORA_RANK = 256
KV_LORA_RANK = 128
ROPE_THETA = 10000.0
N_EXPERTS = 32
TOP_K = 4
D_FF = 1024
SWIGLU_LIMIT = 7.0
SWIGLU_ALPHA = 1.702
NORM_EPS = 1e-6
NEG_INF = -1e30

LANES = 128
SUBLANES = 8
BF16_ROWS = 16
ROW_CHUNKS = 8
SCORE_LOOKAHEAD = 2
MIX_TILE = 512
HALO = 32
VMEM_LIMIT = 56 * 1024 * 1024

F32 = jnp.float32
BF16 = jnp.bfloat16


def _cparams(sem):
    return pltpu.CompilerParams(dimension_semantics=sem, vmem_limit_bytes=VMEM_LIMIT)


def _dot(a, b):
    return jnp.dot(a, b, preferred_element_type=F32)


def _dot_t(a, b):
    return lax.dot_general(a, b, (((1,), (1,)), ((), ())), preferred_element_type=F32)


def _sigmoid(x):
    return 1.0 / (1.0 + jnp.exp(-x))


def _rms(x, w):
    return x * lax.rsqrt(jnp.mean(x * x, axis=-1, keepdims=True) + NORM_EPS) * w


C_CONV = 2 * CONV_DIM
C_LAT = Q_LORA_RANK + KV_LORA_RANK + 2 * LANES
C_GATE = 2 * 1024


def _proj_body(x_ref, anw_ref, win_ref, qaw_ref, wq_ref, kvaw_ref, wk_ref, wvt_ref, qlane_ref, klane_ref,
               cos_ref, sin_ref, gb_ref, glu_ref, q_ref, k_ref, vt_ref, g_ref, *, q_scale):
    x = x_ref[...]
    xn = _rms(x, anw_ref[...]).astype(BF16)

    lat = _dot(xn, win_ref[:, C_CONV:C_CONV + C_LAT])
    q_lat = lat[:, :Q_LORA_RANK]
    c_kv = lat[:, Q_LORA_RANK:Q_LORA_RANK + KV_LORA_RANK]
    kpe = lat[:, Q_LORA_RANK + KV_LORA_RANK:Q_LORA_RANK + KV_LORA_RANK + LANES]
    kpe_rot = lat[:, Q_LORA_RANK + KV_LORA_RANK + LANES:]

    qn = _rms(q_lat, qaw_ref[...]).astype(BF16)
    cn = _rms(c_kv, kvaw_ref[...]).astype(BF16)
    qq = _dot(qn, wq_ref[...])
    kv = _dot(cn, wk_ref[...])
    vt_ref[...] = _dot_t(wvt_ref[...], cn).astype(vt_ref.dtype)
    u = _dot(xn, win_ref[:, 0:C_CONV])

    cos = cos_ref[...]
    sin = sin_ref[...]
    q_cos = qlane_ref[...] * cos
    for h in range(N_HEADS):
        q1 = qq[:, h * LANES:(h + 1) * LANES]
        q2 = qq[:, (N_HEADS + h) * LANES:(N_HEADS + h + 1) * LANES]
        ss = jnp.sum(q1 * q1, axis=-1, keepdims=True)
        s = lax.rsqrt(ss * (1.0 / QK_HEAD_DIM) + NORM_EPS) * q_scale
        q_ref[:, h * LANES:(h + 1) * LANES] = (s * (q1 * q_cos + q2 * sin)).astype(q_ref.dtype)

    gl = _dot(xn, win_ref[:, C_CONV + C_LAT:]) + gb_ref[...]

    k_cos = klane_ref[...] * cos
    ss_pe = jnp.sum(kpe * kpe, axis=-1, keepdims=True)
    k_pe_roped = kpe * k_cos + kpe_rot * sin
    for h in range(N_HEADS):
        kn = kv[:, h * LANES:(h + 1) * LANES]
        ss = jnp.sum(kn * kn, axis=-1, keepdims=True) + ss_pe
        s = lax.rsqrt(ss * (1.0 / QK_HEAD_DIM) + NORM_EPS)
        k_ref[:, h * LANES:(h + 1) * LANES] = (s * (kn * k_cos + k_pe_roped)).astype(k_ref.dtype)

    glu_ref[...] = u[:, :CONV_DIM] * _sigmoid(u[:, CONV_DIM:])
    g_ref[...] = _sigmoid(gl).astype(g_ref.dtype)


def _proj_call(x2, tables, weights, *, tm, tiles_per_seq):
    n = x2.shape[0]
    d = x2.shape[1]
    cos_t, sin_t = tables
    (anw, win, qaw, wq, kvaw, wk, wvt, qlane, klane, gb) = weights
    full = lambda a: pl.BlockSpec(a.shape, lambda i: (0,) * a.ndim)
    row = lambda c: pl.BlockSpec((tm, c), lambda i: (i, 0))
    pos = pl.BlockSpec((tm, LANES), lambda i: (i % tiles_per_seq, 0))
    q_scale = (QK_HEAD_DIM ** -0.5) * math.log2(math.e)
    return pl.pallas_call(
        functools.partial(_proj_body, q_scale=q_scale),
        grid=(n // tm,),
        in_specs=[row(d), full(anw), full(win), full(qaw), full(wq), full(kvaw), full(wk), full(wvt),
                  full(qlane), full(klane), pos, pos, full(gb)],
        out_specs=[row(CONV_DIM), row(N_HEADS * LANES), row(N_HEADS * LANES),
                   pl.BlockSpec((N_HEADS * V_HEAD_DIM, tm), lambda i: (0, i)), row(C_GATE)],
        out_shape=[jax.ShapeDtypeStruct((n, CONV_DIM), F32),
                   jax.ShapeDtypeStruct((n, N_HEADS * LANES), BF16),
                   jax.ShapeDtypeStruct((n, N_HEADS * LANES), BF16),
                   jax.ShapeDtypeStruct((N_HEADS * V_HEAD_DIM, n), BF16),
                   jax.ShapeDtypeStruct((n, C_GATE), BF16)],
        compiler_params=_cparams(("parallel",)),
        name="proj",
    )(x2, anw, win, qaw, wq, kvaw, wk, wvt, qlane, klane, cos_t, sin_t, gb)


def _attn_body(q_ref, k_ref, vt_ref, km_ref, vmt_ref, o_ref, m_ref, l_ref, acc_ref):
    i = pl.program_id(1)
    tq = q_ref.shape[0]

    def head_slices(h):
        return slice(h * LANES, (h + 1) * LANES), slice(h * V_HEAD_DIM, (h + 1) * V_HEAD_DIM)

    def _meta():
        ms, ls = [], []
        sts = [_dot_t(km_ref[:, head_slices(h)[0]], q_ref[:, head_slices(h)[0]]) for h in range(N_HEADS)]
        for h in range(N_HEADS):
            ks, vs = head_slices(h)
            st = sts[h]
            m = jnp.max(st, axis=0, keepdims=True)
            p = jnp.exp2(st - m)
            ms.append(m)
            ls.append(jnp.sum(p, axis=0, keepdims=True))
            acc_ref[h] = _dot(vmt_ref[vs, :], p.astype(BF16))
        m_ref[...] = jnp.concatenate(ms, axis=0)
        l_ref[...] = jnp.concatenate(ls, axis=0)

    def step(j, diagonal):
        keys = pl.ds(pl.multiple_of(j * tq, tq), tq)

        def scores(h):
            ks, _ = head_slices(h)
            return _dot_t(k_ref[keys, ks], q_ref[:, ks])

        m_all = m_ref[...]
        l_all = l_ref[...]
        ms, ls = [], []
        pending = [scores(h) for h in range(SCORE_LOOKAHEAD)]
        for h in range(N_HEADS):
            _, vs = head_slices(h)
            st = pending.pop(0)
            if h + SCORE_LOOKAHEAD < N_HEADS:
                pending.append(scores(h + SCORE_LOOKAHEAD))
            if diagonal:
                key = lax.broadcasted_iota(jnp.int32, st.shape, 0)
                qry = lax.broadcasted_iota(jnp.int32, st.shape, 1)
                st = jnp.where(key <= qry, st, NEG_INF)
            m_prev = m_all[h:h + 1, :]
            m_new = jnp.maximum(m_prev, jnp.max(st, axis=0, keepdims=True))
            alpha = jnp.exp2(m_prev - m_new)
            p = jnp.exp2(st - m_new)
            l_new = alpha * l_all[h:h + 1, :] + jnp.sum(p, axis=0, keepdims=True)
            acc = alpha * acc_ref[h] + _dot(vt_ref[vs, keys], p.astype(BF16))
            if diagonal:
                acc_ref[h] = acc / l_new
            else:
                acc_ref[h] = acc
                ms.append(m_new)
                ls.append(l_new)
        if not diagonal:
            m_ref[...] = jnp.concatenate(ms, axis=0)
            l_ref[...] = jnp.concatenate(ls, axis=0)

    def full_step(j, carry):
        step(j, False)
        return carry

    _meta()
    lax.fori_loop(0, i, full_step, 0)
    step(i, True)
    ot = acc_ref[...].reshape(N_HEADS * V_HEAD_DIM, tq)
    o_ref[...] = jnp.transpose(ot).astype(o_ref.dtype)


def _attn_call(q, k, vt, km, vmt, *, tq, batch):
    n = q.shape[0]
    s = n // batch
    nq = s // tq
    const = lambda bi, i: (0, 0)
    return pl.pallas_call(
        _attn_body,
        grid=(batch, nq),
        in_specs=[pl.BlockSpec((tq, N_HEADS * LANES), lambda bi, i: (bi * nq + i, 0)),
                  pl.BlockSpec((s, N_HEADS * LANES), lambda bi, i: (bi, 0)),
                  pl.BlockSpec((N_HEADS * V_HEAD_DIM, s), lambda bi, i: (0, bi)),
                  pl.BlockSpec(km.shape, const), pl.BlockSpec(vmt.shape, const)],
        out_specs=pl.BlockSpec((tq, N_HEADS * V_HEAD_DIM), lambda bi, i: (bi * nq + i, 0)),
        out_shape=jax.ShapeDtypeStruct((n, N_HEADS * V_HEAD_DIM), BF16),
        scratch_shapes=[pltpu.VMEM((N_HEADS, tq), F32), pltpu.VMEM((N_HEADS, tq), F32),
                        pltpu.VMEM((N_HEADS, V_HEAD_DIM, tq), F32)],
        compiler_params=_cparams(("parallel", "arbitrary")),
        name="attn",
    )(q, k, vt, km, vmt)


def _tile_view_shape(rows):
    return (rows // SUBLANES, ROW_CHUNKS, SUBLANES, LANES)


def _to_row_tiles(ref, val):
    groups = val.shape[0] // SUBLANES
    for c in range(ROW_CHUNKS):
        ref[:, c] = val[:, c * LANES:(c + 1) * LANES].reshape(groups, SUBLANES, LANES)


def _from_row_tiles(ref):
    rows = ref.shape[0] * SUBLANES
    return jnp.concatenate([ref[:, c].reshape(rows, LANES) for c in range(ROW_CHUNKS)], axis=1)


def _row_of(ref, row_group, sublane):
    return ref.at[row_group, :, sublane, :]


def _route(hn, whi_ref, wlo_ref, rb_ref, idx_ref, gate_ref):
    x_hi = hn.astype(BF16)
    x_lo = (hn - x_hi.astype(F32)).astype(BF16)
    w_hi = whi_ref[...]
    logits = _dot_t(w_hi, x_hi) + _dot_t(w_hi, x_lo) + _dot_t(wlo_ref[...], x_hi) + rb_ref[...]

    e_iota = lax.broadcasted_iota(jnp.int32, logits.shape, 0).astype(F32)
    vals, idxs = [], []
    cur = logits
    for _ in range(TOP_K):
        m = jnp.max(cur, axis=0, keepdims=True)
        idx = jnp.min(jnp.where(cur == m, e_iota, float(N_EXPERTS)), axis=0, keepdims=True)
        vals.append(m)
        idxs.append(idx)
        cur = jnp.where(e_iota == idx, -jnp.inf, cur)
    exps = [jnp.exp(v - vals[0]) for v in vals]
    denom = exps[0] + exps[1] + exps[2] + exps[3]
    idx_ref[...] = jnp.concatenate(idxs, axis=0).astype(jnp.int32)
    gate_ref[...] = jnp.concatenate([e / denom for e in exps] + [jnp.zeros_like(denom)] * 4, axis=0)


def _mix_body(glu_ref, halo_ref, glum_ref, o_ref, g_ref, x_ref, dww_ref, dwb_ref, lnw_ref, lnb_ref,
              pw2_ref, wo_ref, wout_ref, fnw_ref, whi_ref, wlo_ref, rb_ref,
              h_ref, hn_ref, idx_ref, gate_ref, xpad_ref, shift_ref, *, tm, tiles_per_seq):
    i = pl.program_id(0)
    y_mla = _dot(o_ref[...], wo_ref[...])
    first = (i % tiles_per_seq) == 0
    meta_ctx = jnp.concatenate([jnp.zeros((HALO - N_META, CONV_DIM), F32), glum_ref[...]], axis=0)
    xpad_ref[0:HALO, :] = jnp.where(first, meta_ctx, halo_ref[...])
    xpad_ref[HALO:HALO + tm, :] = glu_ref[...]

    span = tm + HALO - SUBLANES
    for rho in range(1, SUBLANES):
        shift_ref[rho, 0:span, :] = xpad_ref[rho:rho + span, :]
    off = HALO - (CONV_WIDTH - 1)
    acc = jnp.zeros((tm, CONV_DIM), F32) + dwb_ref[...]
    for t in range(CONV_WIDTH):
        rho, base = (off + t) % SUBLANES, (off + t) // SUBLANES * SUBLANES
        if rho == 0:
            win = xpad_ref[base:base + tm, :]
        else:
            win = shift_ref[rho, base:base + tm, :]
        acc = acc + dww_ref[t:t + 1, :] * win

    mu = jnp.mean(acc, axis=-1, keepdims=True)
    xc = acc - mu
    y = xc * lax.rsqrt(jnp.mean(xc * xc, axis=-1, keepdims=True) + NORM_EPS) * lnw_ref[...] + lnb_ref[...]
    y = y * _sigmoid(y)
    y_conv = _dot(y.astype(BF16), pw2_ref[...])
    g = g_ref[...].astype(F32)
    mixed = g[:, :1024] * y_conv + g[:, 1024:] * y_mla
    h = x_ref[...] + _dot(mixed.astype(BF16), wout_ref[...])
    h_ref[...] = h
    hn = _rms(h, fnw_ref[...])
    _to_row_tiles(hn_ref, hn)
    _route(hn, whi_ref, wlo_ref, rb_ref, idx_ref, gate_ref)


def _mix_call(glu, glum, o, g, x2, weights, *, tm, tiles_per_seq):
    n, d = x2.shape
    full = lambda a: pl.BlockSpec(a.shape, lambda i: (0,) * a.ndim)
    row = lambda c: pl.BlockSpec((tm, c), lambda i: (i, 0))
    halo = pl.BlockSpec((HALO, CONV_DIM), lambda i: (jnp.maximum(i * (tm // HALO) - 1, 0), 0))
    return pl.pallas_call(
        functools.partial(_mix_body, tm=tm, tiles_per_seq=tiles_per_seq),
        grid=(n // tm,),
        in_specs=[row(CONV_DIM), halo, full(glum), row(N_HEADS * V_HEAD_DIM), row(C_GATE), row(d)]
                 + [full(w) for w in weights],
        out_specs=[row(d), pl.BlockSpec(_tile_view_shape(tm), lambda i: (i, 0, 0, 0)),
                   pl.BlockSpec((TOP_K, tm), lambda i: (0, i)), pl.BlockSpec((2 * TOP_K, tm), lambda i: (0, i))],
        out_shape=[jax.ShapeDtypeStruct((n, d), F32), jax.ShapeDtypeStruct(_tile_view_shape(n), F32),
                   jax.ShapeDtypeStruct((TOP_K, n), jnp.int32), jax.ShapeDtypeStruct((2 * TOP_K, n), F32)],
        scratch_shapes=[pltpu.VMEM((HALO + tm, CONV_DIM), F32),
                        pltpu.VMEM((SUBLANES, HALO + tm, CONV_DIM), F32)],
        compiler_params=_cparams(("parallel",)),
        name="mix",
    )(glu, glu, glum, o, g, x2, *weights)


def _rank_body(idx_ref, rank_ref, cnt_ref, carry_ref, *, t):
    i = pl.program_id(0)

    @pl.when(i == 0)
    def _init():
        carry_ref[...] = jnp.zeros_like(carry_ref)

    idx = idx_ref[...]
    e_iota = lax.broadcasted_iota(jnp.int32, (N_EXPERTS, t), 0)
    hits = [e_iota == idx[k:k + 1, :] for k in range(TOP_K)]
    onehot = jnp.zeros((N_EXPERTS, t), F32)
    for hk in hits:
        onehot = onehot + jnp.where(hk, 1.0, 0.0)
    r = lax.broadcasted_iota(jnp.int32, (t, t), 0)
    c = lax.broadcasted_iota(jnp.int32, (t, t), 1)
    before = jnp.where(r < c, 1.0, 0.0).astype(BF16)
    val = _dot(onehot.astype(BF16), before) + carry_ref[...]
    ranks = [jnp.sum(jnp.where(hk, val, 0.0), axis=0, keepdims=True) for hk in hits]
    rank_ref[...] = jnp.concatenate(ranks, axis=0).astype(jnp.int32)
    total = carry_ref[...] + jnp.sum(onehot, axis=1, keepdims=True)
    carry_ref[...] = total
    cnt_ref[...] = jnp.broadcast_to(total, cnt_ref.shape).astype(jnp.int32)


def _rank_call(idx, *, t):
    n = idx.shape[1]
    return pl.pallas_call(
        functools.partial(_rank_body, t=t),
        grid=(n // t,),
        in_specs=[pl.BlockSpec((TOP_K, t), lambda i: (0, i))],
        out_specs=[pl.BlockSpec((TOP_K, t), lambda i: (0, i)), pl.BlockSpec((N_EXPERTS, LANES), lambda i: (0, 0))],
        out_shape=[jax.ShapeDtypeStruct((TOP_K, n), jnp.int32), jax.ShapeDtypeStruct((N_EXPERTS, LANES), jnp.int32)],
        scratch_shapes=[pltpu.VMEM((N_EXPERTS, 1), F32)],
        compiler_params=_cparams(("arbitrary",)),
        name="rank",
    )(idx)


def _plan_body(cnt_ref, start_ref, bexp_ref, nact_ref, *, bm, n_blocks):
    def per_expert(e, carry):
        blk, last = carry
        c = cnt_ref[e]
        nb = (c + (bm - 1)) // bm
        start_ref[e] = blk * bm

        def fill(b, _):
            bexp_ref[b] = e
            return 0

        lax.fori_loop(blk, blk + nb, fill, 0)
        return blk + nb, jnp.where(nb > 0, e, last)

    n_act, last = lax.fori_loop(0, N_EXPERTS, per_expert, (jnp.int32(0), jnp.int32(0)))
    nact_ref[0] = n_act

    def tail(b, _):
        bexp_ref[b] = last
        return 0

    lax.fori_loop(n_act, n_blocks, tail, 0)


def _plan_call(cnt, *, bm, n_blocks):
    smem = lambda: pl.BlockSpec(memory_space=pltpu.SMEM)
    return pl.pallas_call(
        functools.partial(_plan_body, bm=bm, n_blocks=n_blocks),
        in_specs=[smem()],
        out_specs=[smem(), smem(), smem()],
        out_shape=[jax.ShapeDtypeStruct((N_EXPERTS,), jnp.int32), jax.ShapeDtypeStruct((n_blocks,), jnp.int32),
                   jax.ShapeDtypeStruct((1,), jnp.int32)],
        name="plan",
    )(cnt)


def _dest_body(start_ref, idx_ref, rank_ref, dest_ref, *, t):
    idx = idx_ref[...]
    base = jnp.zeros(idx.shape, jnp.int32)
    for e in range(N_EXPERTS):
        base = jnp.where(idx == e, start_ref[e], base)
    dest = base + rank_ref[...]
    for s in range(dest_ref.shape[0]):
        dest_ref[s] = jnp.concatenate([dest[k:k + 1, s * t:(s + 1) * t] for k in range(TOP_K)], axis=1)


def _dest_call(start, idx, rank, *, t, tb):
    n = idx.shape[1]
    blk = pl.BlockSpec((TOP_K, tb), lambda i, s: (0, i))
    return pl.pallas_call(
        functools.partial(_dest_body, t=t),
        grid_spec=pltpu.PrefetchScalarGridSpec(
            num_scalar_prefetch=1, grid=(n // tb,), in_specs=[blk, blk],
            out_specs=pl.BlockSpec((tb // t, 1, TOP_K * t), lambda i, s: (i, 0, 0))),
        out_shape=jax.ShapeDtypeStruct((n // t, 1, TOP_K * t), jnp.int32),
        compiler_params=_cparams(("parallel",)),
        name="dest",
    )(start, idx, rank)


DMA_STEPS = 128
WAIT_GROUP = 32


def _drain(copy, count):
    def group(_, carry):
        for _ in range(WAIT_GROUP):
            copy.wait()
        return carry

    lax.fori_loop(0, count // WAIT_GROUP, group, 0)


def _scatter_body(dest_ref, x_ref, wgu_ref, wd_ref, xs_ref, wgu_out, wd_out, sem, *, t, iters):
    groups = t // SUBLANES // iters
    cast_rows = wgu_ref.shape[0] // iters

    def issue(it, carry):
        for g in range(groups):
            grp = it * groups + g
            for s in range(SUBLANES):
                for k in range(TOP_K):
                    d = dest_ref[0, 0, k * t + grp * SUBLANES + s]
                    pltpu.make_async_copy(_row_of(x_ref, grp, s), _row_of(xs_ref, d >> 3, d & (SUBLANES - 1)),
                                          sem).start(priority=k % 2)
        rows = pl.ds(pl.multiple_of(it * cast_rows, cast_rows), cast_rows)
        wgu_out[rows, :] = wgu_ref[rows, :].astype(BF16)
        wd_out[rows, :] = wd_ref[rows, :].astype(BF16)
        return carry

    lax.fori_loop(0, iters, issue, 0)
    _drain(pltpu.make_async_copy(_row_of(x_ref, 0, 0), _row_of(xs_ref, 0, 0), sem), TOP_K * t)


def _scatter_call(dest, x, wgu, wd, *, t, p_rows):
    n = x.shape[0] * SUBLANES
    steps = n // t
    wgu2 = wgu.reshape(-1, wgu.shape[-1])
    wd2 = wd.reshape(-1, wd.shape[-1])
    w_rows = wgu2.shape[0] // steps
    assert wgu2.shape[0] == wd2.shape[0] == w_rows * steps
    iters = max(t // SUBLANES // 2, 1)
    assert (w_rows // iters) % BF16_ROWS == 0 and w_rows % iters == 0
    wblk = lambda a: pl.BlockSpec((w_rows, a.shape[1]), lambda i: (i, 0))
    xs, wgu_b, wd_b = pl.pallas_call(
        functools.partial(_scatter_body, t=t, iters=iters),
        grid=(steps,),
        in_specs=[pl.BlockSpec((1, 1, TOP_K * t), lambda i: (i, 0, 0), memory_space=pltpu.SMEM),
                  pl.BlockSpec(_tile_view_shape(t), lambda i: (i, 0, 0, 0)), wblk(wgu2), wblk(wd2)],
        out_specs=[pl.BlockSpec(memory_space=pl.ANY), wblk(wgu2), wblk(wd2)],
        out_shape=[jax.ShapeDtypeStruct(_tile_view_shape(p_rows), F32),
                   jax.ShapeDtypeStruct(wgu2.shape, BF16), jax.ShapeDtypeStruct(wd2.shape, BF16)],
        scratch_shapes=[pltpu.SemaphoreType.DMA],
        compiler_params=_cparams(("arbitrary",)),
        name="scatter",
    )(dest, x, wgu2, wd2)
    return xs, wgu_b.reshape(wgu.shape), wd_b.reshape(wd.shape)


def _expert_body(bexp_ref, nact_ref, xs_ref, wgu_ref, bgu_ref, wd_ref, bd_ref, ys_ref):
    b = pl.program_id(0)

    @pl.when(b < nact_ref[0])
    def _():
        x = _from_row_tiles(xs_ref).astype(BF16)
        gu = _dot(x, wgu_ref[0]) + bgu_ref[0]
        g = jnp.minimum(gu[:, :D_FF], SWIGLU_LIMIT)
        u = jnp.clip(gu[:, D_FF:], -SWIGLU_LIMIT, SWIGLU_LIMIT)
        act = (u + 1.0) * (g * _sigmoid(SWIGLU_ALPHA * g))
        _to_row_tiles(ys_ref, _dot(act.astype(BF16), wd_ref[0]) + bd_ref[0])


def _expert_call(bexp, nact, xs, wgu, bgu, wd, bd, *, bm):
    p_rows = xs.shape[0] * SUBLANES
    d = ROW_CHUNKS * LANES
    n_blocks = p_rows // bm
    rows = lambda b, be, na: (jnp.minimum(b, na[0] - 1), 0, 0, 0)
    wsel = lambda b, be, na: (be[b], 0, 0)
    return pl.pallas_call(
        _expert_body,
        grid_spec=pltpu.PrefetchScalarGridSpec(
            num_scalar_prefetch=2, grid=(n_blocks,),
            in_specs=[pl.BlockSpec(_tile_view_shape(bm), rows),
                      pl.BlockSpec((1, d, 2 * D_FF), wsel), pl.BlockSpec((1, 1, 2 * D_FF), wsel),
                      pl.BlockSpec((1, D_FF, d), wsel), pl.BlockSpec((1, 1, d), wsel)],
            out_specs=pl.BlockSpec(_tile_view_shape(bm), rows)),
        out_shape=jax.ShapeDtypeStruct(_tile_view_shape(p_rows), F32),
        compiler_params=_cparams(("arbitrary",)),
        name="experts",
    )(bexp, nact, xs, wgu, bgu, wd, bd)


def _combine_body(dest_ref, ys_ref, gate_ref, h_ref, out_ref, buf_ref, gt_ref, sems, *, t, steps):
    i = pl.program_id(0)
    slot = i % 2
    other = 1 - slot

    def row_gather(d, k, grp, s, sl):
        return pltpu.make_async_copy(_row_of(ys_ref, d >> 3, d & (SUBLANES - 1)),
                                     _row_of(buf_ref.at[sl, k], grp, s), sems.at[sl])

    @pl.when(i == 0)
    def _first():
        buf_ref[...] = jnp.zeros_like(buf_ref)

    @pl.when(i > 0)
    def _wait_previous_tile():
        _drain(row_gather(0, 0, 0, 0, other), TOP_K * t)

    gt_ref[...] = jnp.transpose(gate_ref[...])

    def body(grp, carry):
        for s in range(SUBLANES):
            for k in range(TOP_K):
                d = dest_ref[0, 0, k * t + grp * SUBLANES + s]
                row_gather(d, k, grp, s, slot).start(priority=k % 2)
        rows = pl.ds(pl.multiple_of(grp * SUBLANES, SUBLANES), SUBLANES)
        acc = h_ref[rows, :]
        g = gt_ref[rows, :]
        for k in range(TOP_K):
            tiles = buf_ref[other, k, grp]
            acc = acc + g[:, k:k + 1] * jnp.concatenate([tiles[c] for c in range(ROW_CHUNKS)], axis=1)
        out_ref[rows, :] = acc
        return carry

    lax.fori_loop(0, t // SUBLANES, body, 0)

    @pl.when(i == steps)
    def _drain_extra_gather():
        _drain(row_gather(0, 0, 0, 0, slot), TOP_K * t)


def _combine_call(dest, ys, gates, h, *, t):
    n, d = h.shape
    steps = n // t
    prev = lambda i: jnp.maximum(i - 1, 0)
    return pl.pallas_call(
        functools.partial(_combine_body, t=t, steps=steps),
        grid=(steps + 1,),
        in_specs=[pl.BlockSpec((1, 1, TOP_K * t), lambda i: (jnp.minimum(i, steps - 1), 0, 0),
                               memory_space=pltpu.SMEM),
                  pl.BlockSpec(memory_space=pl.ANY),
                  pl.BlockSpec((2 * TOP_K, t), lambda i: (0, prev(i))),
                  pl.BlockSpec((t, d), lambda i: (prev(i), 0))],
        out_specs=pl.BlockSpec((t, d), lambda i: (prev(i), 0)),
        out_shape=jax.ShapeDtypeStruct((n, d), F32),
        scratch_shapes=[pltpu.VMEM((2, TOP_K) + _tile_view_shape(t), F32), pltpu.VMEM((t, 2 * TOP_K), F32),
                        pltpu.SemaphoreType.DMA((2,))],
        compiler_params=_cparams(("arbitrary",)),
        name="combine",
    )(dest, ys, gates, h)


def _rot_half(z):
    half = QK_ROPE_DIM // 2
    return jnp.concatenate([-z[:, half:], z[:, :half]], axis=1)


def _pe_slot(z):
    rows = z.shape[0]
    return jnp.concatenate([jnp.zeros((rows, QK_NOPE_DIM), F32), z,
                            jnp.zeros((rows, LANES - QK_HEAD_DIM), F32)], axis=1)


def _prep_proj_weights(attn_norm_w, w_in, q_a_norm_w, w_q_b, kv_a_norm_w, w_kv_b, q_norm_w, k_norm_w, gate_b):
    c0 = 2 * CONV_DIM
    c1 = c0 + Q_LORA_RANK
    c2 = c1 + KV_LORA_RANK
    c3 = c2 + QK_ROPE_DIM
    kpe = w_in[:, c2:c3]
    win = jnp.concatenate([w_in[:, :c2].astype(BF16), _pe_slot(kpe).astype(BF16),
                           _pe_slot(_rot_half(kpe * k_norm_w[None, QK_NOPE_DIM:])).astype(BF16),
                           w_in[:, c3:].astype(BF16)], axis=1)

    pad = jnp.zeros((Q_LORA_RANK, LANES - QK_HEAD_DIM), F32)
    q1, q2 = [], []
    for h in range(N_HEADS):
        cols = w_q_b[:, h * QK_HEAD_DIM:(h + 1) * QK_HEAD_DIM]
        q1.append(jnp.concatenate([cols, pad], axis=1))
        q2.append(_pe_slot(_rot_half(cols[:, QK_NOPE_DIM:] * q_norm_w[None, QK_NOPE_DIM:])))
    wq = jnp.concatenate(q1 + q2, axis=1).astype(BF16)

    kpad = jnp.zeros((KV_LORA_RANK, LANES - QK_NOPE_DIM), F32)
    ks, vs = [], []
    per_head = QK_NOPE_DIM + V_HEAD_DIM
    for h in range(N_HEADS):
        cols = w_kv_b[:, h * per_head:(h + 1) * per_head]
        ks.append(jnp.concatenate([cols[:, :QK_NOPE_DIM], kpad], axis=1))
        vs.append(cols[:, QK_NOPE_DIM:])
    wk = jnp.concatenate(ks, axis=1).astype(BF16)
    wvt = jnp.concatenate(vs, axis=1).T.astype(BF16)

    lane_pad = jnp.zeros((LANES - QK_HEAD_DIM,), F32)
    qlane = jnp.concatenate([q_norm_w, lane_pad])[None, :]
    klane = jnp.concatenate([k_norm_w, lane_pad])[None, :]
    return (attn_norm_w[None, :], win, q_a_norm_w[None, :], wq, kv_a_norm_w[None, :], wk, wvt, qlane, klane,
            gate_b[None, :])


def _rope_tables(length):
    half = QK_ROPE_DIM // 2
    inv_freq = ROPE_THETA ** (-jnp.arange(half, dtype=F32) / half)
    ang = jnp.arange(length, dtype=F32)[:, None] * inv_freq[None, :]
    cos, sin = jnp.cos(ang), jnp.sin(ang)
    ones = jnp.ones((length, QK_NOPE_DIM), F32)
    tail = LANES - QK_HEAD_DIM
    cos_t = jnp.concatenate([ones, cos, cos, jnp.ones((length, tail), F32)], axis=1)
    sin_t = jnp.concatenate([0.0 * ones, sin, sin, jnp.zeros((length, tail), F32)], axis=1)
    return cos_t, sin_t


def _tile(n, pref):
    t = pref
    while n % t:
        t //= 2
    return t


def kernel(x, meta_tokens, attn_norm_w, w_in, conv_dw_w, conv_dw_b, conv_ln_w, conv_ln_b, conv_pw2_w, q_a_norm_w, w_q_b, kv_a_norm_w, w_kv_b, q_norm_w, k_norm_w, w_o_mla, gate_b, w_out, ffn_norm_w, router_w, router_b, w_gate_up, b_gate_up, w_down, b_down):
    assert attn_norm_w.shape[0] == 1, "one layer: rows of meta tokens never feed a later layer"
    b, s, d = x.shape
    n = b * s
    x2 = x.reshape(n, d)

    tm = _tile(s, MIX_TILE)
    tq = _tile(s, 512)
    cos_t, sin_t = _rope_tables(N_META + s)
    pw = _prep_proj_weights(attn_norm_w[0], w_in[0], q_a_norm_w[0], w_q_b[0], kv_a_norm_w[0], w_kv_b[0],
                            q_norm_w[0], k_norm_w[0], gate_b[0])

    tp = _tile(s, 512)
    glu, q, k, vt, g = _proj_call(x2, (cos_t[N_META:], sin_t[N_META:]), pw, tm=tp, tiles_per_seq=s // tp)
    glum, _, km, vmt, _ = _proj_call(meta_tokens.astype(F32), (cos_t[:N_META], sin_t[:N_META]), pw, tm=N_META,
                                     tiles_per_seq=1)

    o = _attn_call(q, k, vt, km, vmt, tq=tq, batch=b)

    rw = router_w[0].T
    rw_hi = rw.astype(BF16)
    rw_lo = (rw - rw_hi.astype(F32)).astype(BF16)
    mix_w = (conv_dw_w[0], conv_dw_b[0][None, :], conv_ln_w[0][None, :], conv_ln_b[0][None, :],
             conv_pw2_w[0].astype(BF16), w_o_mla[0].astype(BF16), w_out[0].astype(BF16), ffn_norm_w[0][None, :],
             rw_hi, rw_lo, router_b[0][:, None])
    h, hn, idx, gates = _mix_call(glu, glum, o, g, x2, mix_w, tm=tm, tiles_per_seq=s // tm)

    bm = 512
    n_blocks = (n * TOP_K) // bm + N_EXPERTS
    t_dma = max(min(256, n // DMA_STEPS), SUBLANES)
    rank, cnt = _rank_call(idx, t=_tile(n, 512))
    start, bexp, nact = _plan_call(cnt[:, 0], bm=bm, n_blocks=n_blocks)
    dest = _dest_call(start, idx, rank, t=t_dma, tb=_tile(n, 2048))
    xs, wgu_b, wd_b = _scatter_call(dest, hn, w_gate_up[0], w_down[0], t=t_dma, p_rows=n_blocks * bm)
    ys = _expert_call(bexp, nact, xs, wgu_b, b_gate_up[0][:, None, :], wd_b, b_down[0][:, None, :], bm=bm)
    out = _combine_call(dest, ys, gates, h, t=t_dma)
    return out.reshape(b, s, d)
```

```python
import functools
import math

import jax
import jax.numpy as jnp
from jax import lax
from jax.experimental import pallas as pl
from jax.experimental.pallas import tpu as pltpu

N_META = 16
CONV_DIM = 512
CONV_WIDTH = 31
N_HEADS = 8
QK_NOPE_DIM = 64
QK_ROPE_DIM = 32
QK_HEAD_DIM = QK_NOPE_DIM + QK_ROPE_DIM
V_HEAD_DIM = 64
Q_LORA_RANK = 256
KV_LORA_RANK = 128
ROPE_THETA = 10000.0
N_EXPERTS = 32
TOP_K = 4
D_FF = 1024
SWIGLU_LIMIT = 7.0
SWIGLU_ALPHA = 1.702
NORM_EPS = 1e-6
NEG_INF = -1e30

LANES = 128
SUBLANES = 8
BF16_ROWS = 16
ROW_CHUNKS = 8
SCORE_LOOKAHEAD = 2
MIX_TILE = 512
HALO = 32
VMEM_LIMIT = 56 * 1024 * 1024

F32 = jnp.float32
BF16 = jnp.bfloat16


def _cparams(sem):
    return pltpu.CompilerParams(dimension_semantics=sem, vmem_limit_bytes=VMEM_LIMIT)


def _dot(a, b):
    return jnp.dot(a, b, preferred_element_type=F32)


def _dot_t(a, b):
    return lax.dot_general(a, b, (((1,), (1,)), ((), ())), preferred_element_type=F32)


def _sigmoid(x):
    return 1.0 / (1.0 + jnp.exp(-x))


def _rms(x, w):
    return x * lax.rsqrt(jnp.mean(x * x, axis=-1, keepdims=True) + NORM_EPS) * w


C_CONV = 2 * CONV_DIM
C_LAT = Q_LORA_RANK + KV_LORA_RANK + 2 * LANES
C_GATE = 2 * 1024


def _proj_body(x_ref, anw_ref, win_ref, qaw_ref, wq_ref, kvaw_ref, wk_ref, wvt_ref, qlane_ref, klane_ref,
               cos_ref, sin_ref, gb_ref, glu_ref, q_ref, k_ref, vt_ref, g_ref, *, q_scale):
    x = x_ref[...]
    xn = _rms(x, anw_ref[...]).astype(BF16)

    lat = _dot(xn, win_ref[:, C_CONV:C_CONV + C_LAT])
    q_lat = lat[:, :Q_LORA_RANK]
    c_kv = lat[:, Q_LORA_RANK:Q_LORA_RANK + KV_LORA_RANK]
    kpe = lat[:, Q_LORA_RANK + KV_LORA_RANK:Q_LORA_RANK + KV_LORA_RANK + LANES]
    kpe_rot = lat[:, Q_LORA_RANK + KV_LORA_RANK + LANES:]

    qn = _rms(q_lat, qaw_ref[...]).astype(BF16)
    cn = _rms(c_kv, kvaw_ref[...]).astype(BF16)
    qq = _dot(qn, wq_ref[...])
    kv = _dot(cn, wk_ref[...])
    vt_ref[...] = _dot_t(wvt_ref[...], cn).astype(vt_ref.dtype)
    u = _dot(xn, win_ref[:, 0:C_CONV])

    cos = cos_ref[...]
    sin = sin_ref[...]
    q_cos = qlane_ref[...] * cos
    for h in range(N_HEADS):
        q1 = qq[:, h * LANES:(h + 1) * LANES]
        q2 = qq[:, (N_HEADS + h) * LANES:(N_HEADS + h + 1) * LANES]
        ss = jnp.sum(q1 * q1, axis=-1, keepdims=True)
        s = lax.rsqrt(ss * (1.0 / QK_HEAD_DIM) + NORM_EPS) * q_scale
        q_ref[:, h * LANES:(h + 1) * LANES] = (s * (q1 * q_cos + q2 * sin)).astype(q_ref.dtype)

    gl = _dot(xn, win_ref[:, C_CONV + C_LAT:]) + gb_ref[...]

    k_cos = klane_ref[...] * cos
    ss_pe = jnp.sum(kpe * kpe, axis=-1, keepdims=True)
    k_pe_roped = kpe * k_cos + kpe_rot * sin
    for h in range(N_HEADS):
        kn = kv[:, h * LANES:(h + 1) * LANES]
        ss = jnp.sum(kn * kn, axis=-1, keepdims=True) + ss_pe
        s = lax.rsqrt(ss * (1.0 / QK_HEAD_DIM) + NORM_EPS)
        k_ref[:, h * LANES:(h + 1) * LANES] = (s * (kn * k_cos + k_pe_roped)).astype(k_ref.dtype)

    glu_ref[...] = u[:, :CONV_DIM] * _sigmoid(u[:, CONV_DIM:])
    g_ref[...] = _sigmoid(gl).astype(g_ref.dtype)


def _proj_call(x2, tables, weights, *, tm, tiles_per_seq):
    n = x2.shape[0]
    d = x2.shape[1]
    cos_t, sin_t = tables
    (anw, win, qaw, wq, kvaw, wk, wvt, qlane, klane, gb) = weights
    full = lambda a: pl.BlockSpec(a.shape, lambda i: (0,) * a.ndim)
    row = lambda c: pl.BlockSpec((tm, c), lambda i: (i, 0))
    pos = pl.BlockSpec((tm, LANES), lambda i: (i % tiles_per_seq, 0))
    q_scale = (QK_HEAD_DIM ** -0.5) * math.log2(math.e)
    return pl.pallas_call(
        functools.partial(_proj_body, q_scale=q_scale),
        grid=(n // tm,),
        in_specs=[row(d), full(anw), full(win), full(qaw), full(wq), full(kvaw), full(wk), full(wvt),
                  full(qlane), full(klane), pos, pos, full(gb)],
        out_specs=[row(CONV_DIM), row(N_HEADS * LANES), row(N_HEADS * LANES),
                   pl.BlockSpec((N_HEADS * V_HEAD_DIM, tm), lambda i: (0, i)), row(C_GATE)],
        out_shape=[jax.ShapeDtypeStruct((n, CONV_DIM), F32),
                   jax.ShapeDtypeStruct((n, N_HEADS * LANES), BF16),
                   jax.ShapeDtypeStruct((n, N_HEADS * LANES), BF16),
                   jax.ShapeDtypeStruct((N_HEADS * V_HEAD_DIM, n), BF16),
                   jax.ShapeDtypeStruct((n, C_GATE), BF16)],
        compiler_params=_cparams(("parallel",)),
        name="proj",
    )(x2, anw, win, qaw, wq, kvaw, wk, wvt, qlane, klane, cos_t, sin_t, gb)


def _attn_body(q_ref, k_ref, vt_ref, km_ref, vmt_ref, o_ref, m_ref, l_ref, acc_ref):
    i = pl.program_id(1)
    tq = q_ref.shape[0]

    def head_slices(h):
        return slice(h * LANES, (h + 1) * LANES), slice(h * V_HEAD_DIM, (h + 1) * V_HEAD_DIM)

    def _meta():
        ms, ls = [], []
        sts = [_dot_t(km_ref[:, head_slices(h)[0]], q_ref[:, head_slices(h)[0]]) for h in range(N_HEADS)]
        for h in range(N_HEADS):
            ks, vs = head_slices(h)
            st = sts[h]
            m = jnp.max(st, axis=0, keepdims=True)
            p = jnp.exp2(st - m)
            ms.append(m)
            ls.append(jnp.sum(p, axis=0, keepdims=True))
            acc_ref[h] = _dot(vmt_ref[vs, :], p.astype(BF16))
        m_ref[...] = jnp.concatenate(ms, axis=0)
        l_ref[...] = jnp.concatenate(ls, axis=0)

    def step(j, diagonal):
        keys = pl.ds(pl.multiple_of(j * tq, tq), tq)

        def scores(h):
            ks, _ = head_slices(h)
            return _dot_t(k_ref[keys, ks], q_ref[:, ks])

        m_all = m_ref[...]
        l_all = l_ref[...]
        ms, ls = [], []
        pending = [scores(h) for h in range(SCORE_LOOKAHEAD)]
        for h in range(N_HEADS):
            _, vs = head_slices(h)
            st = pending.pop(0)
            if h + SCORE_LOOKAHEAD < N_HEADS:
                pending.append(scores(h + SCORE_LOOKAHEAD))
            if diagonal:
                key = lax.broadcasted_iota(jnp.int32, st.shape, 0)
                qry = lax.broadcasted_iota(jnp.int32, st.shape, 1)
                st = jnp.where(key <= qry, st, NEG_INF)
            m_prev = m_all[h:h + 1, :]
            m_new = jnp.maximum(m_prev, jnp.max(st, axis=0, keepdims=True))
            alpha = jnp.exp2(m_prev - m_new)
            p = jnp.exp2(st - m_new)
            l_new = alpha * l_all[h:h + 1, :] + jnp.sum(p, axis=0, keepdims=True)
            acc = alpha * acc_ref[h] + _dot(vt_ref[vs, keys], p.astype(BF16))
            if diagonal:
                acc_ref[h] = acc / l_new
            else:
                acc_ref[h] = acc
                ms.append(m_new)
                ls.append(l_new)
        if not diagonal:
            m_ref[...] = jnp.concatenate(ms, axis=0)
            l_ref[...] = jnp.concatenate(ls, axis=0)

    def full_step(j, carry):
        step(j, False)
        return carry

    _meta()
    lax.fori_loop(0, i, full_step, 0)
    step(i, True)
    ot = acc_ref[...].reshape(N_HEADS * V_HEAD_DIM, tq)
    o_ref[...] = jnp.transpose(ot).astype(o_ref.dtype)


def _attn_call(q, k, vt, km, vmt, *, tq, batch):
    n = q.shape[0]
    s = n // batch
    nq = s // tq
    const = lambda bi, i: (0, 0)
    return pl.pallas_call(
        _attn_body,
        grid=(batch, nq),
        in_specs=[pl.BlockSpec((tq, N_HEADS * LANES), lambda bi, i: (bi * nq + i, 0)),
                  pl.BlockSpec((s, N_HEADS * LANES), lambda bi, i: (bi, 0)),
                  pl.BlockSpec((N_HEADS * V_HEAD_DIM, s), lambda bi, i: (0, bi)),
                  pl.BlockSpec(km.shape, const), pl.BlockSpec(vmt.shape, const)],
        out_specs=pl.BlockSpec((tq, N_HEADS * V_HEAD_DIM), lambda bi, i: (bi * nq + i, 0)),
        out_shape=jax.ShapeDtypeStruct((n, N_HEADS * V_HEAD_DIM), BF16),
        scratch_shapes=[pltpu.VMEM((N_HEADS, tq), F32), pltpu.VMEM((N_HEADS, tq), F32),
                        pltpu.VMEM((N_HEADS, V_HEAD_DIM, tq), F32)],
        compiler_params=_cparams(("parallel", "arbitrary")),
        name="attn",
    )(q, k, vt, km, vmt)


def _tile_view_shape(rows):
    return (rows // SUBLANES, ROW_CHUNKS, SUBLANES, LANES)


def _to_row_tiles(ref, val):
    groups = val.shape[0] // SUBLANES
    for c in range(ROW_CHUNKS):
        ref[:, c] = val[:, c * LANES:(c + 1) * LANES].reshape(groups, SUBLANES, LANES)


def _from_row_tiles(ref):
    rows = ref.shape[0] * SUBLANES
    return jnp.concatenate([ref[:, c].reshape(rows, LANES) for c in range(ROW_CHUNKS)], axis=1)


def _row_of(ref, row_group, sublane):
    return ref.at[row_group, :, sublane, :]


def _route(hn, whi_ref, wlo_ref, rb_ref, idx_ref, gate_ref):
    x_hi = hn.astype(BF16)
    x_lo = (hn - x_hi.astype(F32)).astype(BF16)
    w_hi = whi_ref[...]
    logits = _dot_t(w_hi, x_hi) + _dot_t(w_hi, x_lo) + _dot_t(wlo_ref[...], x_hi) + rb_ref[...]

    e_iota = lax.broadcasted_iota(jnp.int32, logits.shape, 0).astype(F32)
    vals, idxs = [], []
    cur = logits
    for _ in range(TOP_K):
        m = jnp.max(cur, axis=0, keepdims=True)
        idx = jnp.min(jnp.where(cur == m, e_iota, float(N_EXPERTS)), axis=0, keepdims=True)
        vals.append(m)
        idxs.append(idx)
        cur = jnp.where(e_iota == idx, -jnp.inf, cur)
    exps = [jnp.exp(v - vals[0]) for v in vals]
    denom = exps[0] + exps[1] + exps[2] + exps[3]
    idx_ref[...] = jnp.concatenate(idxs, axis=0).astype(jnp.int32)
    gate_ref[...] = jnp.concatenate([e / denom for e in exps] + [jnp.zeros_like(denom)] * 4, axis=0)


def _mix_body(glu_ref, halo_ref, glum_ref, o_ref, g_ref, x_ref, dww_ref, dwb_ref, lnw_ref, lnb_ref,
              pw2_ref, wo_ref, wout_ref, fnw_ref, whi_ref, wlo_ref, rb_ref,
              h_ref, hn_ref, idx_ref, gate_ref, xpad_ref, shift_ref, *, tm, tiles_per_seq):
    i = pl.program_id(0)
    y_mla = _dot(o_ref[...], wo_ref[...])
    first = (i % tiles_per_seq) == 0
    meta_ctx = jnp.concatenate([jnp.zeros((HALO - N_META, CONV_DIM), F32), glum_ref[...]], axis=0)
    xpad_ref[0:HALO, :] = jnp.where(first, meta_ctx, halo_ref[...])
    xpad_ref[HALO:HALO + tm, :] = glu_ref[...]

    span = tm + HALO - SUBLANES
    for rho in range(1, SUBLANES):
        shift_ref[rho, 0:span, :] = xpad_ref[rho:rho + span, :]
    off = HALO - (CONV_WIDTH - 1)
    acc = jnp.zeros((tm, CONV_DIM), F32) + dwb_ref[...]
    for t in range(CONV_WIDTH):
        rho, base = (off + t) % SUBLANES, (off + t) // SUBLANES * SUBLANES
        if rho == 0:
            win = xpad_ref[base:base + tm, :]
        else:
            win = shift_ref[rho, base:base + tm, :]
        acc = acc + dww_ref[t:t + 1, :] * win

    mu = jnp.mean(acc, axis=-1, keepdims=True)
    xc = acc - mu
    y = xc * lax.rsqrt(jnp.mean(xc * xc, axis=-1, keepdims=True) + NORM_EPS) * lnw_ref[...] + lnb_ref[...]
    y = y * _sigmoid(y)
    y_conv = _dot(y.astype(BF16), pw2_ref[...])
    g = g_ref[...].astype(F32)
    mixed = g[:, :1024] * y_conv + g[:, 1024:] * y_mla
    h = x_ref[...] + _dot(mixed.astype(BF16), wout_ref[...])
    h_ref[...] = h
    hn = _rms(h, fnw_ref[...])
    _to_row_tiles(hn_ref, hn)
    _route(hn, whi_ref, wlo_ref, rb_ref, idx_ref, gate_ref)


def _mix_call(glu, glum, o, g, x2, weights, *, tm, tiles_per_seq):
    n, d = x2.shape
    full = lambda a: pl.BlockSpec(a.shape, lambda i: (0,) * a.ndim)
    row = lambda c: pl.BlockSpec((tm, c), lambda i: (i, 0))
    halo = pl.BlockSpec((HALO, CONV_DIM), lambda i: (jnp.maximum(i * (tm // HALO) - 1, 0), 0))
    return pl.pallas_call(
        functools.partial(_mix_body, tm=tm, tiles_per_seq=tiles_per_seq),
        grid=(n // tm,),
        in_specs=[row(CONV_DIM), halo, full(glum), row(N_HEADS * V_HEAD_DIM), row(C_GATE), row(d)]
                 + [full(w) for w in weights],
        out_specs=[row(d), pl.BlockSpec(_tile_view_shape(tm), lambda i: (i, 0, 0, 0)),
                   pl.BlockSpec((TOP_K, tm), lambda i: (0, i)), pl.BlockSpec((2 * TOP_K, tm), lambda i: (0, i))],
        out_shape=[jax.ShapeDtypeStruct((n, d), F32), jax.ShapeDtypeStruct(_tile_view_shape(n), F32),
                   jax.ShapeDtypeStruct((TOP_K, n), jnp.int32), jax.ShapeDtypeStruct((2 * TOP_K, n), F32)],
        scratch_shapes=[pltpu.VMEM((HALO + tm, CONV_DIM), F32),
                        pltpu.VMEM((SUBLANES, HALO + tm, CONV_DIM), F32)],
        compiler_params=_cparams(("parallel",)),
        name="mix",
    )(glu, glu, glum, o, g, x2, *weights)


def _rank_body(idx_ref, rank_ref, cnt_ref, carry_ref, *, t):
    i = pl.program_id(0)

    @pl.when(i == 0)
    def _init():
        carry_ref[...] = jnp.zeros_like(carry_ref)

    idx = idx_ref[...]
    e_iota = lax.broadcasted_iota(jnp.int32, (N_EXPERTS, t), 0)
    hits = [e_iota == idx[k:k + 1, :] for k in range(TOP_K)]
    onehot = jnp.zeros((N_EXPERTS, t), F32)
    for hk in hits:
        onehot = onehot + jnp.where(hk, 1.0, 0.0)
    r = lax.broadcasted_iota(jnp.int32, (t, t), 0)
    c = lax.broadcasted_iota(jnp.int32, (t, t), 1)
    before = jnp.where(r < c, 1.0, 0.0).astype(BF16)
    val = _dot(onehot.astype(BF16), before) + carry_ref[...]
    ranks = [jnp.sum(jnp.where(hk, val, 0.0), axis=0, keepdims=True) for hk in hits]
    rank_ref[...] = jnp.concatenate(ranks, axis=0).astype(jnp.int32)
    total = carry_ref[...] + jnp.sum(onehot, axis=1, keepdims=True)
    carry_ref[...] = total
    cnt_ref[...] = jnp.broadcast_to(total, cnt_ref.shape).astype(jnp.int32)


def _rank_call(idx, *, t):
    n = idx.shape[1]
    return pl.pallas_call(
        functools.partial(_rank_body, t=t),
        grid=(n // t,),
        in_specs=[pl.BlockSpec((TOP_K, t), lambda i: (0, i))],
        out_specs=[pl.BlockSpec((TOP_K, t), lambda i: (0, i)), pl.BlockSpec((N_EXPERTS, LANES), lambda i: (0, 0))],
        out_shape=[jax.ShapeDtypeStruct((TOP_K, n), jnp.int32), jax.ShapeDtypeStruct((N_EXPERTS, LANES), jnp.int32)],
        scratch_shapes=[pltpu.VMEM((N_EXPERTS, 1), F32)],
        compiler_params=_cparams(("arbitrary",)),
        name="rank",
    )(idx)


def _plan_body(cnt_ref, start_ref, bexp_ref, nact_ref, *, bm, n_blocks):
    def per_expert(e, carry):
        blk, last = carry
        c = cnt_ref[e]
        nb = (c + (bm - 1)) // bm
        start_ref[e] = blk * bm

        def fill(b, _):
            bexp_ref[b] = e
            return 0

        lax.fori_loop(blk, blk + nb, fill, 0)
        return blk + nb, jnp.where(nb > 0, e, last)

    n_act, last = lax.fori_loop(0, N_EXPERTS, per_expert, (jnp.int32(0), jnp.int32(0)))
    nact_ref[0] = n_act

    def tail(b, _):
        bexp_ref[b] = last
        return 0

    lax.fori_loop(n_act, n_blocks, tail, 0)


def _plan_call(cnt, *, bm, n_blocks):
    smem = lambda: pl.BlockSpec(memory_space=pltpu.SMEM)
    return pl.pallas_call(
        functools.partial(_plan_body, bm=bm, n_blocks=n_blocks),
        in_specs=[smem()],
        out_specs=[smem(), smem(), smem()],
        out_shape=[jax.ShapeDtypeStruct((N_EXPERTS,), jnp.int32), jax.ShapeDtypeStruct((n_blocks,), jnp.int32),
                   jax.ShapeDtypeStruct((1,), jnp.int32)],
        name="plan",
    )(cnt)


def _dest_body(start_ref, idx_ref, rank_ref, dest_ref, *, t):
    idx = idx_ref[...]
    base = jnp.zeros(idx.shape, jnp.int32)
    for e in range(N_EXPERTS):
        base = jnp.where(idx == e, start_ref[e], base)
    dest = base + rank_ref[...]
    for s in range(dest_ref.shape[0]):
        dest_ref[s] = jnp.concatenate([dest[k:k + 1, s * t:(s + 1) * t] for k in range(TOP_K)], axis=1)


def _dest_call(start, idx, rank, *, t, tb):
    n = idx.shape[1]
    blk = pl.BlockSpec((TOP_K, tb), lambda i, s: (0, i))
    return pl.pallas_call(
        functools.partial(_dest_body, t=t),
        grid_spec=pltpu.PrefetchScalarGridSpec(
            num_scalar_prefetch=1, grid=(n // tb,), in_specs=[blk, blk],
            out_specs=pl.BlockSpec((tb // t, 1, TOP_K * t), lambda i, s: (i, 0, 0))),
        out_shape=jax.ShapeDtypeStruct((n // t, 1, TOP_K * t), jnp.int32),
        compiler_params=_cparams(("parallel",)),
        name="dest",
    )(start, idx, rank)


DMA_STEPS = 128
WAIT_GROUP = 32


def _drain(copy, count):
    def group(_, carry):
        for _ in range(WAIT_GROUP):
            copy.wait()
        return carry

    lax.fori_loop(0, count // WAIT_GROUP, group, 0)


def _scatter_body(dest_ref, x_ref, wgu_ref, wd_ref, xs_ref, wgu_out, wd_out, sem, *, t, iters):
    groups = t // SUBLANES // iters
    cast_rows = wgu_ref.shape[0] // iters

    def issue(it, carry):
        for g in range(groups):
            grp = it * groups + g
            for s in range(SUBLANES):
                for k in range(TOP_K):
                    d = dest_ref[0, 0, k * t + grp * SUBLANES + s]
                    pltpu.make_async_copy(_row_of(x_ref, grp, s), xs_ref.at[d], sem).start(priority=k % 2)
        rows = pl.ds(pl.multiple_of(it * cast_rows, cast_rows), cast_rows)
        wgu_out[rows, :] = wgu_ref[rows, :].astype(BF16)
        wd_out[rows, :] = wd_ref[rows, :].astype(BF16)
        return carry

    lax.fori_loop(0, iters, issue, 0)
    _drain(pltpu.make_async_copy(_row_of(x_ref, 0, 0), xs_ref.at[0], sem), TOP_K * t)


def _scatter_call(dest, x, wgu, wd, *, t, p_rows):
    n = x.shape[0] * SUBLANES
    steps = n // t
    wgu2 = wgu.reshape(-1, wgu.shape[-1])
    wd2 = wd.reshape(-1, wd.shape[-1])
    w_rows = wgu2.shape[0] // steps
    assert wgu2.shape[0] == wd2.shape[0] == w_rows * steps
    iters = max(t // SUBLANES // 2, 1)
    assert (w_rows // iters) % BF16_ROWS == 0 and w_rows % iters == 0
    wblk = lambda a: pl.BlockSpec((w_rows, a.shape[1]), lambda i: (i, 0))
    xs, wgu_b, wd_b = pl.pallas_call(
        functools.partial(_scatter_body, t=t, iters=iters),
        grid=(steps,),
        in_specs=[pl.BlockSpec((1, 1, TOP_K * t), lambda i: (i, 0, 0), memory_space=pltpu.SMEM),
                  pl.BlockSpec(_tile_view_shape(t), lambda i: (i, 0, 0, 0)), wblk(wgu2), wblk(wd2)],
        out_specs=[pl.BlockSpec(memory_space=pl.ANY), wblk(wgu2), wblk(wd2)],
        out_shape=[jax.ShapeDtypeStruct((p_rows, ROW_CHUNKS, LANES), F32),
                   jax.ShapeDtypeStruct(wgu2.shape, BF16), jax.ShapeDtypeStruct(wd2.shape, BF16)],
        scratch_shapes=[pltpu.SemaphoreType.DMA],
        compiler_params=_cparams(("arbitrary",)),
        name="scatter",
    )(dest, x, wgu2, wd2)
    return xs, wgu_b.reshape(wgu.shape), wd_b.reshape(wd.shape)


def _expert_body(bexp_ref, nact_ref, xs_ref, wgu_ref, bgu_ref, wd_ref, bd_ref, ys_ref,
                 xin_ref, yout_ref, in_sems, out_sems, *, bm):
    b = pl.program_id(0)
    n_act = nact_ref[0]
    slot = b % 2
    other = 1 - slot

    def chunk(c):
        return slice(c * LANES, (c + 1) * LANES)

    def block_rows(blk):
        return pl.ds(pl.multiple_of(blk * bm, bm), bm)

    def fetch(blk, sl, c):
        return pltpu.make_async_copy(xs_ref.at[block_rows(blk), c], xin_ref.at[sl, :, chunk(c)], in_sems.at[sl])

    def write_back(blk, sl, c):
        return pltpu.make_async_copy(yout_ref.at[sl, :, chunk(c)], ys_ref.at[block_rows(blk), c], out_sems.at[sl])

    @pl.when(b == 0)
    def _prime():
        for c in range(ROW_CHUNKS):
            fetch(0, 0, c).start()

    @pl.when(b < n_act)
    def _block():
        for c in range(ROW_CHUNKS):
            fetch(b, slot, c).wait()

        @pl.when(b + 1 < n_act)
        def _prefetch():
            for c in range(ROW_CHUNKS):
                fetch(b + 1, other, c).start()

        @pl.when(b >= 2)
        def _free_result_slot():
            for c in range(ROW_CHUNKS):
                write_back(b - 2, slot, c).wait()

        x = xin_ref[slot].astype(BF16)
        gu = _dot(x, wgu_ref[0]) + bgu_ref[0]
        g = jnp.minimum(gu[:, :D_FF], SWIGLU_LIMIT)
        u = jnp.clip(gu[:, D_FF:], -SWIGLU_LIMIT, SWIGLU_LIMIT)
        act = (u + 1.0) * (g * _sigmoid(SWIGLU_ALPHA * g))
        yout_ref[slot] = _dot(act.astype(BF16), wd_ref[0]) + bd_ref[0]
        for c in range(ROW_CHUNKS):
            write_back(b, slot, c).start()

    @pl.when(b == n_act)
    def _drain_results():
        for c in range(ROW_CHUNKS):
            write_back(b - 1, other, c).wait()

        @pl.when(b >= 2)
        def _older():
            for c in range(ROW_CHUNKS):
                write_back(b - 2, slot, c).wait()


def _expert_call(bexp, nact, xs, wgu, bgu, wd, bd, *, bm):
    p_rows = xs.shape[0]
    d = ROW_CHUNKS * LANES
    n_blocks = p_rows // bm
    wsel = lambda b, be, na: (be[jnp.minimum(b, n_blocks - 1)], 0, 0)
    hbm = pl.BlockSpec(memory_space=pl.ANY)
    return pl.pallas_call(
        functools.partial(_expert_body, bm=bm),
        grid_spec=pltpu.PrefetchScalarGridSpec(
            num_scalar_prefetch=2, grid=(n_blocks + 1,),
            in_specs=[hbm, pl.BlockSpec((1, d, 2 * D_FF), wsel), pl.BlockSpec((1, 1, 2 * D_FF), wsel),
                      pl.BlockSpec((1, D_FF, d), wsel), pl.BlockSpec((1, 1, d), wsel)],
            out_specs=hbm,
            scratch_shapes=[pltpu.VMEM((2, bm, d), F32), pltpu.VMEM((2, bm, d), F32),
                            pltpu.SemaphoreType.DMA((2,)), pltpu.SemaphoreType.DMA((2,))]),
        out_shape=jax.ShapeDtypeStruct(xs.shape, F32),
        compiler_params=_cparams(("arbitrary",)),
        name="experts",
    )(bexp, nact, xs, wgu, bgu, wd, bd)


def _combine_body(dest_ref, ys_ref, gate_ref, h_ref, out_ref, buf_ref, gt_ref, sems, *, t, steps):
    i = pl.program_id(0)
    slot = i % 2
    other = 1 - slot

    def row_gather(d, k, grp, s, sl):
        return pltpu.make_async_copy(ys_ref.at[d], _row_of(buf_ref.at[sl, k], grp, s), sems.at[sl])

    @pl.when(i == 0)
    def _first():
        buf_ref[...] = jnp.zeros_like(buf_ref)

    @pl.when(i > 0)
    def _wait_previous_tile():
        _drain(row_gather(0, 0, 0, 0, other), TOP_K * t)

    gt_ref[...] = jnp.transpose(gate_ref[...])

    def body(grp, carry):
        for s in range(SUBLANES):
            for k in range(TOP_K):
                d = dest_ref[0, 0, k * t + grp * SUBLANES + s]
                row_gather(d, k, grp, s, slot).start(priority=k % 2)
        rows = pl.ds(pl.multiple_of(grp * SUBLANES, SUBLANES), SUBLANES)
        acc = h_ref[rows, :]
        g = gt_ref[rows, :]
        for k in range(TOP_K):
            tiles = buf_ref[other, k, grp]
            acc = acc + g[:, k:k + 1] * jnp.concatenate([tiles[c] for c in range(ROW_CHUNKS)], axis=1)
        out_ref[rows, :] = acc
        return carry

    lax.fori_loop(0, t // SUBLANES, body, 0)

    @pl.when(i == steps)
    def _drain_extra_gather():
        _drain(row_gather(0, 0, 0, 0, slot), TOP_K * t)


def _combine_call(dest, ys, gates, h, *, t):
    n, d = h.shape
    steps = n // t
    prev = lambda i: jnp.maximum(i - 1, 0)
    return pl.pallas_call(
        functools.partial(_combine_body, t=t, steps=steps),
        grid=(steps + 1,),
        in_specs=[pl.BlockSpec((1, 1, TOP_K * t), lambda i: (jnp.minimum(i, steps - 1), 0, 0),
                               memory_space=pltpu.SMEM),
                  pl.BlockSpec(memory_space=pl.ANY),
                  pl.BlockSpec((2 * TOP_K, t), lambda i: (0, prev(i))),
                  pl.BlockSpec((t, d), lambda i: (prev(i), 0))],
        out_specs=pl.BlockSpec((t, d), lambda i: (prev(i), 0)),
        out_shape=jax.ShapeDtypeStruct((n, d), F32),
        scratch_shapes=[pltpu.VMEM((2, TOP_K) + _tile_view_shape(t), F32), pltpu.VMEM((t, 2 * TOP_K), F32),
                        pltpu.SemaphoreType.DMA((2,))],
        compiler_params=_cparams(("arbitrary",)),
        name="combine",
    )(dest, ys, gates, h)


def _rot_half(z):
    half = QK_ROPE_DIM // 2
    return jnp.concatenate([-z[:, half:], z[:, :half]], axis=1)


def _pe_slot(z):
    rows = z.shape[0]
    return jnp.concatenate([jnp.zeros((rows, QK_NOPE_DIM), F32), z,
                            jnp.zeros((rows, LANES - QK_HEAD_DIM), F32)], axis=1)


def _prep_proj_weights(attn_norm_w, w_in, q_a_norm_w, w_q_b, kv_a_norm_w, w_kv_b, q_norm_w, k_norm_w, gate_b):
    c0 = 2 * CONV_DIM
    c1 = c0 + Q_LORA_RANK
    c2 = c1 + KV_LORA_RANK
    c3 = c2 + QK_ROPE_DIM
    kpe = w_in[:, c2:c3]
    win = jnp.concatenate([w_in[:, :c2].astype(BF16), _pe_slot(kpe).astype(BF16),
                           _pe_slot(_rot_half(kpe * k_norm_w[None, QK_NOPE_DIM:])).astype(BF16),
                           w_in[:, c3:].astype(BF16)], axis=1)

    pad = jnp.zeros((Q_LORA_RANK, LANES - QK_HEAD_DIM), F32)
    q1, q2 = [], []
    for h in range(N_HEADS):
        cols = w_q_b[:, h * QK_HEAD_DIM:(h + 1) * QK_HEAD_DIM]
        q1.append(jnp.concatenate([cols, pad], axis=1))
        q2.append(_pe_slot(_rot_half(cols[:, QK_NOPE_DIM:] * q_norm_w[None, QK_NOPE_DIM:])))
    wq = jnp.concatenate(q1 + q2, axis=1).astype(BF16)

    kpad = jnp.zeros((KV_LORA_RANK, LANES - QK_NOPE_DIM), F32)
    ks, vs = [], []
    per_head = QK_NOPE_DIM + V_HEAD_DIM
    for h in range(N_HEADS):
        cols = w_kv_b[:, h * per_head:(h + 1) * per_head]
        ks.append(jnp.concatenate([cols[:, :QK_NOPE_DIM], kpad], axis=1))
        vs.append(cols[:, QK_NOPE_DIM:])
    wk = jnp.concatenate(ks, axis=1).astype(BF16)
    wvt = jnp.concatenate(vs, axis=1).T.astype(BF16)

    lane_pad = jnp.zeros((LANES - QK_HEAD_DIM,), F32)
    qlane = jnp.concatenate([q_norm_w, lane_pad])[None, :]
    klane = jnp.concatenate([k_norm_w, lane_pad])[None, :]
    return (attn_norm_w[None, :], win, q_a_norm_w[None, :], wq, kv_a_norm_w[None, :], wk, wvt, qlane, klane,
            gate_b[None, :])


def _rope_tables(length):
    half = QK_ROPE_DIM // 2
    inv_freq = ROPE_THETA ** (-jnp.arange(half, dtype=F32) / half)
    ang = jnp.arange(length, dtype=F32)[:, None] * inv_freq[None, :]
    cos, sin = jnp.cos(ang), jnp.sin(ang)
    ones = jnp.ones((length, QK_NOPE_DIM), F32)
    tail = LANES - QK_HEAD_DIM
    cos_t = jnp.concatenate([ones, cos, cos, jnp.ones((length, tail), F32)], axis=1)
    sin_t = jnp.concatenate([0.0 * ones, sin, sin, jnp.zeros((length, tail), F32)], axis=1)
    return cos_t, sin_t


def _tile(n, pref):
    t = pref
    while n % t:
        t //= 2
    return t


def kernel(x, meta_tokens, attn_norm_w, w_in, conv_dw_w, conv_dw_b, conv_ln_w, conv_ln_b, conv_pw2_w, q_a_norm_w, w_q_b, kv_a_norm_w, w_kv_b, q_norm_w, k_norm_w, w_o_mla, gate_b, w_out, ffn_norm_w, router_w, router_b, w_gate_up, b_gate_up, w_down, b_down):
    assert attn_norm_w.shape[0] == 1, "one layer: rows of meta tokens never feed a later layer"
    b, s, d = x.shape
    n = b * s
    x2 = x.reshape(n, d)

    tm = _tile(s, MIX_TILE)
    tq = _tile(s, 512)
    cos_t, sin_t = _rope_tables(N_META + s)
    pw = _prep_proj_weights(attn_norm_w[0], w_in[0], q_a_norm_w[0], w_q_b[0], kv_a_norm_w[0], w_kv_b[0],
                            q_norm_w[0], k_norm_w[0], gate_b[0])

    tp = _tile(s, 512)
    glu, q, k, vt, g = _proj_call(x2, (cos_t[N_META:], sin_t[N_META:]), pw, tm=tp, tiles_per_seq=s // tp)
    glum, _, km, vmt, _ = _proj_call(meta_tokens.astype(F32), (cos_t[:N_META], sin_t[:N_META]), pw, tm=N_META,
                                     tiles_per_seq=1)

    o = _attn_call(q, k, vt, km, vmt, tq=tq, batch=b)

    rw = router_w[0].T
    rw_hi = rw.astype(BF16)
    rw_lo = (rw - rw_hi.astype(F32)).astype(BF16)
    mix_w = (conv_dw_w[0], conv_dw_b[0][None, :], conv_ln_w[0][None, :], conv_ln_b[0][None, :],
             conv_pw2_w[0].astype(BF16), w_o_mla[0].astype(BF16), w_out[0].astype(BF16), ffn_norm_w[0][None, :],
             rw_hi, rw_lo, router_b[0][:, None])
    h, hn, idx, gates = _mix_call(glu, glum, o, g, x2, mix_w, tm=tm, tiles_per_seq=s // tm)

    bm = 512
    n_blocks = (n * TOP_K) // bm + N_EXPERTS
    t_dma = max(min(256, n // DMA_STEPS), SUBLANES)
    rank, cnt = _rank_call(idx, t=_tile(n, 512))
    start, bexp, nact = _plan_call(cnt[:, 0], bm=bm, n_blocks=n_blocks)
    dest = _dest_call(start, idx, rank, t=t_dma, tb=_tile(n, 2048))
    xs, wgu_b, wd_b = _scatter_call(dest, hn, w_gate_up[0], w_down[0], t=t_dma, p_rows=n_blocks * bm)
    ys = _expert_call(bexp, nact, xs, wgu_b, b_gate_up[0][:, None, :], wd_b, b_down[0][:, None, :], bm=bm)
    out = _combine_call(dest, ys, gates, h, t=t_dma)
    return out.reshape(b, s, d)
```

```python
import functools
import math

import jax
import jax.numpy as jnp
from jax import lax
from jax.experimental import pallas as pl
from jax.experimental.pallas import tpu as pltpu

N_META = 16
CONV_DIM = 512
CONV_WIDTH = 31
N_HEADS = 8
QK_NOPE_DIM = 64
QK_ROPE_DIM = 32
QK_HEAD_DIM = QK_NOPE_DIM + QK_ROPE_DIM
V_HEAD_DIM = 64
Q_LORA_RANK = 256
KV_LORA_RANK = 128
ROPE_THETA = 10000.0
N_EXPERTS = 32
TOP_K = 4
D_FF = 1024
SWIGLU_LIMIT = 7.0
SWIGLU_ALPHA = 1.702
NORM_EPS = 1e-6
NEG_INF = -1e30

LANES = 128
SUBLANES = 8
BF16_ROWS = 16
ROW_CHUNKS = 8
SCORE_LOOKAHEAD = 2
MIX_TILE = 512
HALO = 32
VMEM_LIMIT = 56 * 1024 * 1024

F32 = jnp.float32
BF16 = jnp.bfloat16


def _cparams(sem):
    return pltpu.CompilerParams(dimension_semantics=sem, vmem_limit_bytes=VMEM_LIMIT)


def _dot(a, b):
    return jnp.dot(a, b, preferred_element_type=F32)


def _dot_t(a, b):
    return lax.dot_general(a, b, (((1,), (1,)), ((), ())), preferred_element_type=F32)


def _sigmoid(x):
    return 1.0 / (1.0 + jnp.exp(-x))


def _rms(x, w):
    return x * lax.rsqrt(jnp.mean(x * x, axis=-1, keepdims=True) + NORM_EPS) * w


C_CONV = 2 * CONV_DIM
C_LAT = Q_LORA_RANK + KV_LORA_RANK + 2 * LANES
C_GATE = 2 * 1024


def _proj_body(x_ref, anw_ref, win_ref, qaw_ref, wq_ref, kvaw_ref, wk_ref, wvt_ref, qlane_ref, klane_ref,
               cos_ref, sin_ref, gb_ref, glu_ref, q_ref, k_ref, vt_ref, g_ref, *, q_scale):
    x = x_ref[...]
    xn = _rms(x, anw_ref[...]).astype(BF16)

    lat = _dot(xn, win_ref[:, C_CONV:C_CONV + C_LAT])
    q_lat = lat[:, :Q_LORA_RANK]
    c_kv = lat[:, Q_LORA_RANK:Q_LORA_RANK + KV_LORA_RANK]
    kpe = lat[:, Q_LORA_RANK + KV_LORA_RANK:Q_LORA_RANK + KV_LORA_RANK + LANES]
    kpe_rot = lat[:, Q_LORA_RANK + KV_LORA_RANK + LANES:]

    qn = _rms(q_lat, qaw_ref[...]).astype(BF16)
    cn = _rms(c_kv, kvaw_ref[...]).astype(BF16)
    qq = _dot(qn, wq_ref[...])
    kv = _dot(cn, wk_ref[...])
    vt_ref[...] = _dot_t(wvt_ref[...], cn).astype(vt_ref.dtype)
    u = _dot(xn, win_ref[:, 0:C_CONV])

    cos = cos_ref[...]
    sin = sin_ref[...]
    q_cos = qlane_ref[...] * cos
    for h in range(N_HEADS):
        q1 = qq[:, h * LANES:(h + 1) * LANES]
        q2 = qq[:, (N_HEADS + h) * LANES:(N_HEADS + h + 1) * LANES]
        ss = jnp.sum(q1 * q1, axis=-1, keepdims=True)
        s = lax.rsqrt(ss * (1.0 / QK_HEAD_DIM) + NORM_EPS) * q_scale
        q_ref[:, h * LANES:(h + 1) * LANES] = (s * (q1 * q_cos + q2 * sin)).astype(q_ref.dtype)

    gl = _dot(xn, win_ref[:, C_CONV + C_LAT:]) + gb_ref[...]

    k_cos = klane_ref[...] * cos
    ss_pe = jnp.sum(kpe * kpe, axis=-1, keepdims=True)
    k_pe_roped = kpe * k_cos + kpe_rot * sin
    for h in range(N_HEADS):
        kn = kv[:, h * LANES:(h + 1) * LANES]
        ss = jnp.sum(kn * kn, axis=-1, keepdims=True) + ss_pe
        s = lax.rsqrt(ss * (1.0 / QK_HEAD_DIM) + NORM_EPS)
        k_ref[:, h * LANES:(h + 1) * LANES] = (s * (kn * k_cos + k_pe_roped)).astype(k_ref.dtype)

    glu_ref[...] = u[:, :CONV_DIM] * _sigmoid(u[:, CONV_DIM:])
    g_ref[...] = _sigmoid(gl).astype(g_ref.dtype)


def _proj_call(x2, tables, weights, *, tm, tiles_per_seq):
    n = x2.shape[0]
    d = x2.shape[1]
    cos_t, sin_t = tables
    (anw, win, qaw, wq, kvaw, wk, wvt, qlane, klane, gb) = weights
    full = lambda a: pl.BlockSpec(a.shape, lambda i: (0,) * a.ndim)
    row = lambda c: pl.BlockSpec((tm, c), lambda i: (i, 0))
    pos = pl.BlockSpec((tm, LANES), lambda i: (i % tiles_per_seq, 0))
    q_scale = (QK_HEAD_DIM ** -0.5) * math.log2(math.e)
    return pl.pallas_call(
        functools.partial(_proj_body, q_scale=q_scale),
        grid=(n // tm,),
        in_specs=[row(d), full(anw), full(win), full(qaw), full(wq), full(kvaw), full(wk), full(wvt),
                  full(qlane), full(klane), pos, pos, full(gb)],
        out_specs=[row(CONV_DIM), row(N_HEADS * LANES), row(N_HEADS * LANES),
                   pl.BlockSpec((N_HEADS * V_HEAD_DIM, tm), lambda i: (0, i)), row(C_GATE)],
        out_shape=[jax.ShapeDtypeStruct((n, CONV_DIM), F32),
                   jax.ShapeDtypeStruct((n, N_HEADS * LANES), BF16),
                   jax.ShapeDtypeStruct((n, N_HEADS * LANES), BF16),
                   jax.ShapeDtypeStruct((N_HEADS * V_HEAD_DIM, n), BF16),
                   jax.ShapeDtypeStruct((n, C_GATE), BF16)],
        compiler_params=_cparams(("parallel",)),
        name="proj",
    )(x2, anw, win, qaw, wq, kvaw, wk, wvt, qlane, klane, cos_t, sin_t, gb)


def _attn_body(q_ref, k_ref, vt_ref, km_ref, vmt_ref, o_ref, m_ref, l_ref, acc_ref):
    i = pl.program_id(1)
    tq = q_ref.shape[0]

    def head_slices(h):
        return slice(h * LANES, (h + 1) * LANES), slice(h * V_HEAD_DIM, (h + 1) * V_HEAD_DIM)

    def _meta():
        ms, ls = [], []
        sts = [_dot_t(km_ref[:, head_slices(h)[0]], q_ref[:, head_slices(h)[0]]) for h in range(N_HEADS)]
        for h in range(N_HEADS):
            ks, vs = head_slices(h)
            st = sts[h]
            m = jnp.max(st, axis=0, keepdims=True)
            p = jnp.exp2(st - m)
            ms.append(m)
            ls.append(jnp.sum(p, axis=0, keepdims=True))
            acc_ref[h] = _dot(vmt_ref[vs, :], p.astype(BF16))
        m_ref[...] = jnp.concatenate(ms, axis=0)
        l_ref[...] = jnp.concatenate(ls, axis=0)

    def step(j, diagonal):
        keys = pl.ds(pl.multiple_of(j * tq, tq), tq)

        def scores(h):
            ks, _ = head_slices(h)
            return _dot_t(k_ref[keys, ks], q_ref[:, ks])

        m_all = m_ref[...]
        l_all = l_ref[...]
        ms, ls = [], []
        pending = [scores(h) for h in range(SCORE_LOOKAHEAD)]
        for h in range(N_HEADS):
            _, vs = head_slices(h)
            st = pending.pop(0)
            if h + SCORE_LOOKAHEAD < N_HEADS:
                pending.append(scores(h + SCORE_LOOKAHEAD))
            if diagonal:
                key = lax.broadcasted_iota(jnp.int32, st.shape, 0)
                qry = lax.broadcasted_iota(jnp.int32, st.shape, 1)
                st = jnp.where(key <= qry, st, NEG_INF)
            m_prev = m_all[h:h + 1, :]
            m_new = jnp.maximum(m_prev, jnp.max(st, axis=0, keepdims=True))
            alpha = jnp.exp2(m_prev - m_new)
            p = jnp.exp2(st - m_new)
            l_new = alpha * l_all[h:h + 1, :] + jnp.sum(p, axis=0, keepdims=True)
            acc = alpha * acc_ref[h] + _dot(vt_ref[vs, keys], p.astype(BF16))
            if diagonal:
                acc_ref[h] = acc / l_new
            else:
                acc_ref[h] = acc
                ms.append(m_new)
                ls.append(l_new)
        if not diagonal:
            m_ref[...] = jnp.concatenate(ms, axis=0)
            l_ref[...] = jnp.concatenate(ls, axis=0)

    def full_step(j, carry):
        step(j, False)
        return carry

    _meta()
    lax.fori_loop(0, i, full_step, 0)
    step(i, True)
    ot = acc_ref[...].reshape(N_HEADS * V_HEAD_DIM, tq)
    o_ref[...] = jnp.transpose(ot).astype(o_ref.dtype)


def _attn_call(q, k, vt, km, vmt, *, tq, batch):
    n = q.shape[0]
    s = n // batch
    nq = s // tq
    const = lambda bi, i: (0, 0)
    return pl.pallas_call(
        _attn_body,
        grid=(batch, nq),
        in_specs=[pl.BlockSpec((tq, N_HEADS * LANES), lambda bi, i: (bi * nq + i, 0)),
                  pl.BlockSpec((s, N_HEADS * LANES), lambda bi, i: (bi, 0)),
                  pl.BlockSpec((N_HEADS * V_HEAD_DIM, s), lambda bi, i: (0, bi)),
                  pl.BlockSpec(km.shape, const), pl.BlockSpec(vmt.shape, const)],
        out_specs=pl.BlockSpec((tq, N_HEADS * V_HEAD_DIM), lambda bi, i: (bi * nq + i, 0)),
        out_shape=jax.ShapeDtypeStruct((n, N_HEADS * V_HEAD_DIM), BF16),
        scratch_shapes=[pltpu.VMEM((N_HEADS, tq), F32), pltpu.VMEM((N_HEADS, tq), F32),
                        pltpu.VMEM((N_HEADS, V_HEAD_DIM, tq), F32)],
        compiler_params=_cparams(("parallel", "arbitrary")),
        name="attn",
    )(q, k, vt, km, vmt)


def _tile_view_shape(rows):
    return (rows // SUBLANES, ROW_CHUNKS, SUBLANES, LANES)


def _to_row_tiles(ref, val):
    groups = val.shape[0] // SUBLANES
    for c in range(ROW_CHUNKS):
        ref[:, c] = val[:, c * LANES:(c + 1) * LANES].reshape(groups, SUBLANES, LANES)


def _from_row_tiles(ref):
    rows = ref.shape[0] * SUBLANES
    return jnp.concatenate([ref[:, c].reshape(rows, LANES) for c in range(ROW_CHUNKS)], axis=1)


def _row_of(ref, row_group, sublane):
    return ref.at[row_group, :, sublane, :]


def _route(hn, whi_ref, wlo_ref, rb_ref, idx_ref, gate_ref):
    x_hi = hn.astype(BF16)
    x_lo = (hn - x_hi.astype(F32)).astype(BF16)
    w_hi = whi_ref[...]
    logits = _dot_t(w_hi, x_hi) + _dot_t(w_hi, x_lo) + _dot_t(wlo_ref[...], x_hi) + rb_ref[...]

    e_iota = lax.broadcasted_iota(jnp.int32, logits.shape, 0).astype(F32)
    vals, idxs = [], []
    cur = logits
    for _ in range(TOP_K):
        m = jnp.max(cur, axis=0, keepdims=True)
        idx = jnp.min(jnp.where(cur == m, e_iota, float(N_EXPERTS)), axis=0, keepdims=True)
        vals.append(m)
        idxs.append(idx)
        cur = jnp.where(e_iota == idx, -jnp.inf, cur)
    exps = [jnp.exp(v - vals[0]) for v in vals]
    denom = exps[0] + exps[1] + exps[2] + exps[3]
    idx_ref[...] = jnp.concatenate(idxs, axis=0).astype(jnp.int32)
    gate_ref[...] = jnp.concatenate([e / denom for e in exps] + [jnp.zeros_like(denom)] * 4, axis=0)
    return idxs, e_iota


def _mix_body(glu_ref, halo_ref, glum_ref, o_ref, g_ref, x_ref, dww_ref, dwb_ref, lnw_ref, lnb_ref,
              pw2_ref, wo_ref, wout_ref, fnw_ref, whi_ref, wlo_ref, rb_ref,
              h_ref, hn_ref, idx_ref, gate_ref, xpad_ref, shift_ref, *, tm, tiles_per_seq, before):
    i = pl.program_id(0)
    y_mla = _dot(o_ref[...], wo_ref[...])
    first = (i % tiles_per_seq) == 0
    meta_ctx = jnp.concatenate([jnp.zeros((HALO - N_META, CONV_DIM), F32), glum_ref[...]], axis=0)
    xpad_ref[0:HALO, :] = jnp.where(first, meta_ctx, halo_ref[...])
    xpad_ref[HALO:HALO + tm, :] = glu_ref[...]

    span = tm + HALO - SUBLANES
    for rho in range(1, SUBLANES):
        shift_ref[rho, 0:span, :] = xpad_ref[rho:rho + span, :]
    off = HALO - (CONV_WIDTH - 1)
    acc = jnp.zeros((tm, CONV_DIM), F32) + dwb_ref[...]
    for t in range(CONV_WIDTH):
        rho, base = (off + t) % SUBLANES, (off + t) // SUBLANES * SUBLANES
        if rho == 0:
            win = xpad_ref[base:base + tm, :]
        else:
            win = shift_ref[rho, base:base + tm, :]
        acc = acc + dww_ref[t:t + 1, :] * win

    mu = jnp.mean(acc, axis=-1, keepdims=True)
    xc = acc - mu
    y = xc * lax.rsqrt(jnp.mean(xc * xc, axis=-1, keepdims=True) + NORM_EPS) * lnw_ref[...] + lnb_ref[...]
    y = y * _sigmoid(y)
    y_conv = _dot(y.astype(BF16), pw2_ref[...])
    g = g_ref[...].astype(F32)
    mixed = g[:, :1024] * y_conv + g[:, 1024:] * y_mla
    h = x_ref[...] + _dot(mixed.astype(BF16), wout_ref[...])
    h_ref[...] = h
    hn = _rms(h, fnw_ref[...])
    before()
    _to_row_tiles(hn_ref, hn)
    return _route(hn, whi_ref, wlo_ref, rb_ref, idx_ref, gate_ref)


def _mixdisp_body(glu_ref, halo_ref, glum_ref, o_ref, g_ref, x_ref, dww_ref, dwb_ref, lnw_ref, lnb_ref,
                  pw2_ref, wo_ref, wout_ref, fnw_ref, whi_ref, wlo_ref, rb_ref, wgu_ref, wd_ref,
                  h_ref, idx_ref, gate_ref, dest_ref, cnt_ref, wgu_out, wd_out, xs_ref,
                  xpad_ref, shift_ref, hn_slots, dvec_ref, dsm0_ref, dsm1_ref, carry_ref, sems, dsem,
                  *, tm, tiles_per_seq, n_tiles, cap, t_dma):
    i = pl.program_id(0)
    slot = i % 2
    other = 1 - slot
    groups = tm // SUBLANES

    dsm_refs = (dsm0_ref, dsm1_ref)

    def send(sl, grp, s, k):
        d = dsm_refs[sl][0, 0, k * tm + grp * SUBLANES + s]
        return pltpu.make_async_copy(_row_of(hn_slots.at[sl], grp, s), xs_ref.at[d], sems.at[sl])

    def sent(sl):
        return pltpu.make_async_copy(_row_of(hn_slots.at[0], 0, 0), xs_ref.at[0], sems.at[sl])

    @pl.when(i == 0)
    def _init():
        carry_ref[...] = jnp.zeros_like(carry_ref)

    def dispatch_previous_tile(sl):
        iters = groups // 2
        cast_rows = wgu_ref.shape[0] // iters

        def issue(it, c):
            for gg in range(2):
                grp = it * 2 + gg
                for s in range(SUBLANES):
                    for k in range(TOP_K):
                        send(sl, grp, s, k).start(priority=k % 2)
            rows = pl.ds(pl.multiple_of(it * cast_rows, cast_rows), cast_rows)
            wgu_out[rows, :] = wgu_ref[rows, :].astype(BF16)
            wd_out[rows, :] = wd_ref[rows, :].astype(BF16)
            return c

        lax.fori_loop(0, iters, issue, 0)

    for sl in range(2):
        pl.when((i >= 1) & (other == sl))(functools.partial(dispatch_previous_tile, sl))

    def _wait_slot():
        @pl.when(i >= 2)
        def _():
            _drain(sent(slot), TOP_K * tm)

    @pl.when(i < n_tiles)
    def _tile():
        idxs, e_iota = _mix_body(glu_ref, halo_ref, glum_ref, o_ref, g_ref, x_ref, dww_ref, dwb_ref, lnw_ref,
                                 lnb_ref, pw2_ref, wo_ref, wout_ref, fnw_ref, whi_ref, wlo_ref, rb_ref,
                                 h_ref, hn_slots.at[slot], idx_ref, gate_ref, xpad_ref, shift_ref,
                                 tm=tm, tiles_per_seq=tiles_per_seq, before=_wait_slot)
        hits = [e_iota == idxs[k] for k in range(TOP_K)]
        onehot = jnp.zeros(e_iota.shape, F32)
        for hk in hits:
            onehot = onehot + jnp.where(hk, 1.0, 0.0)
        r = lax.broadcasted_iota(jnp.int32, (tm, tm), 0)
        c = lax.broadcasted_iota(jnp.int32, (tm, tm), 1)
        before = jnp.where(r < c, 1.0, 0.0).astype(BF16)
        val = _dot(onehot.astype(BF16), before) + carry_ref[...]
        dests = [(jnp.sum(jnp.where(hk, val, 0.0), axis=0, keepdims=True) + idxs[k] * float(cap)).astype(jnp.int32)
                 for k, hk in enumerate(hits)]
        total = carry_ref[...] + jnp.sum(onehot, axis=1, keepdims=True)
        carry_ref[...] = total
        cnt_ref[...] = jnp.broadcast_to(total, cnt_ref.shape).astype(jnp.int32)
        for sub in range(tm // t_dma):
            dest_ref[sub] = jnp.concatenate([dk[:, sub * t_dma:(sub + 1) * t_dma] for dk in dests], axis=1)
        dvec_ref[0] = jnp.concatenate(dests, axis=1)
        for sl in range(2):
            @pl.when(slot == sl)
            def _to_smem():
                to_smem = pltpu.make_async_copy(dvec_ref, dsm_refs[sl], dsem)
                to_smem.start()
                to_smem.wait()

    @pl.when(i == n_tiles)
    def _drain_all():
        _drain(sent(other), TOP_K * tm)

        @pl.when(i >= 2)
        def _():
            _drain(sent(slot), TOP_K * tm)


def _mixdisp_call(glu, glum, o, g, x2, weights, wgu, wd, *, tm, tiles_per_seq, t_dma):
    n, d = x2.shape
    n_tiles = n // tm
    cap = n
    wgu2 = wgu.reshape(-1, wgu.shape[-1])
    wd2 = wd.reshape(-1, wd.shape[-1])
    w_rows = wgu2.shape[0] // n_tiles
    assert wgu2.shape[0] == wd2.shape[0] == w_rows * n_tiles
    assert (w_rows // (tm // SUBLANES // 2)) % BF16_ROWS == 0
    cur = lambda i: jnp.minimum(i, n_tiles - 1)
    prev = lambda i: jnp.maximum(i - 1, 0)
    full = lambda a: pl.BlockSpec(a.shape, lambda i: (0,) * a.ndim)
    row = lambda c: pl.BlockSpec((tm, c), lambda i: (cur(i), 0))
    halo = pl.BlockSpec((HALO, CONV_DIM), lambda i: (jnp.maximum(cur(i) * (tm // HALO) - 1, 0), 0))
    wblk = lambda a: pl.BlockSpec((w_rows, a.shape[1]), lambda i: (prev(i), 0))
    subs = tm // t_dma
    outs = pl.pallas_call(
        functools.partial(_mixdisp_body, tm=tm, tiles_per_seq=tiles_per_seq, n_tiles=n_tiles, cap=cap, t_dma=t_dma),
        grid=(n_tiles + 1,),
        in_specs=[row(CONV_DIM), halo, full(glum), row(N_HEADS * V_HEAD_DIM), row(C_GATE), row(d)]
                 + [full(w) for w in weights] + [wblk(wgu2), wblk(wd2)],
        out_specs=[row(d), pl.BlockSpec((TOP_K, tm), lambda i: (0, cur(i))),
                   pl.BlockSpec((2 * TOP_K, tm), lambda i: (0, cur(i))),
                   pl.BlockSpec((subs, 1, TOP_K * t_dma), lambda i: (cur(i), 0, 0)),
                   pl.BlockSpec((N_EXPERTS, LANES), lambda i: (0, 0)),
                   wblk(wgu2), wblk(wd2), pl.BlockSpec(memory_space=pl.ANY)],
        out_shape=[jax.ShapeDtypeStruct((n, d), F32), jax.ShapeDtypeStruct((TOP_K, n), jnp.int32),
                   jax.ShapeDtypeStruct((2 * TOP_K, n), F32),
                   jax.ShapeDtypeStruct((n // t_dma, 1, TOP_K * t_dma), jnp.int32),
                   jax.ShapeDtypeStruct((N_EXPERTS, LANES), jnp.int32),
                   jax.ShapeDtypeStruct(wgu2.shape, BF16), jax.ShapeDtypeStruct(wd2.shape, BF16),
                   jax.ShapeDtypeStruct((N_EXPERTS * cap, ROW_CHUNKS, LANES), F32)],
        scratch_shapes=[pltpu.VMEM((HALO + tm, CONV_DIM), F32),
                        pltpu.VMEM((SUBLANES, HALO + tm, CONV_DIM), F32),
                        pltpu.VMEM((2,) + _tile_view_shape(tm), F32),
                        pltpu.VMEM((1, 1, TOP_K * tm), jnp.int32),
                        pltpu.SMEM((1, 1, TOP_K * tm), jnp.int32), pltpu.SMEM((1, 1, TOP_K * tm), jnp.int32),
                        pltpu.VMEM((N_EXPERTS, 1), F32),
                        pltpu.SemaphoreType.DMA((2,)), pltpu.SemaphoreType.DMA],
        compiler_params=_cparams(("arbitrary",)),
        name="mixdisp",
    )(glu, glu, glum, o, g, x2, *weights, wgu2, wd2)
    h, idx, gates, dest, cnt, wgu_b, wd_b, xs = outs
    return h, gates, dest, cnt, wgu_b.reshape(wgu.shape), wd_b.reshape(wd.shape), xs


WAIT_GROUP = 32


def _drain(copy, count):
    def group(_, carry):
        for _ in range(WAIT_GROUP):
            copy.wait()
        return carry

    lax.fori_loop(0, count // WAIT_GROUP, group, 0)


def _plan_body(cnt_ref, bexp_ref, brow_ref, nact_ref, *, bm, n_blocks, cap_blocks):
    def per_expert(e, carry):
        blk, last = carry
        nb = (cnt_ref[e] + (bm - 1)) // bm

        def fill(b, _):
            bexp_ref[b] = e
            brow_ref[b] = e * cap_blocks + (b - blk)
            return 0

        lax.fori_loop(blk, blk + nb, fill, 0)
        return blk + nb, jnp.where(nb > 0, e, last)

    n_act, last = lax.fori_loop(0, N_EXPERTS, per_expert, (jnp.int32(0), jnp.int32(0)))
    nact_ref[0] = n_act

    def tail(b, _):
        bexp_ref[b] = last
        brow_ref[b] = 0
        return 0

    lax.fori_loop(n_act, n_blocks, tail, 0)


def _plan_call(cnt, *, bm, n_blocks, cap_blocks):
    smem = lambda: pl.BlockSpec(memory_space=pltpu.SMEM)
    blocks = jax.ShapeDtypeStruct((n_blocks,), jnp.int32)
    return pl.pallas_call(
        functools.partial(_plan_body, bm=bm, n_blocks=n_blocks, cap_blocks=cap_blocks),
        in_specs=[smem()],
        out_specs=[smem(), smem(), smem()],
        out_shape=[blocks, blocks, jax.ShapeDtypeStruct((1,), jnp.int32)],
        name="plan",
    )(cnt)


def _expert_body(bexp_ref, brow_ref, nact_ref, xs_ref, wgu_ref, bgu_ref, wd_ref, bd_ref, ys_ref,
                 xin_ref, yout_ref, in_sems, out_sems, *, bm):
    b = pl.program_id(0)
    n_act = nact_ref[0]
    slot = b % 2
    other = 1 - slot

    def chunk(c):
        return slice(c * LANES, (c + 1) * LANES)

    def block_rows(blk):
        return pl.ds(pl.multiple_of(brow_ref[blk] * bm, bm), bm)

    def fetch(blk, sl, c):
        return pltpu.make_async_copy(xs_ref.at[block_rows(blk), c], xin_ref.at[sl, :, chunk(c)], in_sems.at[sl])

    def write_back(blk, sl, c):
        return pltpu.make_async_copy(yout_ref.at[sl, :, chunk(c)], ys_ref.at[block_rows(blk), c], out_sems.at[sl])

    @pl.when(b == 0)
    def _prime():
        for c in range(ROW_CHUNKS):
            fetch(0, 0, c).start()

    @pl.when(b < n_act)
    def _block():
        for c in range(ROW_CHUNKS):
            fetch(b, slot, c).wait()

        @pl.when(b + 1 < n_act)
        def _prefetch():
            for c in range(ROW_CHUNKS):
                fetch(b + 1, other, c).start()

        @pl.when(b >= 2)
        def _free_result_slot():
            for c in range(ROW_CHUNKS):
                write_back(b - 2, slot, c).wait()

        x = xin_ref[slot].astype(BF16)
        gu = _dot(x, wgu_ref[0]) + bgu_ref[0]
        g = jnp.minimum(gu[:, :D_FF], SWIGLU_LIMIT)
        u = jnp.clip(gu[:, D_FF:], -SWIGLU_LIMIT, SWIGLU_LIMIT)
        act = (u + 1.0) * (g * _sigmoid(SWIGLU_ALPHA * g))
        yout_ref[slot] = _dot(act.astype(BF16), wd_ref[0]) + bd_ref[0]
        for c in range(ROW_CHUNKS):
            write_back(b, slot, c).start()

    @pl.when(b == n_act)
    def _drain_results():
        for c in range(ROW_CHUNKS):
            write_back(b - 1, other, c).wait()

        @pl.when(b >= 2)
        def _older():
            for c in range(ROW_CHUNKS):
                write_back(b - 2, slot, c).wait()


def _expert_call(bexp, brow, nact, xs, wgu, bgu, wd, bd, *, bm):
    d = ROW_CHUNKS * LANES
    n_blocks = bexp.shape[0]
    wsel = lambda b, be, br, na: (be[jnp.minimum(b, n_blocks - 1)], 0, 0)
    hbm = pl.BlockSpec(memory_space=pl.ANY)
    return pl.pallas_call(
        functools.partial(_expert_body, bm=bm),
        grid_spec=pltpu.PrefetchScalarGridSpec(
            num_scalar_prefetch=3, grid=(n_blocks + 1,),
            in_specs=[hbm, pl.BlockSpec((1, d, 2 * D_FF), wsel), pl.BlockSpec((1, 1, 2 * D_FF), wsel),
                      pl.BlockSpec((1, D_FF, d), wsel), pl.BlockSpec((1, 1, d), wsel)],
            out_specs=hbm,
            scratch_shapes=[pltpu.VMEM((2, bm, d), F32), pltpu.VMEM((2, bm, d), F32),
                            pltpu.SemaphoreType.DMA((2,)), pltpu.SemaphoreType.DMA((2,))]),
        out_shape=jax.ShapeDtypeStruct(xs.shape, F32),
        input_output_aliases={3: 0},
        compiler_params=_cparams(("arbitrary",)),
        name="experts",
    )(bexp, brow, nact, xs, wgu, bgu, wd, bd)


def _combine_body(dest_ref, ys_ref, gate_ref, h_ref, out_ref, buf_ref, gt_ref, sems, *, t, steps):
    i = pl.program_id(0)
    slot = i % 2
    other = 1 - slot

    def row_gather(d, k, grp, s, sl):
        return pltpu.make_async_copy(ys_ref.at[d], _row_of(buf_ref.at[sl, k], grp, s), sems.at[sl])

    @pl.when(i == 0)
    def _first():
        buf_ref[...] = jnp.zeros_like(buf_ref)

    @pl.when(i > 0)
    def _wait_previous_tile():
        _drain(row_gather(0, 0, 0, 0, other), TOP_K * t)

    gt_ref[...] = jnp.transpose(gate_ref[...])

    def body(grp, carry):
        for s in range(SUBLANES):
            for k in range(TOP_K):
                d = dest_ref[0, 0, k * t + grp * SUBLANES + s]
                row_gather(d, k, grp, s, slot).start(priority=k % 2)
        rows = pl.ds(pl.multiple_of(grp * SUBLANES, SUBLANES), SUBLANES)
        acc = h_ref[rows, :]
        g = gt_ref[rows, :]
        for k in range(TOP_K):
            tiles = buf_ref[other, k, grp]
            acc = acc + g[:, k:k + 1] * jnp.concatenate([tiles[c] for c in range(ROW_CHUNKS)], axis=1)
        out_ref[rows, :] = acc
        return carry

    lax.fori_loop(0, t // SUBLANES, body, 0)

    @pl.when(i == steps)
    def _drain_extra_gather():
        _drain(row_gather(0, 0, 0, 0, slot), TOP_K * t)


def _combine_call(dest, ys, gates, h, *, t):
    n, d = h.shape
    steps = n // t
    prev = lambda i: jnp.maximum(i - 1, 0)
    return pl.pallas_call(
        functools.partial(_combine_body, t=t, steps=steps),
        grid=(steps + 1,),
        in_specs=[pl.BlockSpec((1, 1, TOP_K * t), lambda i: (jnp.minimum(i, steps - 1), 0, 0),
                               memory_space=pltpu.SMEM),
                  pl.BlockSpec(memory_space=pl.ANY),
                  pl.BlockSpec((2 * TOP_K, t), lambda i: (0, prev(i))),
                  pl.BlockSpec((t, d), lambda i: (prev(i), 0))],
        out_specs=pl.BlockSpec((t, d), lambda i: (prev(i), 0)),
        out_shape=jax.ShapeDtypeStruct((n, d), F32),
        scratch_shapes=[pltpu.VMEM((2, TOP_K) + _tile_view_shape(t), F32), pltpu.VMEM((t, 2 * TOP_K), F32),
                        pltpu.SemaphoreType.DMA((2,))],
        compiler_params=_cparams(("arbitrary",)),
        name="combine",
    )(dest, ys, gates, h)


def _rot_half(z):
    half = QK_ROPE_DIM // 2
    return jnp.concatenate([-z[:, half:], z[:, :half]], axis=1)


def _pe_slot(z):
    rows = z.shape[0]
    return jnp.concatenate([jnp.zeros((rows, QK_NOPE_DIM), F32), z,
                            jnp.zeros((rows, LANES - QK_HEAD_DIM), F32)], axis=1)


def _prep_proj_weights(attn_norm_w, w_in, q_a_norm_w, w_q_b, kv_a_norm_w, w_kv_b, q_norm_w, k_norm_w, gate_b):
    c0 = 2 * CONV_DIM
    c1 = c0 + Q_LORA_RANK
    c2 = c1 + KV_LORA_RANK
    c3 = c2 + QK_ROPE_DIM
    kpe = w_in[:, c2:c3]
    win = jnp.concatenate([w_in[:, :c2].astype(BF16), _pe_slot(kpe).astype(BF16),
                           _pe_slot(_rot_half(kpe * k_norm_w[None, QK_NOPE_DIM:])).astype(BF16),
                           w_in[:, c3:].astype(BF16)], axis=1)

    pad = jnp.zeros((Q_LORA_RANK, LANES - QK_HEAD_DIM), F32)
    q1, q2 = [], []
    for h in range(N_HEADS):
        cols = w_q_b[:, h * QK_HEAD_DIM:(h + 1) * QK_HEAD_DIM]
        q1.append(jnp.concatenate([cols, pad], axis=1))
        q2.append(_pe_slot(_rot_half(cols[:, QK_NOPE_DIM:] * q_norm_w[None, QK_NOPE_DIM:])))
    wq = jnp.concatenate(q1 + q2, axis=1).astype(BF16)

    kpad = jnp.zeros((KV_LORA_RANK, LANES - QK_NOPE_DIM), F32)
    ks, vs = [], []
    per_head = QK_NOPE_DIM + V_HEAD_DIM
    for h in range(N_HEADS):
        cols = w_kv_b[:, h * per_head:(h + 1) * per_head]
        ks.append(jnp.concatenate([cols[:, :QK_NOPE_DIM], kpad], axis=1))
        vs.append(cols[:, QK_NOPE_DIM:])
    wk = jnp.concatenate(ks, axis=1).astype(BF16)
    wvt = jnp.concatenate(vs, axis=1).T.astype(BF16)

    lane_pad = jnp.zeros((LANES - QK_HEAD_DIM,), F32)
    qlane = jnp.concatenate([q_norm_w, lane_pad])[None, :]
    klane = jnp.concatenate([k_norm_w, lane_pad])[None, :]
    return (attn_norm_w[None, :], win, q_a_norm_w[None, :], wq, kv_a_norm_w[None, :], wk, wvt, qlane, klane,
            gate_b[None, :])


def _rope_tables(length):
    half = QK_ROPE_DIM // 2
    inv_freq = ROPE_THETA ** (-jnp.arange(half, dtype=F32) / half)
    ang = jnp.arange(length, dtype=F32)[:, None] * inv_freq[None, :]
    cos, sin = jnp.cos(ang), jnp.sin(ang)
    ones = jnp.ones((length, QK_NOPE_DIM), F32)
    tail = LANES - QK_HEAD_DIM
    cos_t = jnp.concatenate([ones, cos, cos, jnp.ones((length, tail), F32)], axis=1)
    sin_t = jnp.concatenate([0.0 * ones, sin, sin, jnp.zeros((length, tail), F32)], axis=1)
    return cos_t, sin_t


def _tile(n, pref):
    t = pref
    while n % t:
        t //= 2
    return t


def kernel(x, meta_tokens, attn_norm_w, w_in, conv_dw_w, conv_dw_b, conv_ln_w, conv_ln_b, conv_pw2_w, q_a_norm_w, w_q_b, kv_a_norm_w, w_kv_b, q_norm_w, k_norm_w, w_o_mla, gate_b, w_out, ffn_norm_w, router_w, router_b, w_gate_up, b_gate_up, w_down, b_down):
    assert attn_norm_w.shape[0] == 1, "one layer: rows of meta tokens never feed a later layer"
    b, s, d = x.shape
    n = b * s
    x2 = x.reshape(n, d)

    tm = _tile(s, MIX_TILE)
    tq = _tile(s, 512)
    cos_t, sin_t = _rope_tables(N_META + s)
    pw = _prep_proj_weights(attn_norm_w[0], w_in[0], q_a_norm_w[0], w_q_b[0], kv_a_norm_w[0], w_kv_b[0],
                            q_norm_w[0], k_norm_w[0], gate_b[0])

    tp = _tile(s, 512)
    glu, q, k, vt, g = _proj_call(x2, (cos_t[N_META:], sin_t[N_META:]), pw, tm=tp, tiles_per_seq=s // tp)
    glum, _, km, vmt, _ = _proj_call(meta_tokens.astype(F32), (cos_t[:N_META], sin_t[:N_META]), pw, tm=N_META,
                                     tiles_per_seq=1)

    o = _attn_call(q, k, vt, km, vmt, tq=tq, batch=b)

    rw = router_w[0].T
    rw_hi = rw.astype(BF16)
    rw_lo = (rw - rw_hi.astype(F32)).astype(BF16)
    mix_w = (conv_dw_w[0], conv_dw_b[0][None, :], conv_ln_w[0][None, :], conv_ln_b[0][None, :],
             conv_pw2_w[0].astype(BF16), w_o_mla[0].astype(BF16), w_out[0].astype(BF16), ffn_norm_w[0][None, :],
             rw_hi, rw_lo, router_b[0][:, None])
    t_dma = _tile(tm, 256)
    h, gates, dest, cnt, wgu_b, wd_b, xs = _mixdisp_call(glu, glum, o, g, x2, mix_w, w_gate_up[0], w_down[0], tm=tm,
                                                         tiles_per_seq=s // tm, t_dma=t_dma)

    bm = 512
    n_blocks = (n * TOP_K) // bm + N_EXPERTS
    bexp, brow, nact = _plan_call(cnt[:, 0], bm=bm, n_blocks=n_blocks, cap_blocks=n // bm)
    ys = _expert_call(bexp, brow, nact, xs, wgu_b, b_gate_up[0][:, None, :], wd_b, b_down[0][:, None, :], bm=bm)
    out = _combine_call(dest, ys, gates, h, t=t_dma)
    return out.reshape(b, s, d)
```

```python
import functools
import math

import jax
import jax.numpy as jnp
from jax import lax
from jax.experimental import pallas as pl
from jax.experimental.pallas import tpu as pltpu

N_META = 16
CONV_DIM = 512
CONV_WIDTH = 31
N_HEADS = 8
QK_NOPE_DIM = 64
QK_ROPE_DIM = 32
QK_HEAD_DIM = QK_NOPE_DIM + QK_ROPE_DIM
V_HEAD_DIM = 64
Q_LORA_RANK = 256
KV_LORA_RANK = 128
ROPE_THETA = 10000.0
N_EXPERTS = 32
TOP_K = 4
D_FF = 1024
SWIGLU_LIMIT = 7.0
SWIGLU_ALPHA = 1.702
NORM_EPS = 1e-6
NEG_INF = -1e30

LANES = 128
SUBLANES = 8
BF16_ROWS = 16
ROW_CHUNKS = 8
SCORE_LOOKAHEAD = 2
MIX_TILE = 512
HALO = 32
VMEM_LIMIT = 56 * 1024 * 1024

F32 = jnp.float32
BF16 = jnp.bfloat16


def _cparams(sem):
    return pltpu.CompilerParams(dimension_semantics=sem, vmem_limit_bytes=VMEM_LIMIT)


def _dot(a, b):
    return jnp.dot(a, b, preferred_element_type=F32)


def _dot_t(a, b):
    return lax.dot_general(a, b, (((1,), (1,)), ((), ())), preferred_element_type=F32)


def _sigmoid(x):
    return 1.0 / (1.0 + jnp.exp(-x))


def _rms(x, w):
    return x * lax.rsqrt(jnp.mean(x * x, axis=-1, keepdims=True) + NORM_EPS) * w


C_CONV = 2 * CONV_DIM
C_LAT = Q_LORA_RANK + KV_LORA_RANK + 2 * LANES
C_GATE = 2 * 1024


def _proj_body(x_ref, anw_ref, win_ref, qaw_ref, wq_ref, kvaw_ref, wk_ref, wvt_ref, qlane_ref, klane_ref,
               cos_ref, sin_ref, gb_ref, glu_ref, q_ref, k_ref, vt_ref, g_ref, *, q_scale):
    x = x_ref[...]
    xn = _rms(x, anw_ref[...]).astype(BF16)

    lat = _dot(xn, win_ref[:, C_CONV:C_CONV + C_LAT])
    q_lat = lat[:, :Q_LORA_RANK]
    c_kv = lat[:, Q_LORA_RANK:Q_LORA_RANK + KV_LORA_RANK]
    kpe = lat[:, Q_LORA_RANK + KV_LORA_RANK:Q_LORA_RANK + KV_LORA_RANK + LANES]
    kpe_rot = lat[:, Q_LORA_RANK + KV_LORA_RANK + LANES:]

    qn = _rms(q_lat, qaw_ref[...]).astype(BF16)
    cn = _rms(c_kv, kvaw_ref[...]).astype(BF16)
    qq = _dot(qn, wq_ref[...])
    kv = _dot(cn, wk_ref[...])
    vt_ref[...] = _dot_t(wvt_ref[...], cn).astype(vt_ref.dtype)
    u = _dot(xn, win_ref[:, 0:C_CONV])

    cos = cos_ref[...]
    sin = sin_ref[...]
    q_cos = qlane_ref[...] * cos
    for h in range(N_HEADS):
        q1 = qq[:, h * LANES:(h + 1) * LANES]
        q2 = qq[:, (N_HEADS + h) * LANES:(N_HEADS + h + 1) * LANES]
        ss = jnp.sum(q1 * q1, axis=-1, keepdims=True)
        s = lax.rsqrt(ss * (1.0 / QK_HEAD_DIM) + NORM_EPS) * q_scale
        q_ref[:, h * LANES:(h + 1) * LANES] = (s * (q1 * q_cos + q2 * sin)).astype(q_ref.dtype)

    gl = _dot(xn, win_ref[:, C_CONV + C_LAT:]) + gb_ref[...]

    k_cos = klane_ref[...] * cos
    ss_pe = jnp.sum(kpe * kpe, axis=-1, keepdims=True)
    k_pe_roped = kpe * k_cos + kpe_rot * sin
    for h in range(N_HEADS):
        kn = kv[:, h * LANES:(h + 1) * LANES]
        ss = jnp.sum(kn * kn, axis=-1, keepdims=True) + ss_pe
        s = lax.rsqrt(ss * (1.0 / QK_HEAD_DIM) + NORM_EPS)
        k_ref[:, h * LANES:(h + 1) * LANES] = (s * (kn * k_cos + k_pe_roped)).astype(k_ref.dtype)

    glu_ref[...] = u[:, :CONV_DIM] * _sigmoid(u[:, CONV_DIM:])
    g_ref[...] = _sigmoid(gl).astype(g_ref.dtype)


def _proj_call(x2, tables, weights, *, tm, tiles_per_seq):
    n = x2.shape[0]
    d = x2.shape[1]
    cos_t, sin_t = tables
    (anw, win, qaw, wq, kvaw, wk, wvt, qlane, klane, gb) = weights
    full = lambda a: pl.BlockSpec(a.shape, lambda i: (0,) * a.ndim)
    row = lambda c: pl.BlockSpec((tm, c), lambda i: (i, 0))
    pos = pl.BlockSpec((tm, LANES), lambda i: (i % tiles_per_seq, 0))
    q_scale = (QK_HEAD_DIM ** -0.5) * math.log2(math.e)
    return pl.pallas_call(
        functools.partial(_proj_body, q_scale=q_scale),
        grid=(n // tm,),
        in_specs=[row(d), full(anw), full(win), full(qaw), full(wq), full(kvaw), full(wk), full(wvt),
                  full(qlane), full(klane), pos, pos, full(gb)],
        out_specs=[row(CONV_DIM), row(N_HEADS * LANES), row(N_HEADS * LANES),
                   pl.BlockSpec((N_HEADS * V_HEAD_DIM, tm), lambda i: (0, i)), row(C_GATE)],
        out_shape=[jax.ShapeDtypeStruct((n, CONV_DIM), F32),
                   jax.ShapeDtypeStruct((n, N_HEADS * LANES), BF16),
                   jax.ShapeDtypeStruct((n, N_HEADS * LANES), BF16),
                   jax.ShapeDtypeStruct((N_HEADS * V_HEAD_DIM, n), BF16),
                   jax.ShapeDtypeStruct((n, C_GATE), BF16)],
        compiler_params=_cparams(("parallel",)),
        name="proj",
    )(x2, anw, win, qaw, wq, kvaw, wk, wvt, qlane, klane, cos_t, sin_t, gb)


def _attn_body(q_ref, k_ref, vt_ref, km_ref, vmt_ref, o_ref, m_ref, l_ref, acc_ref):
    i = pl.program_id(1)
    tq = q_ref.shape[0]

    def head_slices(h):
        return slice(h * LANES, (h + 1) * LANES), slice(h * V_HEAD_DIM, (h + 1) * V_HEAD_DIM)

    def _meta():
        ms, ls = [], []
        sts = [_dot_t(km_ref[:, head_slices(h)[0]], q_ref[:, head_slices(h)[0]]) for h in range(N_HEADS)]
        for h in range(N_HEADS):
            ks, vs = head_slices(h)
            st = sts[h]
            m = jnp.max(st, axis=0, keepdims=True)
            p = jnp.exp2(st - m)
            ms.append(m)
            ls.append(jnp.sum(p, axis=0, keepdims=True))
            acc_ref[h] = _dot(vmt_ref[vs, :], p.astype(BF16))
        m_ref[...] = jnp.concatenate(ms, axis=0)
        l_ref[...] = jnp.concatenate(ls, axis=0)

    def step(j, diagonal):
        keys = pl.ds(pl.multiple_of(j * tq, tq), tq)

        def scores(h):
            ks, _ = head_slices(h)
            return _dot_t(k_ref[keys, ks], q_ref[:, ks])

        m_all = m_ref[...]
        l_all = l_ref[...]
        ms, ls = [], []
        pending = [scores(h) for h in range(SCORE_LOOKAHEAD)]
        for h in range(N_HEADS):
            _, vs = head_slices(h)
            st = pending.pop(0)
            if h + SCORE_LOOKAHEAD < N_HEADS:
                pending.append(scores(h + SCORE_LOOKAHEAD))
            if diagonal:
                key = lax.broadcasted_iota(jnp.int32, st.shape, 0)
                qry = lax.broadcasted_iota(jnp.int32, st.shape, 1)
                st = jnp.where(key <= qry, st, NEG_INF)
            m_prev = m_all[h:h + 1, :]
            m_new = jnp.maximum(m_prev, jnp.max(st, axis=0, keepdims=True))
            alpha = jnp.exp2(m_prev - m_new)
            p = jnp.exp2(st - m_new)
            l_new = alpha * l_all[h:h + 1, :] + jnp.sum(p, axis=0, keepdims=True)
            acc = alpha * acc_ref[h] + _dot(vt_ref[vs, keys], p.astype(BF16))
            if diagonal:
                acc_ref[h] = acc / l_new
            else:
                acc_ref[h] = acc
                ms.append(m_new)
                ls.append(l_new)
        if not diagonal:
            m_ref[...] = jnp.concatenate(ms, axis=0)
            l_ref[...] = jnp.concatenate(ls, axis=0)

    def full_step(j, carry):
        step(j, False)
        return carry

    _meta()
    lax.fori_loop(0, i, full_step, 0)
    step(i, True)
    ot = acc_ref[...].reshape(N_HEADS * V_HEAD_DIM, tq)
    o_ref[...] = jnp.transpose(ot).astype(o_ref.dtype)


def _attn_call(q, k, vt, km, vmt, *, tq, batch):
    n = q.shape[0]
    s = n // batch
    nq = s // tq
    const = lambda bi, i: (0, 0)
    return pl.pallas_call(
        _attn_body,
        grid=(batch, nq),
        in_specs=[pl.BlockSpec((tq, N_HEADS * LANES), lambda bi, i: (bi * nq + i, 0)),
                  pl.BlockSpec((s, N_HEADS * LANES), lambda bi, i: (bi, 0)),
                  pl.BlockSpec((N_HEADS * V_HEAD_DIM, s), lambda bi, i: (0, bi)),
                  pl.BlockSpec(km.shape, const), pl.BlockSpec(vmt.shape, const)],
        out_specs=pl.BlockSpec((tq, N_HEADS * V_HEAD_DIM), lambda bi, i: (bi * nq + i, 0)),
        out_shape=jax.ShapeDtypeStruct((n, N_HEADS * V_HEAD_DIM), BF16),
        scratch_shapes=[pltpu.VMEM((N_HEADS, tq), F32), pltpu.VMEM((N_HEADS, tq), F32),
                        pltpu.VMEM((N_HEADS, V_HEAD_DIM, tq), F32)],
        compiler_params=_cparams(("parallel", "arbitrary")),
        name="attn",
    )(q, k, vt, km, vmt)


def _tile_view_shape(rows):
    return (rows // SUBLANES, ROW_CHUNKS, SUBLANES, LANES)


def _to_row_tiles(ref, val):
    groups = val.shape[0] // SUBLANES
    for c in range(ROW_CHUNKS):
        ref[:, c] = val[:, c * LANES:(c + 1) * LANES].reshape(groups, SUBLANES, LANES)


def _from_row_tiles(ref):
    rows = ref.shape[0] * SUBLANES
    return jnp.concatenate([ref[:, c].reshape(rows, LANES) for c in range(ROW_CHUNKS)], axis=1)


def _row_of(ref, row_group, sublane):
    return ref.at[row_group, :, sublane, :]


def _route(hn, whi_ref, wlo_ref, rb_ref, idx_ref, gate_ref):
    x_hi = hn.astype(BF16)
    x_lo = (hn - x_hi.astype(F32)).astype(BF16)
    w_hi = whi_ref[...]
    logits = _dot_t(w_hi, x_hi) + _dot_t(w_hi, x_lo) + _dot_t(wlo_ref[...], x_hi) + rb_ref[...]

    e_iota = lax.broadcasted_iota(jnp.int32, logits.shape, 0).astype(F32)
    vals, idxs = [], []
    cur = logits
    for _ in range(TOP_K):
        m = jnp.max(cur, axis=0, keepdims=True)
        idx = jnp.min(jnp.where(cur == m, e_iota, float(N_EXPERTS)), axis=0, keepdims=True)
        vals.append(m)
        idxs.append(idx)
        cur = jnp.where(e_iota == idx, -jnp.inf, cur)
    exps = [jnp.exp(v - vals[0]) for v in vals]
    denom = exps[0] + exps[1] + exps[2] + exps[3]
    idx_ref[...] = jnp.concatenate(idxs, axis=0).astype(jnp.int32)
    gate_ref[...] = jnp.concatenate([e / denom for e in exps] + [jnp.zeros_like(denom)] * 4, axis=0)
    return idxs, e_iota


CONV_ROWS = 2 * SUBLANES


def _conv_prepare(glu_ref, halo_ref, glum_ref, xpad_ref, shift_ref, *, tm, tiles_per_seq):
    first = (pl.program_id(0) % tiles_per_seq) == 0
    meta_ctx = jnp.concatenate([jnp.zeros((HALO - N_META, CONV_DIM), F32), glum_ref[...]], axis=0)
    xpad_ref[0:HALO, :] = jnp.where(first, meta_ctx, halo_ref[...])
    xpad_ref[HALO:HALO + tm, :] = glu_ref[...]
    span = tm + HALO - SUBLANES
    for rho in range(1, SUBLANES):
        shift_ref[rho, 0:span, :] = xpad_ref[rho:rho + span, :]


def _conv_rows(dww_ref, dwb_ref, xpad_ref, shift_ref, conv_ref, r0):
    off = HALO - (CONV_WIDTH - 1)
    acc = jnp.zeros((CONV_ROWS, CONV_DIM), F32) + dwb_ref[...]
    for t in range(CONV_WIDTH):
        rho, base = (off + t) % SUBLANES, (off + t) // SUBLANES * SUBLANES
        rows = pl.ds(pl.multiple_of(base + r0, SUBLANES), CONV_ROWS)
        win = xpad_ref[rows, :] if rho == 0 else shift_ref[rho, rows, :]
        acc = acc + dww_ref[t:t + 1, :] * win
    conv_ref[pl.ds(pl.multiple_of(r0, CONV_ROWS), CONV_ROWS), :] = acc


def _mix_tail(conv_ref, o_ref, g_ref, x_ref, lnw_ref, lnb_ref, pw2_ref, wo_ref, wout_ref, fnw_ref, whi_ref, wlo_ref,
              rb_ref, h_ref, hn_ref, idx_ref, gate_ref, *, before):
    y_mla = _dot(o_ref[...], wo_ref[...])
    acc = conv_ref[...]
    mu = jnp.mean(acc, axis=-1, keepdims=True)
    xc = acc - mu
    y = xc * lax.rsqrt(jnp.mean(xc * xc, axis=-1, keepdims=True) + NORM_EPS) * lnw_ref[...] + lnb_ref[...]
    y = y * _sigmoid(y)
    y_conv = _dot(y.astype(BF16), pw2_ref[...])
    g = g_ref[...].astype(F32)
    mixed = g[:, :1024] * y_conv + g[:, 1024:] * y_mla
    h = x_ref[...] + _dot(mixed.astype(BF16), wout_ref[...])
    h_ref[...] = h
    hn = _rms(h, fnw_ref[...])
    before()
    _to_row_tiles(hn_ref, hn)
    return _route(hn, whi_ref, wlo_ref, rb_ref, idx_ref, gate_ref)


def _mixdisp_body(glu_ref, halo_ref, glum_ref, o_ref, g_ref, x_ref, dww_ref, dwb_ref, lnw_ref, lnb_ref,
                  pw2_ref, wo_ref, wout_ref, fnw_ref, whi_ref, wlo_ref, rb_ref, wgu_ref, wd_ref,
                  h_ref, idx_ref, gate_ref, dest_ref, cnt_ref, wgu_out, wd_out, xs_ref,
                  xpad_ref, shift_ref, conv_ref, hn_slots, dvec_ref, dsm0_ref, dsm1_ref, carry_ref, sems, dsem,
                  *, tm, tiles_per_seq, n_tiles, cap, t_dma):
    i = pl.program_id(0)
    slot = i % 2
    other = 1 - slot
    groups = tm // SUBLANES

    dsm_refs = (dsm0_ref, dsm1_ref)

    def send(sl, grp, s, k):
        d = dsm_refs[sl][0, 0, k * tm + grp * SUBLANES + s]
        return pltpu.make_async_copy(_row_of(hn_slots.at[sl], grp, s), xs_ref.at[d], sems.at[sl])

    def sent(sl):
        return pltpu.make_async_copy(_row_of(hn_slots.at[0], 0, 0), xs_ref.at[0], sems.at[sl])

    @pl.when(i == 0)
    def _init():
        carry_ref[...] = jnp.zeros_like(carry_ref)

    @pl.when(i < n_tiles)
    def _prepare():
        _conv_prepare(glu_ref, halo_ref, glum_ref, xpad_ref, shift_ref, tm=tm, tiles_per_seq=tiles_per_seq)

    iters = tm // CONV_ROWS

    def sweep(sl, sends, conv):
        cast_rows = wgu_ref.shape[0] // iters

        def body(it, c):
            if sends:
                for gg in range(CONV_ROWS // SUBLANES):
                    grp = it * (CONV_ROWS // SUBLANES) + gg
                    for s in range(SUBLANES):
                        for k in range(TOP_K):
                            send(sl, grp, s, k).start(priority=k % 2)
                rows = pl.ds(pl.multiple_of(it * cast_rows, cast_rows), cast_rows)
                wgu_out[rows, :] = wgu_ref[rows, :].astype(BF16)
                wd_out[rows, :] = wd_ref[rows, :].astype(BF16)
            if conv:
                _conv_rows(dww_ref, dwb_ref, xpad_ref, shift_ref, conv_ref, it * CONV_ROWS)
            return c

        lax.fori_loop(0, iters, body, 0)

    pl.when(i == 0)(functools.partial(sweep, 0, False, True))
    for sl in range(2):
        pl.when((i >= 1) & (i < n_tiles) & (other == sl))(functools.partial(sweep, sl, True, True))
        pl.when((i == n_tiles) & (other == sl))(functools.partial(sweep, sl, True, False))

    def _wait_slot():
        @pl.when(i >= 2)
        def _():
            _drain(sent(slot), TOP_K * tm)

    @pl.when(i < n_tiles)
    def _tile():
        idxs, e_iota = _mix_tail(conv_ref, o_ref, g_ref, x_ref, lnw_ref, lnb_ref, pw2_ref, wo_ref, wout_ref, fnw_ref,
                                 whi_ref, wlo_ref, rb_ref, h_ref, hn_slots.at[slot], idx_ref, gate_ref,
                                 before=_wait_slot)
        hits = [e_iota == idxs[k] for k in range(TOP_K)]
        onehot = jnp.zeros(e_iota.shape, F32)
        for hk in hits:
            onehot = onehot + jnp.where(hk, 1.0, 0.0)
        r = lax.broadcasted_iota(jnp.int32, (tm, tm), 0)
        c = lax.broadcasted_iota(jnp.int32, (tm, tm), 1)
        before = jnp.where(r < c, 1.0, 0.0).astype(BF16)
        val = _dot(onehot.astype(BF16), before) + carry_ref[...]
        dests = [(jnp.sum(jnp.where(hk, val, 0.0), axis=0, keepdims=True) + idxs[k] * float(cap)).astype(jnp.int32)
                 for k, hk in enumerate(hits)]
        total = carry_ref[...] + jnp.sum(onehot, axis=1, keepdims=True)
        carry_ref[...] = total
        cnt_ref[...] = jnp.broadcast_to(total, cnt_ref.shape).astype(jnp.int32)
        for sub in range(tm // t_dma):
            dest_ref[sub] = jnp.concatenate([dk[:, sub * t_dma:(sub + 1) * t_dma] for dk in dests], axis=1)
        dvec_ref[0] = jnp.concatenate(dests, axis=1)
        for sl in range(2):
            @pl.when(slot == sl)
            def _to_smem():
                to_smem = pltpu.make_async_copy(dvec_ref, dsm_refs[sl], dsem)
                to_smem.start()
                to_smem.wait()

    @pl.when(i == n_tiles)
    def _drain_all():
        _drain(sent(other), TOP_K * tm)

        @pl.when(i >= 2)
        def _():
            _drain(sent(slot), TOP_K * tm)


def _mixdisp_call(glu, glum, o, g, x2, weights, wgu, wd, *, tm, tiles_per_seq, t_dma):
    n, d = x2.shape
    n_tiles = n // tm
    cap = n
    wgu2 = wgu.reshape(-1, wgu.shape[-1])
    wd2 = wd.reshape(-1, wd.shape[-1])
    w_rows = wgu2.shape[0] // n_tiles
    assert wgu2.shape[0] == wd2.shape[0] == w_rows * n_tiles
    assert (w_rows // (tm // SUBLANES // 2)) % BF16_ROWS == 0
    cur = lambda i: jnp.minimum(i, n_tiles - 1)
    prev = lambda i: jnp.maximum(i - 1, 0)
    full = lambda a: pl.BlockSpec(a.shape, lambda i: (0,) * a.ndim)
    row = lambda c: pl.BlockSpec((tm, c), lambda i: (cur(i), 0))
    halo = pl.BlockSpec((HALO, CONV_DIM), lambda i: (jnp.maximum(cur(i) * (tm // HALO) - 1, 0), 0))
    wblk = lambda a: pl.BlockSpec((w_rows, a.shape[1]), lambda i: (prev(i), 0))
    subs = tm // t_dma
    outs = pl.pallas_call(
        functools.partial(_mixdisp_body, tm=tm, tiles_per_seq=tiles_per_seq, n_tiles=n_tiles, cap=cap, t_dma=t_dma),
        grid=(n_tiles + 1,),
        in_specs=[row(CONV_DIM), halo, full(glum), row(N_HEADS * V_HEAD_DIM), row(C_GATE), row(d)]
                 + [full(w) for w in weights] + [wblk(wgu2), wblk(wd2)],
        out_specs=[row(d), pl.BlockSpec((TOP_K, tm), lambda i: (0, cur(i))),
                   pl.BlockSpec((2 * TOP_K, tm), lambda i: (0, cur(i))),
                   pl.BlockSpec((subs, 1, TOP_K * t_dma), lambda i: (cur(i), 0, 0)),
                   pl.BlockSpec((N_EXPERTS, LANES), lambda i: (0, 0)),
                   wblk(wgu2), wblk(wd2), pl.BlockSpec(memory_space=pl.ANY)],
        out_shape=[jax.ShapeDtypeStruct((n, d), F32), jax.ShapeDtypeStruct((TOP_K, n), jnp.int32),
                   jax.ShapeDtypeStruct((2 * TOP_K, n), F32),
                   jax.ShapeDtypeStruct((n // t_dma, 1, TOP_K * t_dma), jnp.int32),
                   jax.ShapeDtypeStruct((N_EXPERTS, LANES), jnp.int32),
                   jax.ShapeDtypeStruct(wgu2.shape, BF16), jax.ShapeDtypeStruct(wd2.shape, BF16),
                   jax.ShapeDtypeStruct((N_EXPERTS * cap, ROW_CHUNKS, LANES), F32)],
        scratch_shapes=[pltpu.VMEM((HALO + tm, CONV_DIM), F32),
                        pltpu.VMEM((SUBLANES, HALO + tm, CONV_DIM), F32),
                        pltpu.VMEM((tm, CONV_DIM), F32),
                        pltpu.VMEM((2,) + _tile_view_shape(tm), F32),
                        pltpu.VMEM((1, 1, TOP_K * tm), jnp.int32),
                        pltpu.SMEM((1, 1, TOP_K * tm), jnp.int32), pltpu.SMEM((1, 1, TOP_K * tm), jnp.int32),
                        pltpu.VMEM((N_EXPERTS, 1), F32),
                        pltpu.SemaphoreType.DMA((2,)), pltpu.SemaphoreType.DMA],
        compiler_params=_cparams(("arbitrary",)),
        name="mixdisp",
    )(glu, glu, glum, o, g, x2, *weights, wgu2, wd2)
    h, idx, gates, dest, cnt, wgu_b, wd_b, xs = outs
    return h, gates, dest, cnt, wgu_b.reshape(wgu.shape), wd_b.reshape(wd.shape), xs


WAIT_GROUP = 32


def _drain(copy, count):
    def group(_, carry):
        for _ in range(WAIT_GROUP):
            copy.wait()
        return carry

    lax.fori_loop(0, count // WAIT_GROUP, group, 0)


def _plan_body(cnt_ref, bexp_ref, brow_ref, nact_ref, *, bm, n_blocks, cap_blocks):
    def per_expert(e, carry):
        blk, last = carry
        nb = (cnt_ref[e] + (bm - 1)) // bm

        def fill(b, _):
            bexp_ref[b] = e
            brow_ref[b] = e * cap_blocks + (b - blk)
            return 0

        lax.fori_loop(blk, blk + nb, fill, 0)
        return blk + nb, jnp.where(nb > 0, e, last)

    n_act, last = lax.fori_loop(0, N_EXPERTS, per_expert, (jnp.int32(0), jnp.int32(0)))
    nact_ref[0] = n_act

    def tail(b, _):
        bexp_ref[b] = last
        brow_ref[b] = 0
        return 0

    lax.fori_loop(n_act, n_blocks, tail, 0)


def _plan_call(cnt, *, bm, n_blocks, cap_blocks):
    smem = lambda: pl.BlockSpec(memory_space=pltpu.SMEM)
    blocks = jax.ShapeDtypeStruct((n_blocks,), jnp.int32)
    return pl.pallas_call(
        functools.partial(_plan_body, bm=bm, n_blocks=n_blocks, cap_blocks=cap_blocks),
        in_specs=[smem()],
        out_specs=[smem(), smem(), smem()],
        out_shape=[blocks, blocks, jax.ShapeDtypeStruct((1,), jnp.int32)],
        name="plan",
    )(cnt)


def _expert_body(bexp_ref, brow_ref, nact_ref, xs_ref, wgu_ref, bgu_ref, wd_ref, bd_ref, ys_ref,
                 xin_ref, yout_ref, in_sems, out_sems, *, bm):
    b = pl.program_id(0)
    n_act = nact_ref[0]
    slot = b % 2
    other = 1 - slot

    def chunk(c):
        return slice(c * LANES, (c + 1) * LANES)

    def block_rows(blk):
        return pl.ds(pl.multiple_of(brow_ref[blk] * bm, bm), bm)

    def fetch(blk, sl, c):
        return pltpu.make_async_copy(xs_ref.at[block_rows(blk), c], xin_ref.at[sl, :, chunk(c)], in_sems.at[sl])

    def write_back(blk, sl, c):
        return pltpu.make_async_copy(yout_ref.at[sl, :, chunk(c)], ys_ref.at[block_rows(blk), c], out_sems.at[sl])

    @pl.when(b == 0)
    def _prime():
        for c in range(ROW_CHUNKS):
            fetch(0, 0, c).start()

    @pl.when(b < n_act)
    def _block():
        for c in range(ROW_CHUNKS):
            fetch(b, slot, c).wait()

        @pl.when(b + 1 < n_act)
        def _prefetch():
            for c in range(ROW_CHUNKS):
                fetch(b + 1, other, c).start()

        @pl.when(b >= 2)
        def _free_result_slot():
            for c in range(ROW_CHUNKS):
                write_back(b - 2, slot, c).wait()

        x = xin_ref[slot].astype(BF16)
        gu = _dot(x, wgu_ref[0]) + bgu_ref[0]
        g = jnp.minimum(gu[:, :D_FF], SWIGLU_LIMIT)
        u = jnp.clip(gu[:, D_FF:], -SWIGLU_LIMIT, SWIGLU_LIMIT)
        act = (u + 1.0) * (g * _sigmoid(SWIGLU_ALPHA * g))
        yout_ref[slot] = _dot(act.astype(BF16), wd_ref[0]) + bd_ref[0]
        for c in range(ROW_CHUNKS):
            write_back(b, slot, c).start()

    @pl.when(b == n_act)
    def _drain_results():
        for c in range(ROW_CHUNKS):
            write_back(b - 1, other, c).wait()

        @pl.when(b >= 2)
        def _older():
            for c in range(ROW_CHUNKS):
                write_back(b - 2, slot, c).wait()


def _expert_call(bexp, brow, nact, xs, wgu, bgu, wd, bd, *, bm):
    d = ROW_CHUNKS * LANES
    n_blocks = bexp.shape[0]
    wsel = lambda b, be, br, na: (be[jnp.minimum(b, n_blocks - 1)], 0, 0)
    hbm = pl.BlockSpec(memory_space=pl.ANY)
    return pl.pallas_call(
        functools.partial(_expert_body, bm=bm),
        grid_spec=pltpu.PrefetchScalarGridSpec(
            num_scalar_prefetch=3, grid=(n_blocks + 1,),
            in_specs=[hbm, pl.BlockSpec((1, d, 2 * D_FF), wsel), pl.BlockSpec((1, 1, 2 * D_FF), wsel),
                      pl.BlockSpec((1, D_FF, d), wsel), pl.BlockSpec((1, 1, d), wsel)],
            out_specs=hbm,
            scratch_shapes=[pltpu.VMEM((2, bm, d), F32), pltpu.VMEM((2, bm, d), F32),
                            pltpu.SemaphoreType.DMA((2,)), pltpu.SemaphoreType.DMA((2,))]),
        out_shape=jax.ShapeDtypeStruct(xs.shape, F32),
        input_output_aliases={3: 0},
        compiler_params=_cparams(("arbitrary",)),
        name="experts",
    )(bexp, brow, nact, xs, wgu, bgu, wd, bd)


def _combine_body(dest_ref, ys_ref, gate_ref, h_ref, out_ref, buf_ref, gt_ref, sems, *, t, steps):
    i = pl.program_id(0)
    slot = i % 2
    other = 1 - slot

    def row_gather(d, k, grp, s, sl):
        return pltpu.make_async_copy(ys_ref.at[d], _row_of(buf_ref.at[sl, k], grp, s), sems.at[sl])

    @pl.when(i == 0)
    def _first():
        buf_ref[...] = jnp.zeros_like(buf_ref)

    @pl.when(i > 0)
    def _wait_previous_tile():
        _drain(row_gather(0, 0, 0, 0, other), TOP_K * t)

    gt_ref[...] = jnp.transpose(gate_ref[...])

    def body(grp, carry):
        for s in range(SUBLANES):
            for k in range(TOP_K):
                d = dest_ref[0, 0, k * t + grp * SUBLANES + s]
                row_gather(d, k, grp, s, slot).start(priority=k % 2)
        rows = pl.ds(pl.multiple_of(grp * SUBLANES, SUBLANES), SUBLANES)
        acc = h_ref[rows, :]
        g = gt_ref[rows, :]
        for k in range(TOP_K):
            tiles = buf_ref[other, k, grp]
            acc = acc + g[:, k:k + 1] * jnp.concatenate([tiles[c] for c in range(ROW_CHUNKS)], axis=1)
        out_ref[rows, :] = acc
        return carry

    lax.fori_loop(0, t // SUBLANES, body, 0)

    @pl.when(i == steps)
    def _drain_extra_gather():
        _drain(row_gather(0, 0, 0, 0, slot), TOP_K * t)


def _combine_call(dest, ys, gates, h, *, t):
    n, d = h.shape
    steps = n // t
    prev = lambda i: jnp.maximum(i - 1, 0)
    return pl.pallas_call(
        functools.partial(_combine_body, t=t, steps=steps),
        grid=(steps + 1,),
        in_specs=[pl.BlockSpec((1, 1, TOP_K * t), lambda i: (jnp.minimum(i, steps - 1), 0, 0),
                               memory_space=pltpu.SMEM),
                  pl.BlockSpec(memory_space=pl.ANY),
                  pl.BlockSpec((2 * TOP_K, t), lambda i: (0, prev(i))),
                  pl.BlockSpec((t, d), lambda i: (prev(i), 0))],
        out_specs=pl.BlockSpec((t, d), lambda i: (prev(i), 0)),
        out_shape=jax.ShapeDtypeStruct((n, d), F32),
        scratch_shapes=[pltpu.VMEM((2, TOP_K) + _tile_view_shape(t), F32), pltpu.VMEM((t, 2 * TOP_K), F32),
                        pltpu.SemaphoreType.DMA((2,))],
        compiler_params=_cparams(("arbitrary",)),
        name="combine",
    )(dest, ys, gates, h)


def _rot_half(z):
    half = QK_ROPE_DIM // 2
    return jnp.concatenate([-z[:, half:], z[:, :half]], axis=1)


def _pe_slot(z):
    rows = z.shape[0]
    return jnp.concatenate([jnp.zeros((rows, QK_NOPE_DIM), F32), z,
                            jnp.zeros((rows, LANES - QK_HEAD_DIM), F32)], axis=1)


def _prep_proj_weights(attn_norm_w, w_in, q_a_norm_w, w_q_b, kv_a_norm_w, w_kv_b, q_norm_w, k_norm_w, gate_b):
    c0 = 2 * CONV_DIM
    c1 = c0 + Q_LORA_RANK
    c2 = c1 + KV_LORA_RANK
    c3 = c2 + QK_ROPE_DIM
    kpe = w_in[:, c2:c3]
    win = jnp.concatenate([w_in[:, :c2].astype(BF16), _pe_slot(kpe).astype(BF16),
                           _pe_slot(_rot_half(kpe * k_norm_w[None, QK_NOPE_DIM:])).astype(BF16),
                           w_in[:, c3:].astype(BF16)], axis=1)

    pad = jnp.zeros((Q_LORA_RANK, LANES - QK_HEAD_DIM), F32)
    q1, q2 = [], []
    for h in range(N_HEADS):
        cols = w_q_b[:, h * QK_HEAD_DIM:(h + 1) * QK_HEAD_DIM]
        q1.append(jnp.concatenate([cols, pad], axis=1))
        q2.append(_pe_slot(_rot_half(cols[:, QK_NOPE_DIM:] * q_norm_w[None, QK_NOPE_DIM:])))
    wq = jnp.concatenate(q1 + q2, axis=1).astype(BF16)

    kpad = jnp.zeros((KV_LORA_RANK, LANES - QK_NOPE_DIM), F32)
    ks, vs = [], []
    per_head = QK_NOPE_DIM + V_HEAD_DIM
    for h in range(N_HEADS):
        cols = w_kv_b[:, h * per_head:(h + 1) * per_head]
        ks.append(jnp.concatenate([cols[:, :QK_NOPE_DIM], kpad], axis=1))
        vs.append(cols[:, QK_NOPE_DIM:])
    wk = jnp.concatenate(ks, axis=1).astype(BF16)
    wvt = jnp.concatenate(vs, axis=1).T.astype(BF16)

    lane_pad = jnp.zeros((LANES - QK_HEAD_DIM,), F32)
    qlane = jnp.concatenate([q_norm_w, lane_pad])[None, :]
    klane = jnp.concatenate([k_norm_w, lane_pad])[None, :]
    return (attn_norm_w[None, :], win, q_a_norm_w[None, :], wq, kv_a_norm_w[None, :], wk, wvt, qlane, klane,
            gate_b[None, :])


def _rope_tables(length):
    half = QK_ROPE_DIM // 2
    inv_freq = ROPE_THETA ** (-jnp.arange(half, dtype=F32) / half)
    ang = jnp.arange(length, dtype=F32)[:, None] * inv_freq[None, :]
    cos, sin = jnp.cos(ang), jnp.sin(ang)
    ones = jnp.ones((length, QK_NOPE_DIM), F32)
    tail = LANES - QK_HEAD_DIM
    cos_t = jnp.concatenate([ones, cos, cos, jnp.ones((length, tail), F32)], axis=1)
    sin_t = jnp.concatenate([0.0 * ones, sin, sin, jnp.zeros((length, tail), F32)], axis=1)
    return cos_t, sin_t


def _tile(n, pref):
    t = pref
    while n % t:
        t //= 2
    return t


def kernel(x, meta_tokens, attn_norm_w, w_in, conv_dw_w, conv_dw_b, conv_ln_w, conv_ln_b, conv_pw2_w, q_a_norm_w, w_q_b, kv_a_norm_w, w_kv_b, q_norm_w, k_norm_w, w_o_mla, gate_b, w_out, ffn_norm_w, router_w, router_b, w_gate_up, b_gate_up, w_down, b_down):
    assert attn_norm_w.shape[0] == 1, "one layer: rows of meta tokens never feed a later layer"
    b, s, d = x.shape
    n = b * s
    x2 = x.reshape(n, d)

    tm = _tile(s, MIX_TILE)
    tq = _tile(s, 512)
    cos_t, sin_t = _rope_tables(N_META + s)
    pw = _prep_proj_weights(attn_norm_w[0], w_in[0], q_a_norm_w[0], w_q_b[0], kv_a_norm_w[0], w_kv_b[0],
                            q_norm_w[0], k_norm_w[0], gate_b[0])

    tp = _tile(s, 512)
    glu, q, k, vt, g = _proj_call(x2, (cos_t[N_META:], sin_t[N_META:]), pw, tm=tp, tiles_per_seq=s // tp)
    glum, _, km, vmt, _ = _proj_call(meta_tokens.astype(F32), (cos_t[:N_META], sin_t[:N_META]), pw, tm=N_META,
                                     tiles_per_seq=1)

    o = _attn_call(q, k, vt, km, vmt, tq=tq, batch=b)

    rw = router_w[0].T
    rw_hi = rw.astype(BF16)
    rw_lo = (rw - rw_hi.astype(F32)).astype(BF16)
    mix_w = (conv_dw_w[0], conv_dw_b[0][None, :], conv_ln_w[0][None, :], conv_ln_b[0][None, :],
             conv_pw2_w[0].astype(BF16), w_o_mla[0].astype(BF16), w_out[0].astype(BF16), ffn_norm_w[0][None, :],
             rw_hi, rw_lo, router_b[0][:, None])
    t_dma = _tile(tm, 256)
    h, gates, dest, cnt, wgu_b, wd_b, xs = _mixdisp_call(glu, glum, o, g, x2, mix_w, w_gate_up[0], w_down[0], tm=tm,
                                                         tiles_per_seq=s // tm, t_dma=t_dma)

    bm = 512
    n_blocks = (n * TOP_K) // bm + N_EXPERTS
    bexp, brow, nact = _plan_call(cnt[:, 0], bm=bm, n_blocks=n_blocks, cap_blocks=n // bm)
    ys = _expert_call(bexp, brow, nact, xs, wgu_b, b_gate_up[0][:, None, :], wd_b, b_down[0][:, None, :], bm=bm)
    out = _combine_call(dest, ys, gates, h, t=t_dma)
    return out.reshape(b, s, d)
```

```python
import functools
import math

import jax
import jax.numpy as jnp
from jax import lax
from jax.experimental import pallas as pl
from jax.experimental.pallas import tpu as pltpu

N_META = 16
CONV_DIM = 512
CONV_WIDTH = 31
N_HEADS = 8
QK_NOPE_DIM = 64
QK_ROPE_DIM = 32
QK_HEAD_DIM = QK_NOPE_DIM + QK_ROPE_DIM
V_HEAD_DIM = 64
Q_LORA_RANK = 256
KV_LORA_RANK = 128
ROPE_THETA = 10000.0
N_EXPERTS = 32
TOP_K = 4
D_FF = 1024
SWIGLU_LIMIT = 7.0
SWIGLU_ALPHA = 1.702
NORM_EPS = 1e-6
NEG_INF = -1e30

LANES = 128
SUBLANES = 8
BF16_ROWS = 16
ROW_CHUNKS = 8
SCORE_LOOKAHEAD = 2
MIX_TILE = 512
HALO = 32
VMEM_LIMIT = 56 * 1024 * 1024

F32 = jnp.float32
BF16 = jnp.bfloat16


def _cparams(sem):
    return pltpu.CompilerParams(dimension_semantics=sem, vmem_limit_bytes=VMEM_LIMIT)


def _dot(a, b):
    return jnp.dot(a, b, preferred_element_type=F32)


def _dot_t(a, b):
    return lax.dot_general(a, b, (((1,), (1,)), ((), ())), preferred_element_type=F32)


def _sigmoid(x):
    return 1.0 / (1.0 + jnp.exp(-x))


def _rms(x, w):
    return x * lax.rsqrt(jnp.mean(x * x, axis=-1, keepdims=True) + NORM_EPS) * w


C_CONV = 2 * CONV_DIM
C_LAT = Q_LORA_RANK + KV_LORA_RANK + 2 * LANES
C_GATE = 2 * 1024


def _proj_body(x_ref, anw_ref, win_ref, qaw_ref, wq_ref, kvaw_ref, wk_ref, wvt_ref, qlane_ref, klane_ref,
               cos_ref, sin_ref, gb_ref, glu_ref, q_ref, k_ref, vt_ref, g_ref, *, q_scale):
    x = x_ref[...]
    xn = _rms(x, anw_ref[...]).astype(BF16)

    lat = _dot(xn, win_ref[:, C_CONV:C_CONV + C_LAT])
    q_lat = lat[:, :Q_LORA_RANK]
    c_kv = lat[:, Q_LORA_RANK:Q_LORA_RANK + KV_LORA_RANK]
    kpe = lat[:, Q_LORA_RANK + KV_LORA_RANK:Q_LORA_RANK + KV_LORA_RANK + LANES]
    kpe_rot = lat[:, Q_LORA_RANK + KV_LORA_RANK + LANES:]

    qn = _rms(q_lat, qaw_ref[...]).astype(BF16)
    cn = _rms(c_kv, kvaw_ref[...]).astype(BF16)
    qq = _dot(qn, wq_ref[...])
    kv = _dot(cn, wk_ref[...])
    vt_ref[...] = _dot_t(wvt_ref[...], cn).astype(vt_ref.dtype)
    u = _dot(xn, win_ref[:, 0:C_CONV])

    cos = cos_ref[...]
    sin = sin_ref[...]
    q_cos = qlane_ref[...] * cos
    for h in range(N_HEADS):
        q1 = qq[:, h * LANES:(h + 1) * LANES]
        q2 = qq[:, (N_HEADS + h) * LANES:(N_HEADS + h + 1) * LANES]
        ss = jnp.sum(q1 * q1, axis=-1, keepdims=True)
        s = lax.rsqrt(ss * (1.0 / QK_HEAD_DIM) + NORM_EPS) * q_scale
        q_ref[:, h * LANES:(h + 1) * LANES] = (s * (q1 * q_cos + q2 * sin)).astype(q_ref.dtype)

    gl = _dot(xn, win_ref[:, C_CONV + C_LAT:]) + gb_ref[...]

    k_cos = klane_ref[...] * cos
    ss_pe = jnp.sum(kpe * kpe, axis=-1, keepdims=True)
    k_pe_roped = kpe * k_cos + kpe_rot * sin
    for h in range(N_HEADS):
        kn = kv[:, h * LANES:(h + 1) * LANES]
        ss = jnp.sum(kn * kn, axis=-1, keepdims=True) + ss_pe
        s = lax.rsqrt(ss * (1.0 / QK_HEAD_DIM) + NORM_EPS)
        k_ref[:, h * LANES:(h + 1) * LANES] = (s * (kn * k_cos + k_pe_roped)).astype(k_ref.dtype)

    glu_ref[...] = u[:, :CONV_DIM] * _sigmoid(u[:, CONV_DIM:])
    g_ref[...] = _sigmoid(gl).astype(g_ref.dtype)


def _proj_call(x2, tables, weights, *, tm, tiles_per_seq):
    n = x2.shape[0]
    d = x2.shape[1]
    cos_t, sin_t = tables
    (anw, win, qaw, wq, kvaw, wk, wvt, qlane, klane, gb) = weights
    full = lambda a: pl.BlockSpec(a.shape, lambda i: (0,) * a.ndim)
    row = lambda c: pl.BlockSpec((tm, c), lambda i: (i, 0))
    pos = pl.BlockSpec((tm, LANES), lambda i: (i % tiles_per_seq, 0))
    q_scale = (QK_HEAD_DIM ** -0.5) * math.log2(math.e)
    return pl.pallas_call(
        functools.partial(_proj_body, q_scale=q_scale),
        grid=(n // tm,),
        in_specs=[row(d), full(anw), full(win), full(qaw), full(wq), full(kvaw), full(wk), full(wvt),
                  full(qlane), full(klane), pos, pos, full(gb)],
        out_specs=[row(CONV_DIM), row(N_HEADS * LANES), row(N_HEADS * LANES),
                   pl.BlockSpec((N_HEADS * V_HEAD_DIM, tm), lambda i: (0, i)), row(C_GATE)],
        out_shape=[jax.ShapeDtypeStruct((n, CONV_DIM), F32),
                   jax.ShapeDtypeStruct((n, N_HEADS * LANES), BF16),
                   jax.ShapeDtypeStruct((n, N_HEADS * LANES), BF16),
                   jax.ShapeDtypeStruct((N_HEADS * V_HEAD_DIM, n), BF16),
                   jax.ShapeDtypeStruct((n, C_GATE), BF16)],
        compiler_params=_cparams(("parallel",)),
        name="proj",
    )(x2, anw, win, qaw, wq, kvaw, wk, wvt, qlane, klane, cos_t, sin_t, gb)


def _attn_body(q_ref, k_ref, vt_ref, km_ref, vmt_ref, o_ref, m_ref, l_ref, acc_ref):
    i = pl.program_id(1)
    tq = q_ref.shape[0]

    def head_slices(h):
        return slice(h * LANES, (h + 1) * LANES), slice(h * V_HEAD_DIM, (h + 1) * V_HEAD_DIM)

    def _meta():
        ms, ls = [], []
        sts = [_dot_t(km_ref[:, head_slices(h)[0]], q_ref[:, head_slices(h)[0]]) for h in range(N_HEADS)]
        for h in range(N_HEADS):
            ks, vs = head_slices(h)
            st = sts[h]
            m = jnp.max(st, axis=0, keepdims=True)
            p = jnp.exp2(st - m)
            ms.append(m)
            ls.append(jnp.sum(p, axis=0, keepdims=True))
            acc_ref[h] = _dot(vmt_ref[vs, :], p.astype(BF16))
        m_ref[...] = jnp.concatenate(ms, axis=0)
        l_ref[...] = jnp.concatenate(ls, axis=0)

    def step(j, diagonal):
        keys = pl.ds(pl.multiple_of(j * tq, tq), tq)

        def scores(h):
            ks, _ = head_slices(h)
            return _dot_t(k_ref[keys, ks], q_ref[:, ks])

        m_all = m_ref[...]
        l_all = l_ref[...]
        ms, ls = [], []
        pending = [scores(h) for h in range(SCORE_LOOKAHEAD)]
        for h in range(N_HEADS):
            _, vs = head_slices(h)
            st = pending.pop(0)
            if h + SCORE_LOOKAHEAD < N_HEADS:
                pending.append(scores(h + SCORE_LOOKAHEAD))
            if diagonal:
                key = lax.broadcasted_iota(jnp.int32, st.shape, 0)
                qry = lax.broadcasted_iota(jnp.int32, st.shape, 1)
                st = jnp.where(key <= qry, st, NEG_INF)
            m_prev = m_all[h:h + 1, :]
            m_new = jnp.maximum(m_prev, jnp.max(st, axis=0, keepdims=True))
            alpha = jnp.exp2(m_prev - m_new)
            p = jnp.exp2(st - m_new)
            l_new = alpha * l_all[h:h + 1, :] + jnp.sum(p, axis=0, keepdims=True)
            acc = alpha * acc_ref[h] + _dot(vt_ref[vs, keys], p.astype(BF16))
            if diagonal:
                acc_ref[h] = acc / l_new
            else:
                acc_ref[h] = acc
                ms.append(m_new)
                ls.append(l_new)
        if not diagonal:
            m_ref[...] = jnp.concatenate(ms, axis=0)
            l_ref[...] = jnp.concatenate(ls, axis=0)

    def full_step(j, carry):
        step(j, False)
        return carry

    _meta()
    lax.fori_loop(0, i, full_step, 0)
    step(i, True)
    ot = acc_ref[...].reshape(N_HEADS * V_HEAD_DIM, tq)
    o_ref[...] = jnp.transpose(ot).astype(o_ref.dtype)


def _attn_call(q, k, vt, km, vmt, *, tq, batch):
    n = q.shape[0]
    s = n // batch
    nq = s // tq
    const = lambda bi, i: (0, 0)
    return pl.pallas_call(
        _attn_body,
        grid=(batch, nq),
        in_specs=[pl.BlockSpec((tq, N_HEADS * LANES), lambda bi, i: (bi * nq + i, 0)),
                  pl.BlockSpec((s, N_HEADS * LANES), lambda bi, i: (bi, 0)),
                  pl.BlockSpec((N_HEADS * V_HEAD_DIM, s), lambda bi, i: (0, bi)),
                  pl.BlockSpec(km.shape, const), pl.BlockSpec(vmt.shape, const)],
        out_specs=pl.BlockSpec((tq, N_HEADS * V_HEAD_DIM), lambda bi, i: (bi * nq + i, 0)),
        out_shape=jax.ShapeDtypeStruct((n, N_HEADS * V_HEAD_DIM), BF16),
        scratch_shapes=[pltpu.VMEM((N_HEADS, tq), F32), pltpu.VMEM((N_HEADS, tq), F32),
                        pltpu.VMEM((N_HEADS, V_HEAD_DIM, tq), F32)],
        compiler_params=_cparams(("parallel", "arbitrary")),
        name="attn",
    )(q, k, vt, km, vmt)


def _tile_view_shape(rows):
    return (rows // SUBLANES, ROW_CHUNKS, SUBLANES, LANES)


def _to_row_tiles(ref, val):
    groups = val.shape[0] // SUBLANES
    for c in range(ROW_CHUNKS):
        ref[:, c] = val[:, c * LANES:(c + 1) * LANES].reshape(groups, SUBLANES, LANES)


def _from_row_tiles(ref):
    rows = ref.shape[0] * SUBLANES
    return jnp.concatenate([ref[:, c].reshape(rows, LANES) for c in range(ROW_CHUNKS)], axis=1)


def _row_of(ref, row_group, sublane):
    return ref.at[row_group, :, sublane, :]


def _route(hn, whi_ref, wlo_ref, rb_ref, idx_ref, gate_ref):
    x_hi = hn.astype(BF16)
    x_lo = (hn - x_hi.astype(F32)).astype(BF16)
    w_hi = whi_ref[...]
    logits = _dot_t(w_hi, x_hi) + _dot_t(w_hi, x_lo) + _dot_t(wlo_ref[...], x_hi) + rb_ref[...]

    e_iota = lax.broadcasted_iota(jnp.int32, logits.shape, 0).astype(F32)
    vals, idxs = [], []
    cur = logits
    for _ in range(TOP_K):
        m = jnp.max(cur, axis=0, keepdims=True)
        idx = jnp.min(jnp.where(cur == m, e_iota, float(N_EXPERTS)), axis=0, keepdims=True)
        vals.append(m)
        idxs.append(idx)
        cur = jnp.where(e_iota == idx, -jnp.inf, cur)
    exps = [jnp.exp(v - vals[0]) for v in vals]
    denom = exps[0] + exps[1] + exps[2] + exps[3]
    idx_ref[...] = jnp.concatenate(idxs, axis=0).astype(jnp.int32)
    gate_ref[...] = jnp.concatenate([e / denom for e in exps] + [jnp.zeros_like(denom)] * 4, axis=0)
    return idxs, e_iota


CONV_ROWS = 2 * SUBLANES


def _conv_prepare(glu_ref, halo_ref, glum_ref, xpad_ref, shift_ref, *, tm, tiles_per_seq):
    first = (pl.program_id(0) % tiles_per_seq) == 0
    meta_ctx = jnp.concatenate([jnp.zeros((HALO - N_META, CONV_DIM), F32), glum_ref[...]], axis=0)
    xpad_ref[0:HALO, :] = jnp.where(first, meta_ctx, halo_ref[...])
    xpad_ref[HALO:HALO + tm, :] = glu_ref[...]
    span = tm + HALO - SUBLANES
    for rho in range(1, SUBLANES):
        shift_ref[rho, 0:span, :] = xpad_ref[rho:rho + span, :]


def _conv_rows(dww_ref, dwb_ref, xpad_ref, shift_ref, conv_ref, r0):
    off = HALO - (CONV_WIDTH - 1)
    acc = jnp.zeros((CONV_ROWS, CONV_DIM), F32) + dwb_ref[...]
    for t in range(CONV_WIDTH):
        rho, base = (off + t) % SUBLANES, (off + t) // SUBLANES * SUBLANES
        rows = pl.ds(pl.multiple_of(base + r0, SUBLANES), CONV_ROWS)
        win = xpad_ref[rows, :] if rho == 0 else shift_ref[rho, rows, :]
        acc = acc + dww_ref[t:t + 1, :] * win
    conv_ref[pl.ds(pl.multiple_of(r0, CONV_ROWS), CONV_ROWS), :] = acc


def _mix_tail(conv_ref, o_ref, g_ref, x_ref, lnw_ref, lnb_ref, pw2_ref, wo_ref, wout_ref, fnw_ref, whi_ref, wlo_ref,
              rb_ref, h_ref, hn_ref, idx_ref, gate_ref, *, before):
    y_mla = _dot(o_ref[...], wo_ref[...])
    acc = conv_ref[...]
    mu = jnp.mean(acc, axis=-1, keepdims=True)
    xc = acc - mu
    y = xc * lax.rsqrt(jnp.mean(xc * xc, axis=-1, keepdims=True) + NORM_EPS) * lnw_ref[...] + lnb_ref[...]
    y = y * _sigmoid(y)
    y_conv = _dot(y.astype(BF16), pw2_ref[...])
    g = g_ref[...].astype(F32)
    mixed = g[:, :1024] * y_conv + g[:, 1024:] * y_mla
    h = x_ref[...] + _dot(mixed.astype(BF16), wout_ref[...])
    h_ref[...] = h
    hn = _rms(h, fnw_ref[...])
    before()
    _to_row_tiles(hn_ref, hn)
    return _route(hn, whi_ref, wlo_ref, rb_ref, idx_ref, gate_ref)


def _mixdisp_body(glu_ref, halo_ref, glum_ref, o_ref, g_ref, x_ref, dww_ref, dwb_ref, lnw_ref, lnb_ref,
                  pw2_ref, wo_ref, wout_ref, fnw_ref, whi_ref, wlo_ref, rb_ref, wgu_ref, wd_ref,
                  h_ref, idx_ref, gate_ref, dest_ref, cnt_ref, wgu_out, wd_out, xs_ref,
                  xpad_ref, shift_ref, conv_ref, hn_slots, dvec_ref, dsm0_ref, dsm1_ref, carry_ref, sems, dsem,
                  *, tm, tiles_per_seq, n_tiles, cap, t_dma):
    i = pl.program_id(0)
    slot = i % 2
    other = 1 - slot
    groups = tm // SUBLANES

    dsm_refs = (dsm0_ref, dsm1_ref)

    def send(sl, grp, s, k):
        d = dsm_refs[sl][0, 0, k * tm + grp * SUBLANES + s]
        return pltpu.make_async_copy(_row_of(hn_slots.at[sl], grp, s), xs_ref.at[d], sems.at[sl])

    def sent(sl):
        return pltpu.make_async_copy(_row_of(hn_slots.at[0], 0, 0), xs_ref.at[0], sems.at[sl])

    @pl.when(i == 0)
    def _init():
        carry_ref[...] = jnp.zeros_like(carry_ref)

    @pl.when(i >= 1)
    def _positions_arrived():
        pltpu.make_async_copy(dvec_ref, dsm_refs[0], dsem).wait()

    @pl.when(i < n_tiles)
    def _prepare():
        _conv_prepare(glu_ref, halo_ref, glum_ref, xpad_ref, shift_ref, tm=tm, tiles_per_seq=tiles_per_seq)

    iters = tm // CONV_ROWS

    def sweep(sl, sends, conv):
        cast_rows = wgu_ref.shape[0] // iters

        def body(it, c):
            if sends:
                for gg in range(CONV_ROWS // SUBLANES):
                    grp = it * (CONV_ROWS // SUBLANES) + gg
                    for s in range(SUBLANES):
                        for k in range(TOP_K):
                            send(sl, grp, s, k).start(priority=k % 2)
                rows = pl.ds(pl.multiple_of(it * cast_rows, cast_rows), cast_rows)
                wgu_out[rows, :] = wgu_ref[rows, :].astype(BF16)
                wd_out[rows, :] = wd_ref[rows, :].astype(BF16)
            if conv:
                _conv_rows(dww_ref, dwb_ref, xpad_ref, shift_ref, conv_ref, it * CONV_ROWS)
            return c

        lax.fori_loop(0, iters, body, 0)

    pl.when(i == 0)(functools.partial(sweep, 0, False, True))
    for sl in range(2):
        pl.when((i >= 1) & (i < n_tiles) & (other == sl))(functools.partial(sweep, sl, True, True))
        pl.when((i == n_tiles) & (other == sl))(functools.partial(sweep, sl, True, False))

    def _wait_slot():
        @pl.when(i >= 2)
        def _():
            _drain(sent(slot), TOP_K * tm)

    @pl.when(i < n_tiles)
    def _tile():
        idxs, e_iota = _mix_tail(conv_ref, o_ref, g_ref, x_ref, lnw_ref, lnb_ref, pw2_ref, wo_ref, wout_ref, fnw_ref,
                                 whi_ref, wlo_ref, rb_ref, h_ref, hn_slots.at[slot], idx_ref, gate_ref,
                                 before=_wait_slot)
        hits = [e_iota == idxs[k] for k in range(TOP_K)]
        onehot = jnp.zeros(e_iota.shape, F32)
        for hk in hits:
            onehot = onehot + jnp.where(hk, 1.0, 0.0)
        r = lax.broadcasted_iota(jnp.int32, (tm, tm), 0)
        c = lax.broadcasted_iota(jnp.int32, (tm, tm), 1)
        before = jnp.where(r < c, 1.0, 0.0).astype(BF16)
        val = _dot(onehot.astype(BF16), before) + carry_ref[...]
        dests = [(jnp.sum(jnp.where(hk, val, 0.0), axis=0, keepdims=True) + idxs[k] * float(cap)).astype(jnp.int32)
                 for k, hk in enumerate(hits)]
        total = carry_ref[...] + jnp.sum(onehot, axis=1, keepdims=True)
        carry_ref[...] = total
        cnt_ref[...] = jnp.broadcast_to(total, cnt_ref.shape).astype(jnp.int32)
        for sub in range(tm // t_dma):
            dest_ref[sub] = jnp.concatenate([dk[:, sub * t_dma:(sub + 1) * t_dma] for dk in dests], axis=1)
        dvec_ref[0] = jnp.concatenate(dests, axis=1)
        for sl in range(2):
            @pl.when(slot == sl)
            def _to_smem():
                pltpu.make_async_copy(dvec_ref, dsm_refs[sl], dsem).start()

    @pl.when(i == n_tiles)
    def _drain_all():
        _drain(sent(other), TOP_K * tm)

        @pl.when(i >= 2)
        def _():
            _drain(sent(slot), TOP_K * tm)


def _mixdisp_call(glu, glum, o, g, x2, weights, wgu, wd, *, tm, tiles_per_seq, t_dma):
    n, d = x2.shape
    n_tiles = n // tm
    cap = n
    wgu2 = wgu.reshape(-1, wgu.shape[-1])
    wd2 = wd.reshape(-1, wd.shape[-1])
    w_rows = wgu2.shape[0] // n_tiles
    assert wgu2.shape[0] == wd2.shape[0] == w_rows * n_tiles
    assert (w_rows // (tm // SUBLANES // 2)) % BF16_ROWS == 0
    cur = lambda i: jnp.minimum(i, n_tiles - 1)
    prev = lambda i: jnp.maximum(i - 1, 0)
    full = lambda a: pl.BlockSpec(a.shape, lambda i: (0,) * a.ndim)
    row = lambda c: pl.BlockSpec((tm, c), lambda i: (cur(i), 0))
    halo = pl.BlockSpec((HALO, CONV_DIM), lambda i: (jnp.maximum(cur(i) * (tm // HALO) - 1, 0), 0))
    wblk = lambda a: pl.BlockSpec((w_rows, a.shape[1]), lambda i: (prev(i), 0))
    subs = tm // t_dma
    outs = pl.pallas_call(
        functools.partial(_mixdisp_body, tm=tm, tiles_per_seq=tiles_per_seq, n_tiles=n_tiles, cap=cap, t_dma=t_dma),
        grid=(n_tiles + 1,),
        in_specs=[row(CONV_DIM), halo, full(glum), row(N_HEADS * V_HEAD_DIM), row(C_GATE), row(d)]
                 + [full(w) for w in weights] + [wblk(wgu2), wblk(wd2)],
        out_specs=[row(d), pl.BlockSpec((TOP_K, tm), lambda i: (0, cur(i))),
                   pl.BlockSpec((2 * TOP_K, tm), lambda i: (0, cur(i))),
                   pl.BlockSpec((subs, 1, TOP_K * t_dma), lambda i: (cur(i), 0, 0)),
                   pl.BlockSpec((N_EXPERTS, LANES), lambda i: (0, 0)),
                   wblk(wgu2), wblk(wd2), pl.BlockSpec(memory_space=pl.ANY)],
        out_shape=[jax.ShapeDtypeStruct((n, d), F32), jax.ShapeDtypeStruct((TOP_K, n), jnp.int32),
                   jax.ShapeDtypeStruct((2 * TOP_K, n), F32),
                   jax.ShapeDtypeStruct((n // t_dma, 1, TOP_K * t_dma), jnp.int32),
                   jax.ShapeDtypeStruct((N_EXPERTS, LANES), jnp.int32),
                   jax.ShapeDtypeStruct(wgu2.shape, BF16), jax.ShapeDtypeStruct(wd2.shape, BF16),
                   jax.ShapeDtypeStruct((N_EXPERTS * cap, ROW_CHUNKS, LANES), F32)],
        scratch_shapes=[pltpu.VMEM((HALO + tm, CONV_DIM), F32),
                        pltpu.VMEM((SUBLANES, HALO + tm, CONV_DIM), F32),
                        pltpu.VMEM((tm, CONV_DIM), F32),
                        pltpu.VMEM((2,) + _tile_view_shape(tm), F32),
                        pltpu.VMEM((1, 1, TOP_K * tm), jnp.int32),
                        pltpu.SMEM((1, 1, TOP_K * tm), jnp.int32), pltpu.SMEM((1, 1, TOP_K * tm), jnp.int32),
                        pltpu.VMEM((N_EXPERTS, 1), F32),
                        pltpu.SemaphoreType.DMA((2,)), pltpu.SemaphoreType.DMA],
        compiler_params=_cparams(("arbitrary",)),
        name="mixdisp",
    )(glu, glu, glum, o, g, x2, *weights, wgu2, wd2)
    h, idx, gates, dest, cnt, wgu_b, wd_b, xs = outs
    return h, gates, dest, cnt, wgu_b.reshape(wgu.shape), wd_b.reshape(wd.shape), xs


WAIT_GROUP = 256


def _drain(copy, count):
    def group(_, carry):
        for _ in range(WAIT_GROUP):
            copy.wait()
        return carry

    lax.fori_loop(0, count // WAIT_GROUP, group, 0)


def _plan_body(cnt_ref, bexp_ref, brow_ref, nact_ref, *, bm, n_blocks, cap_blocks):
    def per_expert(e, carry):
        blk, last = carry
        nb = (cnt_ref[e] + (bm - 1)) // bm

        def fill(b, _):
            bexp_ref[b] = e
            brow_ref[b] = e * cap_blocks + (b - blk)
            return 0

        lax.fori_loop(blk, blk + nb, fill, 0)
        return blk + nb, jnp.where(nb > 0, e, last)

    n_act, last = lax.fori_loop(0, N_EXPERTS, per_expert, (jnp.int32(0), jnp.int32(0)))
    nact_ref[0] = n_act

    def tail(b, _):
        bexp_ref[b] = last
        brow_ref[b] = 0
        return 0

    lax.fori_loop(n_act, n_blocks, tail, 0)


def _plan_call(cnt, *, bm, n_blocks, cap_blocks):
    smem = lambda: pl.BlockSpec(memory_space=pltpu.SMEM)
    blocks = jax.ShapeDtypeStruct((n_blocks,), jnp.int32)
    return pl.pallas_call(
        functools.partial(_plan_body, bm=bm, n_blocks=n_blocks, cap_blocks=cap_blocks),
        in_specs=[smem()],
        out_specs=[smem(), smem(), smem()],
        out_shape=[blocks, blocks, jax.ShapeDtypeStruct((1,), jnp.int32)],
        name="plan",
    )(cnt)


def _expert_body(bexp_ref, brow_ref, nact_ref, xs_ref, wgu_ref, bgu_ref, wd_ref, bd_ref, ys_ref,
                 xin_ref, yout_ref, in_sems, out_sems, *, bm):
    b = pl.program_id(0)
    n_act = nact_ref[0]
    slot = b % 2
    other = 1 - slot

    def chunk(c):
        return slice(c * LANES, (c + 1) * LANES)

    def block_rows(blk):
        return pl.ds(pl.multiple_of(brow_ref[blk] * bm, bm), bm)

    def fetch(blk, sl, c):
        return pltpu.make_async_copy(xs_ref.at[block_rows(blk), c], xin_ref.at[sl, :, chunk(c)], in_sems.at[sl])

    def write_back(blk, sl, c):
        return pltpu.make_async_copy(yout_ref.at[sl, :, chunk(c)], ys_ref.at[block_rows(blk), c], out_sems.at[sl])

    @pl.when(b == 0)
    def _prime():
        for c in range(ROW_CHUNKS):
            fetch(0, 0, c).start()

    @pl.when(b < n_act)
    def _block():
        for c in range(ROW_CHUNKS):
            fetch(b, slot, c).wait()

        @pl.when(b + 1 < n_act)
        def _prefetch():
            for c in range(ROW_CHUNKS):
                fetch(b + 1, other, c).start()

        @pl.when(b >= 2)
        def _free_result_slot():
            for c in range(ROW_CHUNKS):
                write_back(b - 2, slot, c).wait()

        x = xin_ref[slot].astype(BF16)
        gu = _dot(x, wgu_ref[0]) + bgu_ref[0]
        g = jnp.minimum(gu[:, :D_FF], SWIGLU_LIMIT)
        u = jnp.clip(gu[:, D_FF:], -SWIGLU_LIMIT, SWIGLU_LIMIT)
        act = (u + 1.0) * (g * _sigmoid(SWIGLU_ALPHA * g))
        yout_ref[slot] = _dot(act.astype(BF16), wd_ref[0]) + bd_ref[0]
        for c in range(ROW_CHUNKS):
            write_back(b, slot, c).start()

    @pl.when(b == n_act)
    def _drain_results():
        for c in range(ROW_CHUNKS):
            write_back(b - 1, other, c).wait()

        @pl.when(b >= 2)
        def _older():
            for c in range(ROW_CHUNKS):
                write_back(b - 2, slot, c).wait()


def _expert_call(bexp, brow, nact, xs, wgu, bgu, wd, bd, *, bm):
    d = ROW_CHUNKS * LANES
    n_blocks = bexp.shape[0]
    wsel = lambda b, be, br, na: (be[jnp.minimum(b, n_blocks - 1)], 0, 0)
    hbm = pl.BlockSpec(memory_space=pl.ANY)
    return pl.pallas_call(
        functools.partial(_expert_body, bm=bm),
        grid_spec=pltpu.PrefetchScalarGridSpec(
            num_scalar_prefetch=3, grid=(n_blocks + 1,),
            in_specs=[hbm, pl.BlockSpec((1, d, 2 * D_FF), wsel), pl.BlockSpec((1, 1, 2 * D_FF), wsel),
                      pl.BlockSpec((1, D_FF, d), wsel), pl.BlockSpec((1, 1, d), wsel)],
            out_specs=hbm,
            scratch_shapes=[pltpu.VMEM((2, bm, d), F32), pltpu.VMEM((2, bm, d), F32),
                            pltpu.SemaphoreType.DMA((2,)), pltpu.SemaphoreType.DMA((2,))]),
        out_shape=jax.ShapeDtypeStruct(xs.shape, F32),
        input_output_aliases={3: 0},
        compiler_params=_cparams(("arbitrary",)),
        name="experts",
    )(bexp, brow, nact, xs, wgu, bgu, wd, bd)


def _combine_body(dest_ref, ys_ref, gate_ref, h_ref, out_ref, buf_ref, gt_ref, sems, *, t, steps):
    i = pl.program_id(0)
    slot = i % 2
    other = 1 - slot

    def row_gather(d, k, grp, s, sl):
        return pltpu.make_async_copy(ys_ref.at[d], _row_of(buf_ref.at[sl, k], grp, s), sems.at[sl])

    @pl.when(i == 0)
    def _first():
        buf_ref[...] = jnp.zeros_like(buf_ref)

    @pl.when(i > 0)
    def _wait_previous_tile():
        _drain(row_gather(0, 0, 0, 0, other), TOP_K * t)

    gt_ref[...] = jnp.transpose(gate_ref[...])

    def body(grp, carry):
        for s in range(SUBLANES):
            for k in range(TOP_K):
                d = dest_ref[0, 0, k * t + grp * SUBLANES + s]
                row_gather(d, k, grp, s, slot).start(priority=k % 2)
        rows = pl.ds(pl.multiple_of(grp * SUBLANES, SUBLANES), SUBLANES)
        acc = h_ref[rows, :]
        g = gt_ref[rows, :]
        for k in range(TOP_K):
            tiles = buf_ref[other, k, grp]
            acc = acc + g[:, k:k + 1] * jnp.concatenate([tiles[c] for c in range(ROW_CHUNKS)], axis=1)
        out_ref[rows, :] = acc
        return carry

    lax.fori_loop(0, t // SUBLANES, body, 0)

    @pl.when(i == steps)
    def _drain_extra_gather():
        _drain(row_gather(0, 0, 0, 0, slot), TOP_K * t)


def _combine_call(dest, ys, gates, h, *, t):
    n, d = h.shape
    steps = n // t
    prev = lambda i: jnp.maximum(i - 1, 0)
    return pl.pallas_call(
        functools.partial(_combine_body, t=t, steps=steps),
        grid=(steps + 1,),
        in_specs=[pl.BlockSpec((1, 1, TOP_K * t), lambda i: (jnp.minimum(i, steps - 1), 0, 0),
                               memory_space=pltpu.SMEM),
                  pl.BlockSpec(memory_space=pl.ANY),
                  pl.BlockSpec((2 * TOP_K, t), lambda i: (0, prev(i))),
                  pl.BlockSpec((t, d), lambda i: (prev(i), 0))],
        out_specs=pl.BlockSpec((t, d), lambda i: (prev(i), 0)),
        out_shape=jax.ShapeDtypeStruct((n, d), F32),
        scratch_shapes=[pltpu.VMEM((2, TOP_K) + _tile_view_shape(t), F32), pltpu.VMEM((t, 2 * TOP_K), F32),
                        pltpu.SemaphoreType.DMA((2,))],
        compiler_params=_cparams(("arbitrary",)),
        name="combine",
    )(dest, ys, gates, h)


def _rot_half(z):
    half = QK_ROPE_DIM // 2
    return jnp.concatenate([-z[:, half:], z[:, :half]], axis=1)


def _pe_slot(z):
    rows = z.shape[0]
    return jnp.concatenate([jnp.zeros((rows, QK_NOPE_DIM), F32), z,
                            jnp.zeros((rows, LANES - QK_HEAD_DIM), F32)], axis=1)


def _prep_proj_weights(attn_norm_w, w_in, q_a_norm_w, w_q_b, kv_a_norm_w, w_kv_b, q_norm_w, k_norm_w, gate_b):
    c0 = 2 * CONV_DIM
    c1 = c0 + Q_LORA_RANK
    c2 = c1 + KV_LORA_RANK
    c3 = c2 + QK_ROPE_DIM
    kpe = w_in[:, c2:c3]
    win = jnp.concatenate([w_in[:, :c2].astype(BF16), _pe_slot(kpe).astype(BF16),
                           _pe_slot(_rot_half(kpe * k_norm_w[None, QK_NOPE_DIM:])).astype(BF16),
                           w_in[:, c3:].astype(BF16)], axis=1)

    pad = jnp.zeros((Q_LORA_RANK, LANES - QK_HEAD_DIM), F32)
    q1, q2 = [], []
    for h in range(N_HEADS):
        cols = w_q_b[:, h * QK_HEAD_DIM:(h + 1) * QK_HEAD_DIM]
        q1.append(jnp.concatenate([cols, pad], axis=1))
        q2.append(_pe_slot(_rot_half(cols[:, QK_NOPE_DIM:] * q_norm_w[None, QK_NOPE_DIM:])))
    wq = jnp.concatenate(q1 + q2, axis=1).astype(BF16)

    kpad = jnp.zeros((KV_LORA_RANK, LANES - QK_NOPE_DIM), F32)
    ks, vs = [], []
    per_head = QK_NOPE_DIM + V_HEAD_DIM
    for h in range(N_HEADS):
        cols = w_kv_b[:, h * per_head:(h + 1) * per_head]
        ks.append(jnp.concatenate([cols[:, :QK_NOPE_DIM], kpad], axis=1))
        vs.append(cols[:, QK_NOPE_DIM:])
    wk = jnp.concatenate(ks, axis=1).astype(BF16)
    wvt = jnp.concatenate(vs, axis=1).T.astype(BF16)

    lane_pad = jnp.zeros((LANES - QK_HEAD_DIM,), F32)
    qlane = jnp.concatenate([q_norm_w, lane_pad])[None, :]
    klane = jnp.concatenate([k_norm_w, lane_pad])[None, :]
    return (attn_norm_w[None, :], win, q_a_norm_w[None, :], wq, kv_a_norm_w[None, :], wk, wvt, qlane, klane,
            gate_b[None, :])


def _rope_tables(length):
    half = QK_ROPE_DIM // 2
    inv_freq = ROPE_THETA ** (-jnp.arange(half, dtype=F32) / half)
    ang = jnp.arange(length, dtype=F32)[:, None] * inv_freq[None, :]
    cos, sin = jnp.cos(ang), jnp.sin(ang)
    ones = jnp.ones((length, QK_NOPE_DIM), F32)
    tail = LANES - QK_HEAD_DIM
    cos_t = jnp.concatenate([ones, cos, cos, jnp.ones((length, tail), F32)], axis=1)
    sin_t = jnp.concatenate([0.0 * ones, sin, sin, jnp.zeros((length, tail), F32)], axis=1)
    return cos_t, sin_t


def _tile(n, pref):
    t = pref
    while n % t:
        t //= 2
    return t


def kernel(x, meta_tokens, attn_norm_w, w_in, conv_dw_w, conv_dw_b, conv_ln_w, conv_ln_b, conv_pw2_w, q_a_norm_w, w_q_b, kv_a_norm_w, w_kv_b, q_norm_w, k_norm_w, w_o_mla, gate_b, w_out, ffn_norm_w, router_w, router_b, w_gate_up, b_gate_up, w_down, b_down):
    assert attn_norm_w.shape[0] == 1, "one layer: rows of meta tokens never feed a later layer"
    b, s, d = x.shape
    n = b * s
    x2 = x.reshape(n, d)

    tm = _tile(s, MIX_TILE)
    tq = _tile(s, 512)
    cos_t, sin_t = _rope_tables(N_META + s)
    pw = _prep_proj_weights(attn_norm_w[0], w_in[0], q_a_norm_w[0], w_q_b[0], kv_a_norm_w[0], w_kv_b[0],
                            q_norm_w[0], k_norm_w[0], gate_b[0])

    tp = _tile(s, 512)
    glu, q, k, vt, g = _proj_call(x2, (cos_t[N_META:], sin_t[N_META:]), pw, tm=tp, tiles_per_seq=s // tp)
    glum, _, km, vmt, _ = _proj_call(meta_tokens.astype(F32), (cos_t[:N_META], sin_t[:N_META]), pw, tm=N_META,
                                     tiles_per_seq=1)

    o = _attn_call(q, k, vt, km, vmt, tq=tq, batch=b)

    rw = router_w[0].T
    rw_hi = rw.astype(BF16)
    rw_lo = (rw - rw_hi.astype(F32)).astype(BF16)
    mix_w = (conv_dw_w[0], conv_dw_b[0][None, :], conv_ln_w[0][None, :], conv_ln_b[0][None, :],
             conv_pw2_w[0].astype(BF16), w_o_mla[0].astype(BF16), w_out[0].astype(BF16), ffn_norm_w[0][None, :],
             rw_hi, rw_lo, router_b[0][:, None])
    t_dma = _tile(tm, 256)
    h, gates, dest, cnt, wgu_b, wd_b, xs = _mixdisp_call(glu, glum, o, g, x2, mix_w, w_gate_up[0], w_down[0], tm=tm,
                                                         tiles_per_seq=s // tm, t_dma=t_dma)

    bm = 512
    n_blocks = (n * TOP_K) // bm + N_EXPERTS
    bexp, brow, nact = _plan_call(cnt[:, 0], bm=bm, n_blocks=n_blocks, cap_blocks=n // bm)
    ys = _expert_call(bexp, brow, nact, xs, wgu_b, b_gate_up[0][:, None, :], wd_b, b_down[0][:, None, :], bm=bm)
    out = _combine_call(dest, ys, gates, h, t=t_dma)
    return out.reshape(b, s, d)
```

```python
import functools
import math

import jax
import jax.numpy as jnp
from jax import lax
from jax.experimental import pallas as pl
from jax.experimental.pallas import tpu as pltpu

N_META = 16
CONV_DIM = 512
CONV_WIDTH = 31
N_HEADS = 8
QK_NOPE_DIM = 64
QK_ROPE_DIM = 32
QK_HEAD_DIM = QK_NOPE_DIM + QK_ROPE_DIM
V_HEAD_DIM = 64
Q_LORA_RANK = 256
KV_LORA_RANK = 128
ROPE_THETA = 10000.0
N_EXPERTS = 32
TOP_K = 4
D_FF = 1024
SWIGLU_LIMIT = 7.0
SWIGLU_ALPHA = 1.702
NORM_EPS = 1e-6
NEG_INF = -1e30

LANES = 128
SUBLANES = 8
BF16_ROWS = 16
ROW_CHUNKS = 8
SCORE_LOOKAHEAD = 2
MIX_TILE = 512
HALO = 32
VMEM_LIMIT = 56 * 1024 * 1024

F32 = jnp.float32
BF16 = jnp.bfloat16


def _cparams(sem):
    return pltpu.CompilerParams(dimension_semantics=sem, vmem_limit_bytes=VMEM_LIMIT)


def _dot(a, b):
    return jnp.dot(a, b, preferred_element_type=F32)


def _dot_t(a, b):
    return lax.dot_general(a, b, (((1,), (1,)), ((), ())), preferred_element_type=F32)


def _sigmoid(x):
    return 1.0 / (1.0 + jnp.exp(-x))


def _rms(x, w):
    return x * lax.rsqrt(jnp.mean(x * x, axis=-1, keepdims=True) + NORM_EPS) * w


C_CONV = 2 * CONV_DIM
C_LAT = Q_LORA_RANK + KV_LORA_RANK + 2 * LANES
C_GATE = 2 * 1024


def _proj_body(x_ref, anw_ref, win_ref, qaw_ref, wq_ref, kvaw_ref, wk_ref, wvt_ref, qlane_ref, klane_ref,
               cos_ref, sin_ref, gb_ref, glu_ref, q_ref, k_ref, vt_ref, g_ref, *, q_scale):
    x = x_ref[...]
    xn = _rms(x, anw_ref[...]).astype(BF16)

    lat = _dot(xn, win_ref[:, C_CONV:C_CONV + C_LAT])
    q_lat = lat[:, :Q_LORA_RANK]
    c_kv = lat[:, Q_LORA_RANK:Q_LORA_RANK + KV_LORA_RANK]
    kpe = lat[:, Q_LORA_RANK + KV_LORA_RANK:Q_LORA_RANK + KV_LORA_RANK + LANES]
    kpe_rot = lat[:, Q_LORA_RANK + KV_LORA_RANK + LANES:]

    qn = _rms(q_lat, qaw_ref[...]).astype(BF16)
    cn = _rms(c_kv, kvaw_ref[...]).astype(BF16)
    qq = _dot(qn, wq_ref[...])
    kv = _dot(cn, wk_ref[...])
    vt_ref[...] = _dot_t(wvt_ref[...], cn).astype(vt_ref.dtype)
    u = _dot(xn, win_ref[:, 0:C_CONV])

    cos = cos_ref[...]
    sin = sin_ref[...]
    q_cos = qlane_ref[...] * cos
    for h in range(N_HEADS):
        q1 = qq[:, h * LANES:(h + 1) * LANES]
        q2 = qq[:, (N_HEADS + h) * LANES:(N_HEADS + h + 1) * LANES]
        ss = jnp.sum(q1 * q1, axis=-1, keepdims=True)
        s = lax.rsqrt(ss * (1.0 / QK_HEAD_DIM) + NORM_EPS) * q_scale
        q_ref[:, h * LANES:(h + 1) * LANES] = (s * (q1 * q_cos + q2 * sin)).astype(q_ref.dtype)

    gl = _dot(xn, win_ref[:, C_CONV + C_LAT:]) + gb_ref[...]

    k_cos = klane_ref[...] * cos
    ss_pe = jnp.sum(kpe * kpe, axis=-1, keepdims=True)
    k_pe_roped = kpe * k_cos + kpe_rot * sin
    for h in range(N_HEADS):
        kn = kv[:, h * LANES:(h + 1) * LANES]
        ss = jnp.sum(kn * kn, axis=-1, keepdims=True) + ss_pe
        s = lax.rsqrt(ss * (1.0 / QK_HEAD_DIM) + NORM_EPS)
        k_ref[:, h * LANES:(h + 1) * LANES] = (s * (kn * k_cos + k_pe_roped)).astype(k_ref.dtype)

    glu_ref[...] = u[:, :CONV_DIM] * _sigmoid(u[:, CONV_DIM:])
    g_ref[...] = _sigmoid(gl).astype(g_ref.dtype)


def _proj_call(x2, tables, weights, *, tm, tiles_per_seq):
    n = x2.shape[0]
    d = x2.shape[1]
    cos_t, sin_t = tables
    (anw, win, qaw, wq, kvaw, wk, wvt, qlane, klane, gb) = weights
    full = lambda a: pl.BlockSpec(a.shape, lambda i: (0,) * a.ndim)
    row = lambda c: pl.BlockSpec((tm, c), lambda i: (i, 0))
    pos = pl.BlockSpec((tm, LANES), lambda i: (i % tiles_per_seq, 0))
    q_scale = (QK_HEAD_DIM ** -0.5) * math.log2(math.e)
    return pl.pallas_call(
        functools.partial(_proj_body, q_scale=q_scale),
        grid=(n // tm,),
        in_specs=[row(d), full(anw), full(win), full(qaw), full(wq), full(kvaw), full(wk), full(wvt),
                  full(qlane), full(klane), pos, pos, full(gb)],
        out_specs=[row(CONV_DIM), row(N_HEADS * LANES), row(N_HEADS * LANES),
                   pl.BlockSpec((N_HEADS * V_HEAD_DIM, tm), lambda i: (0, i)), row(C_GATE)],
        out_shape=[jax.ShapeDtypeStruct((n, CONV_DIM), F32),
                   jax.ShapeDtypeStruct((n, N_HEADS * LANES), BF16),
                   jax.ShapeDtypeStruct((n, N_HEADS * LANES), BF16),
                   jax.ShapeDtypeStruct((N_HEADS * V_HEAD_DIM, n), BF16),
                   jax.ShapeDtypeStruct((n, C_GATE), BF16)],
        compiler_params=_cparams(("parallel",)),
        name="proj",
    )(x2, anw, win, qaw, wq, kvaw, wk, wvt, qlane, klane, cos_t, sin_t, gb)


def _attn_body(q_ref, k_ref, vt_ref, km_ref, vmt_ref, o_ref, m_ref, l_ref, acc_ref):
    i = pl.program_id(1)
    tq = q_ref.shape[0]

    def head_slices(h):
        return slice(h * LANES, (h + 1) * LANES), slice(h * V_HEAD_DIM, (h + 1) * V_HEAD_DIM)

    def _meta():
        ms, ls = [], []
        sts = [_dot_t(km_ref[:, head_slices(h)[0]], q_ref[:, head_slices(h)[0]]) for h in range(N_HEADS)]
        for h in range(N_HEADS):
            ks, vs = head_slices(h)
            st = sts[h]
            m = jnp.max(st, axis=0, keepdims=True)
            p = jnp.exp2(st - m)
            ms.append(m)
            ls.append(jnp.sum(p, axis=0, keepdims=True))
            acc_ref[h] = _dot(vmt_ref[vs, :], p.astype(BF16))
        m_ref[...] = jnp.concatenate(ms, axis=0)
        l_ref[...] = jnp.concatenate(ls, axis=0)

    def step(j, diagonal):
        keys = pl.ds(pl.multiple_of(j * tq, tq), tq)

        def scores(h):
            ks, _ = head_slices(h)
            return _dot_t(k_ref[keys, ks], q_ref[:, ks])

        m_all = m_ref[...]
        l_all = l_ref[...]
        ms, ls = [], []
        pending = [scores(h) for h in range(SCORE_LOOKAHEAD)]
        for h in range(N_HEADS):
            _, vs = head_slices(h)
            st = pending.pop(0)
            if h + SCORE_LOOKAHEAD < N_HEADS:
                pending.append(scores(h + SCORE_LOOKAHEAD))
            if diagonal:
                key = lax.broadcasted_iota(jnp.int32, st.shape, 0)
                qry = lax.broadcasted_iota(jnp.int32, st.shape, 1)
                st = jnp.where(key <= qry, st, NEG_INF)
            m_prev = m_all[h:h + 1, :]
            m_new = jnp.maximum(m_prev, jnp.max(st, axis=0, keepdims=True))
            alpha = jnp.exp2(m_prev - m_new)
            p = jnp.exp2(st - m_new)
            l_new = alpha * l_all[h:h + 1, :] + jnp.sum(p, axis=0, keepdims=True)
            acc = alpha * acc_ref[h] + _dot(vt_ref[vs, keys], p.astype(BF16))
            if diagonal:
                acc_ref[h] = acc / l_new
            else:
                acc_ref[h] = acc
                ms.append(m_new)
                ls.append(l_new)
        if not diagonal:
            m_ref[...] = jnp.concatenate(ms, axis=0)
            l_ref[...] = jnp.concatenate(ls, axis=0)

    def full_step(j, carry):
        step(j, False)
        return carry

    _meta()
    lax.fori_loop(0, i, full_step, 0)
    step(i, True)
    ot = acc_ref[...].reshape(N_HEADS * V_HEAD_DIM, tq)
    o_ref[...] = jnp.transpose(ot).astype(o_ref.dtype)


def _attn_call(q, k, vt, km, vmt, *, tq, batch):
    n = q.shape[0]
    s = n // batch
    nq = s // tq
    const = lambda bi, i: (0, 0)
    return pl.pallas_call(
        _attn_body,
        grid=(batch, nq),
        in_specs=[pl.BlockSpec((tq, N_HEADS * LANES), lambda bi, i: (bi * nq + i, 0)),
                  pl.BlockSpec((s, N_HEADS * LANES), lambda bi, i: (bi, 0)),
                  pl.BlockSpec((N_HEADS * V_HEAD_DIM, s), lambda bi, i: (0, bi)),
                  pl.BlockSpec(km.shape, const), pl.BlockSpec(vmt.shape, const)],
        out_specs=pl.BlockSpec((tq, N_HEADS * V_HEAD_DIM), lambda bi, i: (bi * nq + i, 0)),
        out_shape=jax.ShapeDtypeStruct((n, N_HEADS * V_HEAD_DIM), BF16),
        scratch_shapes=[pltpu.VMEM((N_HEADS, tq), F32), pltpu.VMEM((N_HEADS, tq), F32),
                        pltpu.VMEM((N_HEADS, V_HEAD_DIM, tq), F32)],
        compiler_params=_cparams(("parallel", "arbitrary")),
        name="attn",
    )(q, k, vt, km, vmt)


def _tile_view_shape(rows):
    return (rows // SUBLANES, ROW_CHUNKS, SUBLANES, LANES)


def _to_row_tiles(ref, val):
    groups = val.shape[0] // SUBLANES
    for c in range(ROW_CHUNKS):
        ref[:, c] = val[:, c * LANES:(c + 1) * LANES].reshape(groups, SUBLANES, LANES)


def _row_of(ref, row_group, sublane):
    return ref.at[row_group, :, sublane, :]


def _route(hn, whi_ref, wlo_ref, rb_ref, idx_ref, gate_ref):
    x_hi = hn.astype(BF16)
    x_lo = (hn - x_hi.astype(F32)).astype(BF16)
    w_hi = whi_ref[...]
    logits = _dot_t(w_hi, x_hi) + _dot_t(w_hi, x_lo) + _dot_t(wlo_ref[...], x_hi) + rb_ref[...]

    e_iota = lax.broadcasted_iota(jnp.int32, logits.shape, 0).astype(F32)
    vals, idxs = [], []
    cur = logits
    for _ in range(TOP_K):
        m = jnp.max(cur, axis=0, keepdims=True)
        idx = jnp.min(jnp.where(cur == m, e_iota, float(N_EXPERTS)), axis=0, keepdims=True)
        vals.append(m)
        idxs.append(idx)
        cur = jnp.where(e_iota == idx, -jnp.inf, cur)
    exps = [jnp.exp(v - vals[0]) for v in vals]
    denom = exps[0] + exps[1] + exps[2] + exps[3]
    idx_ref[...] = jnp.concatenate(idxs, axis=0).astype(jnp.int32)
    gate_ref[...] = jnp.concatenate([e / denom for e in exps] + [jnp.zeros_like(denom)] * 4, axis=0)
    return idxs, e_iota


CONV_ROWS = 2 * SUBLANES


def _conv_prepare(glu_ref, halo_ref, glum_ref, xpad_ref, shift_ref, *, tm, tiles_per_seq):
    first = (pl.program_id(0) % tiles_per_seq) == 0
    meta_ctx = jnp.concatenate([jnp.zeros((HALO - N_META, CONV_DIM), F32), glum_ref[...]], axis=0)
    xpad_ref[0:HALO, :] = jnp.where(first, meta_ctx, halo_ref[...])
    xpad_ref[HALO:HALO + tm, :] = glu_ref[...]
    span = tm + HALO - SUBLANES
    for rho in range(1, SUBLANES):
        shift_ref[rho, 0:span, :] = xpad_ref[rho:rho + span, :]


def _conv_rows(dww_ref, dwb_ref, xpad_ref, shift_ref, conv_ref, r0):
    off = HALO - (CONV_WIDTH - 1)
    acc = jnp.zeros((CONV_ROWS, CONV_DIM), F32) + dwb_ref[...]
    for t in range(CONV_WIDTH):
        rho, base = (off + t) % SUBLANES, (off + t) // SUBLANES * SUBLANES
        rows = pl.ds(pl.multiple_of(base + r0, SUBLANES), CONV_ROWS)
        win = xpad_ref[rows, :] if rho == 0 else shift_ref[rho, rows, :]
        acc = acc + dww_ref[t:t + 1, :] * win
    conv_ref[pl.ds(pl.multiple_of(r0, CONV_ROWS), CONV_ROWS), :] = acc


def _mix_tail(conv_ref, o_ref, g_ref, x_ref, lnw_ref, lnb_ref, pw2_ref, wo_ref, wout_ref, fnw_ref, whi_ref, wlo_ref,
              rb_ref, h_ref, hn_ref, idx_ref, gate_ref, *, before):
    y_mla = _dot(o_ref[...], wo_ref[...])
    acc = conv_ref[...]
    mu = jnp.mean(acc, axis=-1, keepdims=True)
    xc = acc - mu
    y = xc * lax.rsqrt(jnp.mean(xc * xc, axis=-1, keepdims=True) + NORM_EPS) * lnw_ref[...] + lnb_ref[...]
    y = y * _sigmoid(y)
    y_conv = _dot(y.astype(BF16), pw2_ref[...])
    g = g_ref[...].astype(F32)
    mixed = g[:, :1024] * y_conv + g[:, 1024:] * y_mla
    h = x_ref[...] + _dot(mixed.astype(BF16), wout_ref[...])
    h_ref[...] = h
    hn = _rms(h, fnw_ref[...])
    before()
    _to_row_tiles(hn_ref, hn)
    return _route(hn, whi_ref, wlo_ref, rb_ref, idx_ref, gate_ref)


def _mixdisp_body(glu_ref, halo_ref, glum_ref, o_ref, g_ref, x_ref, dww_ref, dwb_ref, lnw_ref, lnb_ref,
                  pw2_ref, wo_ref, wout_ref, fnw_ref, whi_ref, wlo_ref, rb_ref, wgu_ref, wd_ref,
                  h_ref, idx_ref, gate_ref, dest_ref, cnt_ref, wgu_out, wd_out, xs_ref,
                  xpad_ref, shift_ref, conv_ref, hn_slots, dvec_ref, dsm0_ref, dsm1_ref, carry_ref, sems, dsem,
                  *, tm, tiles_per_seq, n_tiles, cap, t_dma):
    i = pl.program_id(0)
    slot = i % 2
    other = 1 - slot

    dsm_refs = (dsm0_ref, dsm1_ref)

    def send(sl, grp, s, k):
        d = dsm_refs[sl][0, 0, k * tm + grp * SUBLANES + s]
        return pltpu.make_async_copy(_row_of(hn_slots.at[sl], grp, s), xs_ref.at[d], sems.at[sl])

    def sent(sl):
        return pltpu.make_async_copy(_row_of(hn_slots.at[0], 0, 0), xs_ref.at[0], sems.at[sl])

    @pl.when(i == 0)
    def _init():
        carry_ref[...] = jnp.zeros_like(carry_ref)

    @pl.when(i >= 1)
    def _positions_arrived():
        pltpu.make_async_copy(dvec_ref, dsm_refs[0], dsem).wait()

    @pl.when(i < n_tiles)
    def _prepare():
        _conv_prepare(glu_ref, halo_ref, glum_ref, xpad_ref, shift_ref, tm=tm, tiles_per_seq=tiles_per_seq)

    iters = tm // CONV_ROWS

    def sweep(sl, sends, conv):
        cast_rows = wgu_ref.shape[0] // iters

        def body(it, c):
            if sends:
                for gg in range(CONV_ROWS // SUBLANES):
                    grp = it * (CONV_ROWS // SUBLANES) + gg
                    for s in range(SUBLANES):
                        for k in range(TOP_K):
                            send(sl, grp, s, k).start(priority=k % 2)
                rows = pl.ds(pl.multiple_of(it * cast_rows, cast_rows), cast_rows)
                wgu_out[rows, :] = wgu_ref[rows, :].astype(BF16)
                wd_out[rows, :] = wd_ref[rows, :].astype(BF16)
            if conv:
                _conv_rows(dww_ref, dwb_ref, xpad_ref, shift_ref, conv_ref, it * CONV_ROWS)
            return c

        lax.fori_loop(0, iters, body, 0)

    pl.when(i == 0)(functools.partial(sweep, 0, False, True))
    for sl in range(2):
        pl.when((i >= 1) & (i < n_tiles) & (other == sl))(functools.partial(sweep, sl, True, True))
        pl.when((i == n_tiles) & (other == sl))(functools.partial(sweep, sl, True, False))

    def _wait_slot():
        @pl.when(i >= 2)
        def _():
            _drain(sent(slot), TOP_K * tm)

    @pl.when(i < n_tiles)
    def _tile():
        idxs, e_iota = _mix_tail(conv_ref, o_ref, g_ref, x_ref, lnw_ref, lnb_ref, pw2_ref, wo_ref, wout_ref, fnw_ref,
                                 whi_ref, wlo_ref, rb_ref, h_ref, hn_slots.at[slot], idx_ref, gate_ref,
                                 before=_wait_slot)
        hits = [e_iota == idxs[k] for k in range(TOP_K)]
        onehot = jnp.zeros(e_iota.shape, F32)
        for hk in hits:
            onehot = onehot + jnp.where(hk, 1.0, 0.0)
        r = lax.broadcasted_iota(jnp.int32, (tm, tm), 0)
        c = lax.broadcasted_iota(jnp.int32, (tm, tm), 1)
        before = jnp.where(r < c, 1.0, 0.0).astype(BF16)
        val = _dot(onehot.astype(BF16), before) + carry_ref[...]
        dests = [(jnp.sum(jnp.where(hk, val, 0.0), axis=0, keepdims=True) + idxs[k] * float(cap)).astype(jnp.int32)
                 for k, hk in enumerate(hits)]
        total = carry_ref[...] + jnp.sum(onehot, axis=1, keepdims=True)
        carry_ref[...] = total
        cnt_ref[...] = jnp.broadcast_to(total, cnt_ref.shape).astype(jnp.int32)
        for sub in range(tm // t_dma):
            dest_ref[sub] = jnp.concatenate([dk[:, sub * t_dma:(sub + 1) * t_dma] for dk in dests], axis=1)
        dvec_ref[0] = jnp.concatenate(dests, axis=1)
        for sl in range(2):
            @pl.when(slot == sl)
            def _to_smem():
                pltpu.make_async_copy(dvec_ref, dsm_refs[sl], dsem).start()

    @pl.when(i == n_tiles)
    def _drain_all():
        _drain(sent(other), TOP_K * tm)

        @pl.when(i >= 2)
        def _():
            _drain(sent(slot), TOP_K * tm)


def _mixdisp_call(glu, glum, o, g, x2, weights, wgu, wd, *, tm, tiles_per_seq, t_dma):
    n, d = x2.shape
    n_tiles = n // tm
    cap = n
    wgu2 = wgu.reshape(-1, wgu.shape[-1])
    wd2 = wd.reshape(-1, wd.shape[-1])
    w_rows = wgu2.shape[0] // n_tiles
    assert wgu2.shape[0] == wd2.shape[0] == w_rows * n_tiles
    assert (w_rows // (tm // SUBLANES // 2)) % BF16_ROWS == 0
    cur = lambda i: jnp.minimum(i, n_tiles - 1)
    prev = lambda i: jnp.maximum(i - 1, 0)
    full = lambda a: pl.BlockSpec(a.shape, lambda i: (0,) * a.ndim)
    row = lambda c: pl.BlockSpec((tm, c), lambda i: (cur(i), 0))
    halo = pl.BlockSpec((HALO, CONV_DIM), lambda i: (jnp.maximum(cur(i) * (tm // HALO) - 1, 0), 0))
    wblk = lambda a: pl.BlockSpec((w_rows, a.shape[1]), lambda i: (prev(i), 0))
    subs = tm // t_dma
    outs = pl.pallas_call(
        functools.partial(_mixdisp_body, tm=tm, tiles_per_seq=tiles_per_seq, n_tiles=n_tiles, cap=cap, t_dma=t_dma),
        grid=(n_tiles + 1,),
        in_specs=[row(CONV_DIM), halo, full(glum), row(N_HEADS * V_HEAD_DIM), row(C_GATE), row(d)]
                 + [full(w) for w in weights] + [wblk(wgu2), wblk(wd2)],
        out_specs=[row(d), pl.BlockSpec((TOP_K, tm), lambda i: (0, cur(i))),
                   pl.BlockSpec((2 * TOP_K, tm), lambda i: (0, cur(i))),
                   pl.BlockSpec((subs, 1, TOP_K * t_dma), lambda i: (cur(i), 0, 0)),
                   pl.BlockSpec((N_EXPERTS, LANES), lambda i: (0, 0)),
                   wblk(wgu2), wblk(wd2), pl.BlockSpec(memory_space=pl.ANY)],
        out_shape=[jax.ShapeDtypeStruct((n, d), F32), jax.ShapeDtypeStruct((TOP_K, n), jnp.int32),
                   jax.ShapeDtypeStruct((2 * TOP_K, n), F32),
                   jax.ShapeDtypeStruct((n // t_dma, 1, TOP_K * t_dma), jnp.int32),
                   jax.ShapeDtypeStruct((N_EXPERTS, LANES), jnp.int32),
                   jax.ShapeDtypeStruct(wgu2.shape, BF16), jax.ShapeDtypeStruct(wd2.shape, BF16),
                   jax.ShapeDtypeStruct((N_EXPERTS * cap, ROW_CHUNKS, LANES), F32)],
        scratch_shapes=[pltpu.VMEM((HALO + tm, CONV_DIM), F32),
                        pltpu.VMEM((SUBLANES, HALO + tm, CONV_DIM), F32),
                        pltpu.VMEM((tm, CONV_DIM), F32),
                        pltpu.VMEM((2,) + _tile_view_shape(tm), F32),
                        pltpu.VMEM((1, 1, TOP_K * tm), jnp.int32),
                        pltpu.SMEM((1, 1, TOP_K * tm), jnp.int32), pltpu.SMEM((1, 1, TOP_K * tm), jnp.int32),
                        pltpu.VMEM((N_EXPERTS, 1), F32),
                        pltpu.SemaphoreType.DMA((2,)), pltpu.SemaphoreType.DMA],
        compiler_params=_cparams(("arbitrary",)),
        name="mixdisp",
    )(glu, glu, glum, o, g, x2, *weights, wgu2, wd2)
    h, idx, gates, dest, cnt, wgu_b, wd_b, xs = outs
    return h, gates, dest, cnt, wgu_b.reshape(wgu.shape), wd_b.reshape(wd.shape), xs


WAIT_GROUP = 256


def _drain(copy, count):
    def group(_, carry):
        for _ in range(WAIT_GROUP):
            copy.wait()
        return carry

    lax.fori_loop(0, count // WAIT_GROUP, group, 0)


def _plan_body(cnt_ref, bexp_ref, brow_ref, nact_ref, *, bm, n_blocks, cap_blocks):
    def per_expert(e, carry):
        blk, last = carry
        nb = (cnt_ref[e] + (bm - 1)) // bm

        def fill(b, _):
            bexp_ref[b] = e
            brow_ref[b] = e * cap_blocks + (b - blk)
            return 0

        lax.fori_loop(blk, blk + nb, fill, 0)
        return blk + nb, jnp.where(nb > 0, e, last)

    n_act, last = lax.fori_loop(0, N_EXPERTS, per_expert, (jnp.int32(0), jnp.int32(0)))
    nact_ref[0] = n_act

    def tail(b, _):
        bexp_ref[b] = last
        brow_ref[b] = 0
        return 0

    lax.fori_loop(n_act, n_blocks, tail, 0)


def _plan_call(cnt, *, bm, n_blocks, cap_blocks):
    smem = lambda: pl.BlockSpec(memory_space=pltpu.SMEM)
    blocks = jax.ShapeDtypeStruct((n_blocks,), jnp.int32)
    return pl.pallas_call(
        functools.partial(_plan_body, bm=bm, n_blocks=n_blocks, cap_blocks=cap_blocks),
        in_specs=[smem()],
        out_specs=[smem(), smem(), smem()],
        out_shape=[blocks, blocks, jax.ShapeDtypeStruct((1,), jnp.int32)],
        name="plan",
    )(cnt)


def _expert_body(bexp_ref, brow_ref, nact_ref, xs_ref, wgu_ref, bgu_ref, wd_ref, bd_ref, ys_ref,
                 xin_ref, yout_ref, in_sems, out_sems, *, bm):
    b = pl.program_id(0)
    n_act = nact_ref[0]
    slot = b % 2
    other = 1 - slot

    def chunk(c):
        return slice(c * LANES, (c + 1) * LANES)

    def block_rows(blk):
        return pl.ds(pl.multiple_of(brow_ref[blk] * bm, bm), bm)

    def fetch(blk, sl, c):
        return pltpu.make_async_copy(xs_ref.at[block_rows(blk), c], xin_ref.at[sl, :, chunk(c)], in_sems.at[sl])

    def write_back(blk, sl, c):
        return pltpu.make_async_copy(yout_ref.at[sl, :, chunk(c)], ys_ref.at[block_rows(blk), c], out_sems.at[sl])

    @pl.when(b == 0)
    def _prime():
        for c in range(ROW_CHUNKS):
            fetch(0, 0, c).start()

    @pl.when(b < n_act)
    def _block():
        for c in range(ROW_CHUNKS):
            fetch(b, slot, c).wait()

        @pl.when(b + 1 < n_act)
        def _prefetch():
            for c in range(ROW_CHUNKS):
                fetch(b + 1, other, c).start()

        @pl.when(b >= 2)
        def _free_result_slot():
            for c in range(ROW_CHUNKS):
                write_back(b - 2, slot, c).wait()

        x = xin_ref[slot].astype(BF16)
        gu = _dot(x, wgu_ref[0]) + bgu_ref[0]
        g = jnp.minimum(gu[:, :D_FF], SWIGLU_LIMIT)
        u = jnp.clip(gu[:, D_FF:], -SWIGLU_LIMIT, SWIGLU_LIMIT)
        act = (u + 1.0) * (g * _sigmoid(SWIGLU_ALPHA * g))
        yout_ref[slot] = _dot(act.astype(BF16), wd_ref[0]) + bd_ref[0]
        for c in range(ROW_CHUNKS):
            write_back(b, slot, c).start()

    @pl.when(b == n_act)
    def _drain_results():
        for c in range(ROW_CHUNKS):
            write_back(b - 1, other, c).wait()

        @pl.when(b >= 2)
        def _older():
            for c in range(ROW_CHUNKS):
                write_back(b - 2, slot, c).wait()


def _expert_call(bexp, brow, nact, xs, wgu, bgu, wd, bd, *, bm):
    d = ROW_CHUNKS * LANES
    n_blocks = bexp.shape[0]
    wsel = lambda b, be, br, na: (be[jnp.minimum(b, n_blocks - 1)], 0, 0)
    hbm = pl.BlockSpec(memory_space=pl.ANY)
    return pl.pallas_call(
        functools.partial(_expert_body, bm=bm),
        grid_spec=pltpu.PrefetchScalarGridSpec(
            num_scalar_prefetch=3, grid=(n_blocks + 1,),
            in_specs=[hbm, pl.BlockSpec((1, d, 2 * D_FF), wsel), pl.BlockSpec((1, 1, 2 * D_FF), wsel),
                      pl.BlockSpec((1, D_FF, d), wsel), pl.BlockSpec((1, 1, d), wsel)],
            out_specs=hbm,
            scratch_shapes=[pltpu.VMEM((2, bm, d), F32), pltpu.VMEM((2, bm, d), F32),
                            pltpu.SemaphoreType.DMA((2,)), pltpu.SemaphoreType.DMA((2,))]),
        out_shape=jax.ShapeDtypeStruct(xs.shape, F32),
        input_output_aliases={3: 0},
        compiler_params=_cparams(("arbitrary",)),
        name="experts",
    )(bexp, brow, nact, xs, wgu, bgu, wd, bd)


def _combine_body(dest_ref, ys_ref, gate_ref, h_ref, out_ref, buf_ref, gt_ref, sems, *, t, steps):
    i = pl.program_id(0)
    slot = i % 2
    other = 1 - slot

    def row_gather(d, k, grp, s, sl):
        return pltpu.make_async_copy(ys_ref.at[d], _row_of(buf_ref.at[sl, k], grp, s), sems.at[sl])

    @pl.when(i == 0)
    def _first():
        buf_ref[...] = jnp.zeros_like(buf_ref)

    @pl.when(i > 0)
    def _wait_previous_tile():
        _drain(row_gather(0, 0, 0, 0, other), TOP_K * t)

    gt_ref[...] = jnp.transpose(gate_ref[...])

    def body(grp, carry):
        for s in range(SUBLANES):
            for k in range(TOP_K):
                d = dest_ref[0, 0, k * t + grp * SUBLANES + s]
                row_gather(d, k, grp, s, slot).start(priority=k % 2)
        rows = pl.ds(pl.multiple_of(grp * SUBLANES, SUBLANES), SUBLANES)
        acc = h_ref[rows, :]
        g = gt_ref[rows, :]
        for k in range(TOP_K):
            tiles = buf_ref[other, k, grp]
            acc = acc + g[:, k:k + 1] * jnp.concatenate([tiles[c] for c in range(ROW_CHUNKS)], axis=1)
        out_ref[rows, :] = acc
        return carry

    lax.fori_loop(0, t // SUBLANES, body, 0)

    @pl.when(i == steps)
    def _drain_extra_gather():
        _drain(row_gather(0, 0, 0, 0, slot), TOP_K * t)


def _combine_call(dest, ys, gates, h, *, t):
    n, d = h.shape
    steps = n // t
    prev = lambda i: jnp.maximum(i - 1, 0)
    return pl.pallas_call(
        functools.partial(_combine_body, t=t, steps=steps),
        grid=(steps + 1,),
        in_specs=[pl.BlockSpec((1, 1, TOP_K * t), lambda i: (jnp.minimum(i, steps - 1), 0, 0),
                               memory_space=pltpu.SMEM),
                  pl.BlockSpec(memory_space=pl.ANY),
                  pl.BlockSpec((2 * TOP_K, t), lambda i: (0, prev(i))),
                  pl.BlockSpec((t, d), lambda i: (prev(i), 0))],
        out_specs=pl.BlockSpec((t, d), lambda i: (prev(i), 0)),
        out_shape=jax.ShapeDtypeStruct((n, d), F32),
        scratch_shapes=[pltpu.VMEM((2, TOP_K) + _tile_view_shape(t), F32), pltpu.VMEM((t, 2 * TOP_K), F32),
                        pltpu.SemaphoreType.DMA((2,))],
        compiler_params=_cparams(("arbitrary",)),
        name="combine",
    )(dest, ys, gates, h)


def _rot_half(z):
    half = QK_ROPE_DIM // 2
    return jnp.concatenate([-z[:, half:], z[:, :half]], axis=1)


def _pe_slot(z):
    rows = z.shape[0]
    return jnp.concatenate([jnp.zeros((rows, QK_NOPE_DIM), F32), z,
                            jnp.zeros((rows, LANES - QK_HEAD_DIM), F32)], axis=1)


def _prep_proj_weights(attn_norm_w, w_in, q_a_norm_w, w_q_b, kv_a_norm_w, w_kv_b, q_norm_w, k_norm_w, gate_b):
    c0 = 2 * CONV_DIM
    c1 = c0 + Q_LORA_RANK
    c2 = c1 + KV_LORA_RANK
    c3 = c2 + QK_ROPE_DIM
    kpe = w_in[:, c2:c3]
    win = jnp.concatenate([w_in[:, :c2].astype(BF16), _pe_slot(kpe).astype(BF16),
                           _pe_slot(_rot_half(kpe * k_norm_w[None, QK_NOPE_DIM:])).astype(BF16),
                           w_in[:, c3:].astype(BF16)], axis=1)

    pad = jnp.zeros((Q_LORA_RANK, LANES - QK_HEAD_DIM), F32)
    q1, q2 = [], []
    for h in range(N_HEADS):
        cols = w_q_b[:, h * QK_HEAD_DIM:(h + 1) * QK_HEAD_DIM]
        q1.append(jnp.concatenate([cols, pad], axis=1))
        q2.append(_pe_slot(_rot_half(cols[:, QK_NOPE_DIM:] * q_norm_w[None, QK_NOPE_DIM:])))
    wq = jnp.concatenate(q1 + q2, axis=1).astype(BF16)

    kpad = jnp.zeros((KV_LORA_RANK, LANES - QK_NOPE_DIM), F32)
    ks, vs = [], []
    per_head = QK_NOPE_DIM + V_HEAD_DIM
    for h in range(N_HEADS):
        cols = w_kv_b[:, h * per_head:(h + 1) * per_head]
        ks.append(jnp.concatenate([cols[:, :QK_NOPE_DIM], kpad], axis=1))
        vs.append(cols[:, QK_NOPE_DIM:])
    wk = jnp.concatenate(ks, axis=1).astype(BF16)
    wvt = jnp.concatenate(vs, axis=1).T.astype(BF16)

    lane_pad = jnp.zeros((LANES - QK_HEAD_DIM,), F32)
    qlane = jnp.concatenate([q_norm_w, lane_pad])[None, :]
    klane = jnp.concatenate([k_norm_w, lane_pad])[None, :]
    return (attn_norm_w[None, :], win, q_a_norm_w[None, :], wq, kv_a_norm_w[None, :], wk, wvt, qlane, klane,
            gate_b[None, :])


def _rope_tables(length):
    half = QK_ROPE_DIM // 2
    inv_freq = ROPE_THETA ** (-jnp.arange(half, dtype=F32) / half)
    ang = jnp.arange(length, dtype=F32)[:, None] * inv_freq[None, :]
    cos, sin = jnp.cos(ang), jnp.sin(ang)
    ones = jnp.ones((length, QK_NOPE_DIM), F32)
    tail = LANES - QK_HEAD_DIM
    cos_t = jnp.concatenate([ones, cos, cos, jnp.ones((length, tail), F32)], axis=1)
    sin_t = jnp.concatenate([0.0 * ones, sin, sin, jnp.zeros((length, tail), F32)], axis=1)
    return cos_t, sin_t


def _tile(n, pref):
    t = pref
    while n % t:
        t //= 2
    return t


def kernel(x, meta_tokens, attn_norm_w, w_in, conv_dw_w, conv_dw_b, conv_ln_w, conv_ln_b, conv_pw2_w, q_a_norm_w, w_q_b, kv_a_norm_w, w_kv_b, q_norm_w, k_norm_w, w_o_mla, gate_b, w_out, ffn_norm_w, router_w, router_b, w_gate_up, b_gate_up, w_down, b_down):
    assert attn_norm_w.shape[0] == 1, "one layer: rows of meta tokens never feed a later layer"
    b, s, d = x.shape
    n = b * s
    x2 = x.reshape(n, d)

    tm = _tile(s, MIX_TILE)
    tq = _tile(s, 512)
    cos_t, sin_t = _rope_tables(N_META + s)
    pw = _prep_proj_weights(attn_norm_w[0], w_in[0], q_a_norm_w[0], w_q_b[0], kv_a_norm_w[0], w_kv_b[0],
                            q_norm_w[0], k_norm_w[0], gate_b[0])

    tp = _tile(s, 512)
    glu, q, k, vt, g = _proj_call(x2, (cos_t[N_META:], sin_t[N_META:]), pw, tm=tp, tiles_per_seq=s // tp)
    glum, _, km, vmt, _ = _proj_call(meta_tokens.astype(F32), (cos_t[:N_META], sin_t[:N_META]), pw, tm=N_META,
                                     tiles_per_seq=1)

    o = _attn_call(q, k, vt, km, vmt, tq=tq, batch=b)

    rw = router_w[0].T
    rw_hi = rw.astype(BF16)
    rw_lo = (rw - rw_hi.astype(F32)).astype(BF16)
    mix_w = (conv_dw_w[0], conv_dw_b[0][None, :], conv_ln_w[0][None, :], conv_ln_b[0][None, :],
             conv_pw2_w[0].astype(BF16), w_o_mla[0].astype(BF16), w_out[0].astype(BF16), ffn_norm_w[0][None, :],
             rw_hi, rw_lo, router_b[0][:, None])
    t_dma = _tile(tm, 256)
    h, gates, dest, cnt, wgu_b, wd_b, xs = _mixdisp_call(glu, glum, o, g, x2, mix_w, w_gate_up[0], w_down[0], tm=tm,
                                                         tiles_per_seq=s // tm, t_dma=t_dma)

    bm = 512
    n_blocks = (n * TOP_K) // bm + N_EXPERTS
    bexp, brow, nact = _plan_call(cnt[:, 0], bm=bm, n_blocks=n_blocks, cap_blocks=n // bm)
    ys = _expert_call(bexp, brow, nact, xs, wgu_b, b_gate_up[0][:, None, :], wd_b, b_down[0][:, None, :], bm=bm)
    out = _combine_call(dest, ys, gates, h, t=t_dma)
    return out.reshape(b, s, d)
```

```python
import functools
import math

import jax
import jax.numpy as jnp
from jax import lax
from jax.experimental import pallas as pl
from jax.experimental.pallas import tpu as pltpu

N_META = 16
CONV_DIM = 512
CONV_WIDTH = 31
N_HEADS = 8
QK_NOPE_DIM = 64
QK_ROPE_DIM = 32
QK_HEAD_DIM = QK_NOPE_DIM + QK_ROPE_DIM
V_HEAD_DIM = 64
Q_LORA_RANK = 256
KV_LORA_RANK = 128
ROPE_THETA = 10000.0
N_EXPERTS = 32
TOP_K = 4
D_FF = 1024
SWIGLU_LIMIT = 7.0
SWIGLU_ALPHA = 1.702
NORM_EPS = 1e-6
NEG_INF = -1e30

LANES = 128
SUBLANES = 8
BF16_ROWS = 16
ROW_CHUNKS = 8
SCORE_LOOKAHEAD = 2
MIX_TILE = 512
HALO = 32
VMEM_LIMIT = 56 * 1024 * 1024

F32 = jnp.float32
BF16 = jnp.bfloat16


def _cparams(sem):
    return pltpu.CompilerParams(dimension_semantics=sem, vmem_limit_bytes=VMEM_LIMIT)


def _dot(a, b):
    return jnp.dot(a, b, preferred_element_type=F32)


def _dot_t(a, b):
    return lax.dot_general(a, b, (((1,), (1,)), ((), ())), preferred_element_type=F32)


def _sigmoid(x):
    return 1.0 / (1.0 + jnp.exp(-x))


def _rms(x, w):
    return x * lax.rsqrt(jnp.mean(x * x, axis=-1, keepdims=True) + NORM_EPS) * w


C_CONV = 2 * CONV_DIM
C_LAT = Q_LORA_RANK + KV_LORA_RANK + 2 * LANES
C_GATE = 2 * 1024


def _proj_body(x_ref, anw_ref, win_ref, qaw_ref, wq_ref, kvaw_ref, wk_ref, wvt_ref, qlane_ref, klane_ref,
               cos_ref, sin_ref, gb_ref, glu_ref, q_ref, k_ref, vt_ref, g_ref, *, q_scale):
    x = x_ref[...]
    xn = _rms(x, anw_ref[...]).astype(BF16)

    lat = _dot(xn, win_ref[:, C_CONV:C_CONV + C_LAT])
    q_lat = lat[:, :Q_LORA_RANK]
    c_kv = lat[:, Q_LORA_RANK:Q_LORA_RANK + KV_LORA_RANK]
    kpe = lat[:, Q_LORA_RANK + KV_LORA_RANK:Q_LORA_RANK + KV_LORA_RANK + LANES]
    kpe_rot = lat[:, Q_LORA_RANK + KV_LORA_RANK + LANES:]

    qn = _rms(q_lat, qaw_ref[...]).astype(BF16)
    cn = _rms(c_kv, kvaw_ref[...]).astype(BF16)
    qq = _dot(qn, wq_ref[...])
    kv = _dot(cn, wk_ref[...])
    vt_ref[...] = _dot_t(wvt_ref[...], cn).astype(vt_ref.dtype)
    u = _dot(xn, win_ref[:, 0:C_CONV])

    cos = cos_ref[...]
    sin = sin_ref[...]
    q_cos = qlane_ref[...] * cos
    for h in range(N_HEADS):
        q1 = qq[:, h * LANES:(h + 1) * LANES]
        q2 = qq[:, (N_HEADS + h) * LANES:(N_HEADS + h + 1) * LANES]
        ss = jnp.sum(q1 * q1, axis=-1, keepdims=True)
        s = lax.rsqrt(ss * (1.0 / QK_HEAD_DIM) + NORM_EPS) * q_scale
        q_ref[:, h * LANES:(h + 1) * LANES] = (s * (q1 * q_cos + q2 * sin)).astype(q_ref.dtype)

    gl = _dot(xn, win_ref[:, C_CONV + C_LAT:]) + gb_ref[...]

    k_cos = klane_ref[...] * cos
    ss_pe = jnp.sum(kpe * kpe, axis=-1, keepdims=True)
    k_pe_roped = kpe * k_cos + kpe_rot * sin
    for h in range(N_HEADS):
        kn = kv[:, h * LANES:(h + 1) * LANES]
        ss = jnp.sum(kn * kn, axis=-1, keepdims=True) + ss_pe
        s = lax.rsqrt(ss * (1.0 / QK_HEAD_DIM) + NORM_EPS)
        k_ref[:, h * LANES:(h + 1) * LANES] = (s * (kn * k_cos + k_pe_roped)).astype(k_ref.dtype)

    glu_ref[...] = u[:, :CONV_DIM] * _sigmoid(u[:, CONV_DIM:])
    g_ref[...] = _sigmoid(gl).astype(g_ref.dtype)


def _proj_call(x2, tables, weights, *, tm, tiles_per_seq):
    n = x2.shape[0]
    d = x2.shape[1]
    cos_t, sin_t = tables
    (anw, win, qaw, wq, kvaw, wk, wvt, qlane, klane, gb) = weights
    full = lambda a: pl.BlockSpec(a.shape, lambda i: (0,) * a.ndim)
    row = lambda c: pl.BlockSpec((tm, c), lambda i: (i, 0))
    pos = pl.BlockSpec((tm, LANES), lambda i: (i % tiles_per_seq, 0))
    q_scale = (QK_HEAD_DIM ** -0.5) * math.log2(math.e)
    return pl.pallas_call(
        functools.partial(_proj_body, q_scale=q_scale),
        grid=(n // tm,),
        in_specs=[row(d), full(anw), full(win), full(qaw), full(wq), full(kvaw), full(wk), full(wvt),
                  full(qlane), full(klane), pos, pos, full(gb)],
        out_specs=[row(CONV_DIM), row(N_HEADS * LANES), row(N_HEADS * LANES),
                   pl.BlockSpec((N_HEADS * V_HEAD_DIM, tm), lambda i: (0, i)), row(C_GATE)],
        out_shape=[jax.ShapeDtypeStruct((n, CONV_DIM), F32),
                   jax.ShapeDtypeStruct((n, N_HEADS * LANES), BF16),
                   jax.ShapeDtypeStruct((n, N_HEADS * LANES), BF16),
                   jax.ShapeDtypeStruct((N_HEADS * V_HEAD_DIM, n), BF16),
                   jax.ShapeDtypeStruct((n, C_GATE), BF16)],
        compiler_params=_cparams(("parallel",)),
        name="proj",
    )(x2, anw, win, qaw, wq, kvaw, wk, wvt, qlane, klane, cos_t, sin_t, gb)


def _attn_body(q_ref, k_ref, vt_ref, km_ref, vmt_ref, o_ref, m_ref, l_ref, acc_ref):
    i = pl.program_id(1)
    tq = q_ref.shape[0]

    def head_slices(h):
        return slice(h * LANES, (h + 1) * LANES), slice(h * V_HEAD_DIM, (h + 1) * V_HEAD_DIM)

    def _meta():
        ms, ls = [], []
        sts = [_dot_t(km_ref[:, head_slices(h)[0]], q_ref[:, head_slices(h)[0]]) for h in range(N_HEADS)]
        for h in range(N_HEADS):
            ks, vs = head_slices(h)
            st = sts[h]
            m = jnp.max(st, axis=0, keepdims=True)
            p = jnp.exp2(st - m)
            ms.append(m)
            ls.append(jnp.sum(p, axis=0, keepdims=True))
            acc_ref[h] = _dot(vmt_ref[vs, :], p.astype(BF16))
        m_ref[...] = jnp.concatenate(ms, axis=0)
        l_ref[...] = jnp.concatenate(ls, axis=0)

    def step(j, diagonal):
        keys = pl.ds(pl.multiple_of(j * tq, tq), tq)

        def scores(h):
            ks, _ = head_slices(h)
            return _dot_t(k_ref[keys, ks], q_ref[:, ks])

        m_all = m_ref[...]
        l_all = l_ref[...]
        ms, ls = [], []
        pending = [scores(h) for h in range(SCORE_LOOKAHEAD)]
        for h in range(N_HEADS):
            _, vs = head_slices(h)
            st = pending.pop(0)
            if h + SCORE_LOOKAHEAD < N_HEADS:
                pending.append(scores(h + SCORE_LOOKAHEAD))
            if diagonal:
                key = lax.broadcasted_iota(jnp.int32, st.shape, 0)
                qry = lax.broadcasted_iota(jnp.int32, st.shape, 1)
                st = jnp.where(key <= qry, st, NEG_INF)
            m_prev = m_all[h:h + 1, :]
            m_new = jnp.maximum(m_prev, jnp.max(st, axis=0, keepdims=True))
            alpha = jnp.exp2(m_prev - m_new)
            p = jnp.exp2(st - m_new)
            l_new = alpha * l_all[h:h + 1, :] + jnp.sum(p, axis=0, keepdims=True)
            acc = alpha * acc_ref[h] + _dot(vt_ref[vs, keys], p.astype(BF16))
            if diagonal:
                acc_ref[h] = acc / l_new
            else:
                acc_ref[h] = acc
                ms.append(m_new)
                ls.append(l_new)
        if not diagonal:
            m_ref[...] = jnp.concatenate(ms, axis=0)
            l_ref[...] = jnp.concatenate(ls, axis=0)

    def diag_step(j):
        half = tq // 2
        m_all = m_ref[...]
        l_all = l_ref[...]
        chains = [(h, c) for h in range(N_HEADS) for c in range(2)]
        ahead = 2 * SCORE_LOOKAHEAD

        def keys_of(c):
            return pl.ds(pl.multiple_of(j * tq, tq), (c + 1) * half)

        def scores(h, c):
            ks, _ = head_slices(h)
            return _dot_t(k_ref[keys_of(c), ks], q_ref[c * half:(c + 1) * half, ks])

        pending = [scores(*chains[n]) for n in range(ahead)]
        for n, (h, c) in enumerate(chains):
            _, vs = head_slices(h)
            qs = slice(c * half, (c + 1) * half)
            st = pending.pop(0)
            if n + ahead < len(chains):
                pending.append(scores(*chains[n + ahead]))
            key = lax.broadcasted_iota(jnp.int32, st.shape, 0)
            qry = lax.broadcasted_iota(jnp.int32, st.shape, 1) + c * half
            st = jnp.where(key <= qry, st, NEG_INF)
            m_prev = m_all[h:h + 1, qs]
            m_new = jnp.maximum(m_prev, jnp.max(st, axis=0, keepdims=True))
            alpha = jnp.exp2(m_prev - m_new)
            p = jnp.exp2(st - m_new)
            l_new = alpha * l_all[h:h + 1, qs] + jnp.sum(p, axis=0, keepdims=True)
            acc = alpha * acc_ref[h, :, qs] + _dot(vt_ref[vs, keys_of(c)], p.astype(BF16))
            acc_ref[h, :, qs] = acc / l_new

    def full_step(j, carry):
        step(j, False)
        return carry

    _meta()
    lax.fori_loop(0, i, full_step, 0)
    diag_step(i)
    ot = acc_ref[...].reshape(N_HEADS * V_HEAD_DIM, tq)
    o_ref[...] = jnp.transpose(ot).astype(o_ref.dtype)


def _attn_call(q, k, vt, km, vmt, *, tq, batch):
    n = q.shape[0]
    s = n // batch
    nq = s // tq
    const = lambda bi, i: (0, 0)
    return pl.pallas_call(
        _attn_body,
        grid=(batch, nq),
        in_specs=[pl.BlockSpec((tq, N_HEADS * LANES), lambda bi, i: (bi * nq + i, 0)),
                  pl.BlockSpec((s, N_HEADS * LANES), lambda bi, i: (bi, 0)),
                  pl.BlockSpec((N_HEADS * V_HEAD_DIM, s), lambda bi, i: (0, bi)),
                  pl.BlockSpec(km.shape, const), pl.BlockSpec(vmt.shape, const)],
        out_specs=pl.BlockSpec((tq, N_HEADS * V_HEAD_DIM), lambda bi, i: (bi * nq + i, 0)),
        out_shape=jax.ShapeDtypeStruct((n, N_HEADS * V_HEAD_DIM), BF16),
        scratch_shapes=[pltpu.VMEM((N_HEADS, tq), F32), pltpu.VMEM((N_HEADS, tq), F32),
                        pltpu.VMEM((N_HEADS, V_HEAD_DIM, tq), F32)],
        compiler_params=_cparams(("parallel", "arbitrary")),
        name="attn",
    )(q, k, vt, km, vmt)


def _tile_view_shape(rows):
    return (rows // SUBLANES, ROW_CHUNKS, SUBLANES, LANES)


def _to_row_tiles(ref, val):
    groups = val.shape[0] // SUBLANES
    for c in range(ROW_CHUNKS):
        ref[:, c] = val[:, c * LANES:(c + 1) * LANES].reshape(groups, SUBLANES, LANES)


def _row_of(ref, row_group, sublane):
    return ref.at[row_group, :, sublane, :]


def _route(hn, whi_ref, wlo_ref, rb_ref, idx_ref, gate_ref):
    x_hi = hn.astype(BF16)
    x_lo = (hn - x_hi.astype(F32)).astype(BF16)
    w_hi = whi_ref[...]
    logits = _dot_t(w_hi, x_hi) + _dot_t(w_hi, x_lo) + _dot_t(wlo_ref[...], x_hi) + rb_ref[...]

    e_iota = lax.broadcasted_iota(jnp.int32, logits.shape, 0).astype(F32)
    vals, idxs = [], []
    cur = logits
    for _ in range(TOP_K):
        m = jnp.max(cur, axis=0, keepdims=True)
        idx = jnp.min(jnp.where(cur == m, e_iota, float(N_EXPERTS)), axis=0, keepdims=True)
        vals.append(m)
        idxs.append(idx)
        cur = jnp.where(e_iota == idx, -jnp.inf, cur)
    exps = [jnp.exp(v - vals[0]) for v in vals]
    denom = exps[0] + exps[1] + exps[2] + exps[3]
    idx_ref[...] = jnp.concatenate(idxs, axis=0).astype(jnp.int32)
    gate_ref[...] = jnp.concatenate([e / denom for e in exps] + [jnp.zeros_like(denom)] * 4, axis=0)
    return idxs, e_iota


CONV_ROWS = 2 * SUBLANES


def _conv_prepare(glu_ref, halo_ref, glum_ref, xpad_ref, shift_ref, *, tm, tiles_per_seq):
    first = (pl.program_id(0) % tiles_per_seq) == 0
    meta_ctx = jnp.concatenate([jnp.zeros((HALO - N_META, CONV_DIM), F32), glum_ref[...]], axis=0)
    xpad_ref[0:HALO, :] = jnp.where(first, meta_ctx, halo_ref[...])
    xpad_ref[HALO:HALO + tm, :] = glu_ref[...]
    span = tm + HALO - SUBLANES
    for rho in range(1, SUBLANES):
        shift_ref[rho, 0:span, :] = xpad_ref[rho:rho + span, :]


def _conv_rows(dww_ref, dwb_ref, xpad_ref, shift_ref, conv_ref, r0):
    off = HALO - (CONV_WIDTH - 1)
    acc = jnp.zeros((CONV_ROWS, CONV_DIM), F32) + dwb_ref[...]
    for t in range(CONV_WIDTH):
        rho, base = (off + t) % SUBLANES, (off + t) // SUBLANES * SUBLANES
        rows = pl.ds(pl.multiple_of(base + r0, SUBLANES), CONV_ROWS)
        win = xpad_ref[rows, :] if rho == 0 else shift_ref[rho, rows, :]
        acc = acc + dww_ref[t:t + 1, :] * win
    conv_ref[pl.ds(pl.multiple_of(r0, CONV_ROWS), CONV_ROWS), :] = acc


def _mix_tail(conv_ref, o_ref, g_ref, x_ref, lnw_ref, lnb_ref, pw2_ref, wo_ref, wout_ref, fnw_ref, whi_ref, wlo_ref,
              rb_ref, h_ref, hn_ref, idx_ref, gate_ref, *, before):
    y_mla = _dot(o_ref[...], wo_ref[...])
    acc = conv_ref[...]
    mu = jnp.mean(acc, axis=-1, keepdims=True)
    xc = acc - mu
    y = xc * lax.rsqrt(jnp.mean(xc * xc, axis=-1, keepdims=True) + NORM_EPS) * lnw_ref[...] + lnb_ref[...]
    y = y * _sigmoid(y)
    y_conv = _dot(y.astype(BF16), pw2_ref[...])
    g = g_ref[...].astype(F32)
    mixed = g[:, :1024] * y_conv + g[:, 1024:] * y_mla
    h = x_ref[...] + _dot(mixed.astype(BF16), wout_ref[...])
    h_ref[...] = h
    hn = _rms(h, fnw_ref[...])
    before()
    _to_row_tiles(hn_ref, hn)
    return _route(hn, whi_ref, wlo_ref, rb_ref, idx_ref, gate_ref)


def _mixdisp_body(glu_ref, halo_ref, glum_ref, o_ref, g_ref, x_ref, dww_ref, dwb_ref, lnw_ref, lnb_ref,
                  pw2_ref, wo_ref, wout_ref, fnw_ref, whi_ref, wlo_ref, rb_ref, wgu_ref, wd_ref,
                  h_ref, idx_ref, gate_ref, dest_ref, cnt_ref, wgu_out, wd_out, xs_ref,
                  xpad_ref, shift_ref, conv_ref, hn_slots, dvec_ref, dsm0_ref, dsm1_ref, carry_ref, sems, dsem,
                  *, tm, tiles_per_seq, n_tiles, cap, t_dma):
    i = pl.program_id(0)
    slot = i % 2
    other = 1 - slot

    dsm_refs = (dsm0_ref, dsm1_ref)

    def send(sl, grp, s, k):
        d = dsm_refs[sl][0, 0, k * tm + grp * SUBLANES + s]
        return pltpu.make_async_copy(_row_of(hn_slots.at[sl], grp, s), xs_ref.at[d], sems.at[sl])

    def sent(sl):
        return pltpu.make_async_copy(_row_of(hn_slots.at[0], 0, 0), xs_ref.at[0], sems.at[sl])

    @pl.when(i == 0)
    def _init():
        carry_ref[...] = jnp.zeros_like(carry_ref)

    @pl.when(i >= 1)
    def _positions_arrived():
        pltpu.make_async_copy(dvec_ref, dsm_refs[0], dsem).wait()

    @pl.when(i < n_tiles)
    def _prepare():
        _conv_prepare(glu_ref, halo_ref, glum_ref, xpad_ref, shift_ref, tm=tm, tiles_per_seq=tiles_per_seq)

    iters = tm // CONV_ROWS

    def sweep(sl, sends, conv):
        cast_rows = wgu_ref.shape[0] // iters

        def body(it, c):
            if sends:
                for gg in range(CONV_ROWS // SUBLANES):
                    grp = it * (CONV_ROWS // SUBLANES) + gg
                    for s in range(SUBLANES):
                        for k in range(TOP_K):
                            send(sl, grp, s, k).start(priority=k % 2)
                rows = pl.ds(pl.multiple_of(it * cast_rows, cast_rows), cast_rows)
                wgu_out[rows, :] = wgu_ref[rows, :].astype(BF16)
                wd_out[rows, :] = wd_ref[rows, :].astype(BF16)
            if conv:
                _conv_rows(dww_ref, dwb_ref, xpad_ref, shift_ref, conv_ref, it * CONV_ROWS)
            return c

        lax.fori_loop(0, iters, body, 0)

    pl.when(i == 0)(functools.partial(sweep, 0, False, True))
    for sl in range(2):
        pl.when((i >= 1) & (i < n_tiles) & (other == sl))(functools.partial(sweep, sl, True, True))
        pl.when((i == n_tiles) & (other == sl))(functools.partial(sweep, sl, True, False))

    def _wait_slot():
        @pl.when(i >= 2)
        def _():
            _drain(sent(slot), TOP_K * tm)

    @pl.when(i < n_tiles)
    def _tile():
        idxs, e_iota = _mix_tail(conv_ref, o_ref, g_ref, x_ref, lnw_ref, lnb_ref, pw2_ref, wo_ref, wout_ref, fnw_ref,
                                 whi_ref, wlo_ref, rb_ref, h_ref, hn_slots.at[slot], idx_ref, gate_ref,
                                 before=_wait_slot)
        hits = [e_iota == idxs[k] for k in range(TOP_K)]
        onehot = jnp.zeros(e_iota.shape, F32)
        for hk in hits:
            onehot = onehot + jnp.where(hk, 1.0, 0.0)
        r = lax.broadcasted_iota(jnp.int32, (tm, tm), 0)
        c = lax.broadcasted_iota(jnp.int32, (tm, tm), 1)
        before = jnp.where(r < c, 1.0, 0.0).astype(BF16)
        val = _dot(onehot.astype(BF16), before) + carry_ref[...]
        dests = [(jnp.sum(jnp.where(hk, val, 0.0), axis=0, keepdims=True) + idxs[k] * float(cap)).astype(jnp.int32)
                 for k, hk in enumerate(hits)]
        total = carry_ref[...] + jnp.sum(onehot, axis=1, keepdims=True)
        carry_ref[...] = total
        cnt_ref[...] = jnp.broadcast_to(total, cnt_ref.shape).astype(jnp.int32)
        for sub in range(tm // t_dma):
            dest_ref[sub] = jnp.concatenate([dk[:, sub * t_dma:(sub + 1) * t_dma] for dk in dests], axis=1)
        dvec_ref[0] = jnp.concatenate(dests, axis=1)
        for sl in range(2):
            @pl.when(slot == sl)
            def _to_smem():
                pltpu.make_async_copy(dvec_ref, dsm_refs[sl], dsem).start()

    @pl.when(i == n_tiles)
    def _drain_all():
        _drain(sent(other), TOP_K * tm)

        @pl.when(i >= 2)
        def _():
            _drain(sent(slot), TOP_K * tm)


def _mixdisp_call(glu, glum, o, g, x2, weights, wgu, wd, *, tm, tiles_per_seq, t_dma):
    n, d = x2.shape
    n_tiles = n // tm
    cap = n
    wgu2 = wgu.reshape(-1, wgu.shape[-1])
    wd2 = wd.reshape(-1, wd.shape[-1])
    w_rows = wgu2.shape[0] // n_tiles
    assert wgu2.shape[0] == wd2.shape[0] == w_rows * n_tiles
    assert (w_rows // (tm // SUBLANES // 2)) % BF16_ROWS == 0
    cur = lambda i: jnp.minimum(i, n_tiles - 1)
    prev = lambda i: jnp.maximum(i - 1, 0)
    full = lambda a: pl.BlockSpec(a.shape, lambda i: (0,) * a.ndim)
    row = lambda c: pl.BlockSpec((tm, c), lambda i: (cur(i), 0))
    halo = pl.BlockSpec((HALO, CONV_DIM), lambda i: (jnp.maximum(cur(i) * (tm // HALO) - 1, 0), 0))
    wblk = lambda a: pl.BlockSpec((w_rows, a.shape[1]), lambda i: (prev(i), 0))
    subs = tm // t_dma
    outs = pl.pallas_call(
        functools.partial(_mixdisp_body, tm=tm, tiles_per_seq=tiles_per_seq, n_tiles=n_tiles, cap=cap, t_dma=t_dma),
        grid=(n_tiles + 1,),
        in_specs=[row(CONV_DIM), halo, full(glum), row(N_HEADS * V_HEAD_DIM), row(C_GATE), row(d)]
                 + [full(w) for w in weights] + [wblk(wgu2), wblk(wd2)],
        out_specs=[row(d), pl.BlockSpec((TOP_K, tm), lambda i: (0, cur(i))),
                   pl.BlockSpec((2 * TOP_K, tm), lambda i: (0, cur(i))),
                   pl.BlockSpec((subs, 1, TOP_K * t_dma), lambda i: (cur(i), 0, 0)),
                   pl.BlockSpec((N_EXPERTS, LANES), lambda i: (0, 0)),
                   wblk(wgu2), wblk(wd2), pl.BlockSpec(memory_space=pl.ANY)],
        out_shape=[jax.ShapeDtypeStruct((n, d), F32), jax.ShapeDtypeStruct((TOP_K, n), jnp.int32),
                   jax.ShapeDtypeStruct((2 * TOP_K, n), F32),
                   jax.ShapeDtypeStruct((n // t_dma, 1, TOP_K * t_dma), jnp.int32),
                   jax.ShapeDtypeStruct((N_EXPERTS, LANES), jnp.int32),
                   jax.ShapeDtypeStruct(wgu2.shape, BF16), jax.ShapeDtypeStruct(wd2.shape, BF16),
                   jax.ShapeDtypeStruct((N_EXPERTS * cap, ROW_CHUNKS, LANES), F32)],
        scratch_shapes=[pltpu.VMEM((HALO + tm, CONV_DIM), F32),
                        pltpu.VMEM((SUBLANES, HALO + tm, CONV_DIM), F32),
                        pltpu.VMEM((tm, CONV_DIM), F32),
                        pltpu.VMEM((2,) + _tile_view_shape(tm), F32),
                        pltpu.VMEM((1, 1, TOP_K * tm), jnp.int32),
                        pltpu.SMEM((1, 1, TOP_K * tm), jnp.int32), pltpu.SMEM((1, 1, TOP_K * tm), jnp.int32),
                        pltpu.VMEM((N_EXPERTS, 1), F32),
                        pltpu.SemaphoreType.DMA((2,)), pltpu.SemaphoreType.DMA],
        compiler_params=_cparams(("arbitrary",)),
        name="mixdisp",
    )(glu, glu, glum, o, g, x2, *weights, wgu2, wd2)
    h, idx, gates, dest, cnt, wgu_b, wd_b, xs = outs
    return h, gates, dest, cnt, wgu_b.reshape(wgu.shape), wd_b.reshape(wd.shape), xs


WAIT_GROUP = 256


def _drain(copy, count):
    def group(_, carry):
        for _ in range(WAIT_GROUP):
            copy.wait()
        return carry

    lax.fori_loop(0, count // WAIT_GROUP, group, 0)


def _plan_body(cnt_ref, bexp_ref, brow_ref, nact_ref, *, bm, n_blocks, cap_blocks):
    def per_expert(e, carry):
        blk, last = carry
        nb = (cnt_ref[e] + (bm - 1)) // bm

        def fill(b, _):
            bexp_ref[b] = e
            brow_ref[b] = e * cap_blocks + (b - blk)
            return 0

        lax.fori_loop(blk, blk + nb, fill, 0)
        return blk + nb, jnp.where(nb > 0, e, last)

    n_act, last = lax.fori_loop(0, N_EXPERTS, per_expert, (jnp.int32(0), jnp.int32(0)))
    nact_ref[0] = n_act

    def tail(b, _):
        bexp_ref[b] = last
        brow_ref[b] = 0
        return 0

    lax.fori_loop(n_act, n_blocks, tail, 0)


def _plan_call(cnt, *, bm, n_blocks, cap_blocks):
    smem = lambda: pl.BlockSpec(memory_space=pltpu.SMEM)
    blocks = jax.ShapeDtypeStruct((n_blocks,), jnp.int32)
    return pl.pallas_call(
        functools.partial(_plan_body, bm=bm, n_blocks=n_blocks, cap_blocks=cap_blocks),
        in_specs=[smem()],
        out_specs=[smem(), smem(), smem()],
        out_shape=[blocks, blocks, jax.ShapeDtypeStruct((1,), jnp.int32)],
        name="plan",
    )(cnt)


def _expert_body(bexp_ref, brow_ref, nact_ref, xs_ref, wgu_ref, bgu_ref, wd_ref, bd_ref, ys_ref,
                 xin_ref, yout_ref, in_sems, out_sems, *, bm):
    b = pl.program_id(0)
    n_act = nact_ref[0]
    slot = b % 2
    other = 1 - slot

    def chunk(c):
        return slice(c * LANES, (c + 1) * LANES)

    def block_rows(blk):
        return pl.ds(pl.multiple_of(brow_ref[blk] * bm, bm), bm)

    def fetch(blk, sl, c):
        return pltpu.make_async_copy(xs_ref.at[block_rows(blk), c], xin_ref.at[sl, :, chunk(c)], in_sems.at[sl])

    def write_back(blk, sl, c):
        return pltpu.make_async_copy(yout_ref.at[sl, :, chunk(c)], ys_ref.at[block_rows(blk), c], out_sems.at[sl])

    @pl.when(b == 0)
    def _prime():
        for c in range(ROW_CHUNKS):
            fetch(0, 0, c).start()

    @pl.when(b < n_act)
    def _block():
        for c in range(ROW_CHUNKS):
            fetch(b, slot, c).wait()

        @pl.when(b + 1 < n_act)
        def _prefetch():
            for c in range(ROW_CHUNKS):
                fetch(b + 1, other, c).start()

        @pl.when(b >= 2)
        def _free_result_slot():
            for c in range(ROW_CHUNKS):
                write_back(b - 2, slot, c).wait()

        x = xin_ref[slot].astype(BF16)
        gu = _dot(x, wgu_ref[0]) + bgu_ref[0]
        g = jnp.minimum(gu[:, :D_FF], SWIGLU_LIMIT)
        u = jnp.clip(gu[:, D_FF:], -SWIGLU_LIMIT, SWIGLU_LIMIT)
        act = (u + 1.0) * (g * _sigmoid(SWIGLU_ALPHA * g))
        yout_ref[slot] = _dot(act.astype(BF16), wd_ref[0]) + bd_ref[0]
        for c in range(ROW_CHUNKS):
            write_back(b, slot, c).start()

    @pl.when(b == n_act)
    def _drain_results():
        for c in range(ROW_CHUNKS):
            write_back(b - 1, other, c).wait()

        @pl.when(b >= 2)
        def _older():
            for c in range(ROW_CHUNKS):
                write_back(b - 2, slot, c).wait()


def _expert_call(bexp, brow, nact, xs, wgu, bgu, wd, bd, *, bm):
    d = ROW_CHUNKS * LANES
    n_blocks = bexp.shape[0]
    wsel = lambda b, be, br, na: (be[jnp.minimum(b, n_blocks - 1)], 0, 0)
    hbm = pl.BlockSpec(memory_space=pl.ANY)
    return pl.pallas_call(
        functools.partial(_expert_body, bm=bm),
        grid_spec=pltpu.PrefetchScalarGridSpec(
            num_scalar_prefetch=3, grid=(n_blocks + 1,),
            in_specs=[hbm, pl.BlockSpec((1, d, 2 * D_FF), wsel), pl.BlockSpec((1, 1, 2 * D_FF), wsel),
                      pl.BlockSpec((1, D_FF, d), wsel), pl.BlockSpec((1, 1, d), wsel)],
            out_specs=hbm,
            scratch_shapes=[pltpu.VMEM((2, bm, d), F32), pltpu.VMEM((2, bm, d), F32),
                            pltpu.SemaphoreType.DMA((2,)), pltpu.SemaphoreType.DMA((2,))]),
        out_shape=jax.ShapeDtypeStruct(xs.shape, F32),
        input_output_aliases={3: 0},
        compiler_params=_cparams(("arbitrary",)),
        name="experts",
    )(bexp, brow, nact, xs, wgu, bgu, wd, bd)


def _combine_body(dest_ref, ys_ref, gate_ref, h_ref, out_ref, buf_ref, gt_ref, sems, *, t, steps):
    i = pl.program_id(0)
    slot = i % 2
    other = 1 - slot

    def row_gather(d, k, grp, s, sl):
        return pltpu.make_async_copy(ys_ref.at[d], _row_of(buf_ref.at[sl, k], grp, s), sems.at[sl])

    @pl.when(i == 0)
    def _first():
        buf_ref[...] = jnp.zeros_like(buf_ref)

    @pl.when(i > 0)
    def _wait_previous_tile():
        _drain(row_gather(0, 0, 0, 0, other), TOP_K * t)

    gt_ref[...] = jnp.transpose(gate_ref[...])

    def body(grp, carry):
        for s in range(SUBLANES):
            for k in range(TOP_K):
                d = dest_ref[0, 0, k * t + grp * SUBLANES + s]
                row_gather(d, k, grp, s, slot).start(priority=k % 2)
        rows = pl.ds(pl.multiple_of(grp * SUBLANES, SUBLANES), SUBLANES)
        acc = h_ref[rows, :]
        g = gt_ref[rows, :]
        for k in range(TOP_K):
            tiles = buf_ref[other, k, grp]
            acc = acc + g[:, k:k + 1] * jnp.concatenate([tiles[c] for c in range(ROW_CHUNKS)], axis=1)
        out_ref[rows, :] = acc
        return carry

    lax.fori_loop(0, t // SUBLANES, body, 0)

    @pl.when(i == steps)
    def _drain_extra_gather():
        _drain(row_gather(0, 0, 0, 0, slot), TOP_K * t)


def _combine_call(dest, ys, gates, h, *, t):
    n, d = h.shape
    steps = n // t
    prev = lambda i: jnp.maximum(i - 1, 0)
    return pl.pallas_call(
        functools.partial(_combine_body, t=t, steps=steps),
        grid=(steps + 1,),
        in_specs=[pl.BlockSpec((1, 1, TOP_K * t), lambda i: (jnp.minimum(i, steps - 1), 0, 0),
                               memory_space=pltpu.SMEM),
                  pl.BlockSpec(memory_space=pl.ANY),
                  pl.BlockSpec((2 * TOP_K, t), lambda i: (0, prev(i))),
                  pl.BlockSpec((t, d), lambda i: (prev(i), 0))],
        out_specs=pl.BlockSpec((t, d), lambda i: (prev(i), 0)),
        out_shape=jax.ShapeDtypeStruct((n, d), F32),
        scratch_shapes=[pltpu.VMEM((2, TOP_K) + _tile_view_shape(t), F32), pltpu.VMEM((t, 2 * TOP_K), F32),
                        pltpu.SemaphoreType.DMA((2,))],
        compiler_params=_cparams(("arbitrary",)),
        name="combine",
    )(dest, ys, gates, h)


def _rot_half(z):
    half = QK_ROPE_DIM // 2
    return jnp.concatenate([-z[:, half:], z[:, :half]], axis=1)


def _pe_slot(z):
    rows = z.shape[0]
    return jnp.concatenate([jnp.zeros((rows, QK_NOPE_DIM), F32), z,
                            jnp.zeros((rows, LANES - QK_HEAD_DIM), F32)], axis=1)


def _prep_proj_weights(attn_norm_w, w_in, q_a_norm_w, w_q_b, kv_a_norm_w, w_kv_b, q_norm_w, k_norm_w, gate_b):
    c0 = 2 * CONV_DIM
    c1 = c0 + Q_LORA_RANK
    c2 = c1 + KV_LORA_RANK
    c3 = c2 + QK_ROPE_DIM
    kpe = w_in[:, c2:c3]
    win = jnp.concatenate([w_in[:, :c2].astype(BF16), _pe_slot(kpe).astype(BF16),
                           _pe_slot(_rot_half(kpe * k_norm_w[None, QK_NOPE_DIM:])).astype(BF16),
                           w_in[:, c3:].astype(BF16)], axis=1)

    pad = jnp.zeros((Q_LORA_RANK, LANES - QK_HEAD_DIM), F32)
    q1, q2 = [], []
    for h in range(N_HEADS):
        cols = w_q_b[:, h * QK_HEAD_DIM:(h + 1) * QK_HEAD_DIM]
        q1.append(jnp.concatenate([cols, pad], axis=1))
        q2.append(_pe_slot(_rot_half(cols[:, QK_NOPE_DIM:] * q_norm_w[None, QK_NOPE_DIM:])))
    wq = jnp.concatenate(q1 + q2, axis=1).astype(BF16)

    kpad = jnp.zeros((KV_LORA_RANK, LANES - QK_NOPE_DIM), F32)
    ks, vs = [], []
    per_head = QK_NOPE_DIM + V_HEAD_DIM
    for h in range(N_HEADS):
        cols = w_kv_b[:, h * per_head:(h + 1) * per_head]
        ks.append(jnp.concatenate([cols[:, :QK_NOPE_DIM], kpad], axis=1))
        vs.append(cols[:, QK_NOPE_DIM:])
    wk = jnp.concatenate(ks, axis=1).astype(BF16)
    wvt = jnp.concatenate(vs, axis=1).T.astype(BF16)

    lane_pad = jnp.zeros((LANES - QK_HEAD_DIM,), F32)
    qlane = jnp.concatenate([q_norm_w, lane_pad])[None, :]
    klane = jnp.concatenate([k_norm_w, lane_pad])[None, :]
    return (attn_norm_w[None, :], win, q_a_norm_w[None, :], wq, kv_a_norm_w[None, :], wk, wvt, qlane, klane,
            gate_b[None, :])


def _rope_tables(length):
    half = QK_ROPE_DIM // 2
    inv_freq = ROPE_THETA ** (-jnp.arange(half, dtype=F32) / half)
    ang = jnp.arange(length, dtype=F32)[:, None] * inv_freq[None, :]
    cos, sin = jnp.cos(ang), jnp.sin(ang)
    ones = jnp.ones((length, QK_NOPE_DIM), F32)
    tail = LANES - QK_HEAD_DIM
    cos_t = jnp.concatenate([ones, cos, cos, jnp.ones((length, tail), F32)], axis=1)
    sin_t = jnp.concatenate([0.0 * ones, sin, sin, jnp.zeros((length, tail), F32)], axis=1)
    return cos_t, sin_t


def _tile(n, pref):
    t = pref
    while n % t:
        t //= 2
    return t


def kernel(x, meta_tokens, attn_norm_w, w_in, conv_dw_w, conv_dw_b, conv_ln_w, conv_ln_b, conv_pw2_w, q_a_norm_w, w_q_b, kv_a_norm_w, w_kv_b, q_norm_w, k_norm_w, w_o_mla, gate_b, w_out, ffn_norm_w, router_w, router_b, w_gate_up, b_gate_up, w_down, b_down):
    assert attn_norm_w.shape[0] == 1, "one layer: rows of meta tokens never feed a later layer"
    b, s, d = x.shape
    n = b * s
    x2 = x.reshape(n, d)

    tm = _tile(s, MIX_TILE)
    tq = _tile(s, 512)
    cos_t, sin_t = _rope_tables(N_META + s)
    pw = _prep_proj_weights(attn_norm_w[0], w_in[0], q_a_norm_w[0], w_q_b[0], kv_a_norm_w[0], w_kv_b[0],
                            q_norm_w[0], k_norm_w[0], gate_b[0])

    tp = _tile(s, 512)
    glu, q, k, vt, g = _proj_call(x2, (cos_t[N_META:], sin_t[N_META:]), pw, tm=tp, tiles_per_seq=s // tp)
    glum, _, km, vmt, _ = _proj_call(meta_tokens.astype(F32), (cos_t[:N_META], sin_t[:N_META]), pw, tm=N_META,
                                     tiles_per_seq=1)

    o = _attn_call(q, k, vt, km, vmt, tq=tq, batch=b)

    rw = router_w[0].T
    rw_hi = rw.astype(BF16)
    rw_lo = (rw - rw_hi.astype(F32)).astype(BF16)
    mix_w = (conv_dw_w[0], conv_dw_b[0][None, :], conv_ln_w[0][None, :], conv_ln_b[0][None, :],
             conv_pw2_w[0].astype(BF16), w_o_mla[0].astype(BF16), w_out[0].astype(BF16), ffn_norm_w[0][None, :],
             rw_hi, rw_lo, router_b[0][:, None])
    t_dma = _tile(tm, 256)
    h, gates, dest, cnt, wgu_b, wd_b, xs = _mixdisp_call(glu, glum, o, g, x2, mix_w, w_gate_up[0], w_down[0], tm=tm,
                                                         tiles_per_seq=s // tm, t_dma=t_dma)

    bm = 512
    n_blocks = (n * TOP_K) // bm + N_EXPERTS
    bexp, brow, nact = _plan_call(cnt[:, 0], bm=bm, n_blocks=n_blocks, cap_blocks=n // bm)
    ys = _expert_call(bexp, brow, nact, xs, wgu_b, b_gate_up[0][:, None, :], wd_b, b_down[0][:, None, :], bm=bm)
    out = _combine_call(dest, ys, gates, h, t=t_dma)
    return out.reshape(b, s, d)
```

```python
import functools
import math

import jax
import jax.numpy as jnp
from jax import lax
from jax.experimental import pallas as pl
from jax.experimental.pallas import tpu as pltpu

N_META = 16
CONV_DIM = 512
CONV_WIDTH = 31
N_HEADS = 8
QK_NOPE_DIM = 64
QK_ROPE_DIM = 32
QK_HEAD_DIM = QK_NOPE_DIM + QK_ROPE_DIM
V_HEAD_DIM = 64
Q_LORA_RANK = 256
KV_LORA_RANK = 128
ROPE_THETA = 10000.0
N_EXPERTS = 32
TOP_K = 4
D_FF = 1024
SWIGLU_LIMIT = 7.0
SWIGLU_ALPHA = 1.702
NORM_EPS = 1e-6
NEG_INF = -1e30

LANES = 128
SUBLANES = 8
BF16_ROWS = 16
ROW_CHUNKS = 8
SCORE_LOOKAHEAD = 3
MIX_TILE = 512
HALO = 32
VMEM_LIMIT = 56 * 1024 * 1024

F32 = jnp.float32
BF16 = jnp.bfloat16


def _cparams(sem):
    return pltpu.CompilerParams(dimension_semantics=sem, vmem_limit_bytes=VMEM_LIMIT)


def _dot(a, b):
    return jnp.dot(a, b, preferred_element_type=F32)


def _dot_t(a, b):
    return lax.dot_general(a, b, (((1,), (1,)), ((), ())), preferred_element_type=F32)


def _sigmoid(x):
    return 1.0 / (1.0 + jnp.exp(-x))


def _rms(x, w):
    return x * lax.rsqrt(jnp.mean(x * x, axis=-1, keepdims=True) + NORM_EPS) * w


C_CONV = 2 * CONV_DIM
C_LAT = Q_LORA_RANK + KV_LORA_RANK + 2 * LANES
C_GATE = 2 * 1024


def _proj_body(x_ref, anw_ref, win_ref, qaw_ref, wq_ref, kvaw_ref, wk_ref, wvt_ref, qlane_ref, klane_ref,
               cos_ref, sin_ref, gb_ref, glu_ref, q_ref, k_ref, vt_ref, g_ref, *, q_scale):
    x = x_ref[...]
    xn = _rms(x, anw_ref[...]).astype(BF16)

    lat = _dot(xn, win_ref[:, C_CONV:C_CONV + C_LAT])
    q_lat = lat[:, :Q_LORA_RANK]
    c_kv = lat[:, Q_LORA_RANK:Q_LORA_RANK + KV_LORA_RANK]
    kpe = lat[:, Q_LORA_RANK + KV_LORA_RANK:Q_LORA_RANK + KV_LORA_RANK + LANES]
    kpe_rot = lat[:, Q_LORA_RANK + KV_LORA_RANK + LANES:]

    qn = _rms(q_lat, qaw_ref[...]).astype(BF16)
    cn = _rms(c_kv, kvaw_ref[...]).astype(BF16)
    qq = _dot(qn, wq_ref[...])
    kv = _dot(cn, wk_ref[...])
    vt_ref[...] = _dot_t(wvt_ref[...], cn).astype(vt_ref.dtype)
    u = _dot(xn, win_ref[:, 0:C_CONV])

    cos = cos_ref[...]
    sin = sin_ref[...]
    q_cos = qlane_ref[...] * cos
    for h in range(N_HEADS):
        q1 = qq[:, h * LANES:(h + 1) * LANES]
        q2 = qq[:, (N_HEADS + h) * LANES:(N_HEADS + h + 1) * LANES]
        ss = jnp.sum(q1 * q1, axis=-1, keepdims=True)
        s = lax.rsqrt(ss * (1.0 / QK_HEAD_DIM) + NORM_EPS) * q_scale
        q_ref[:, h * LANES:(h + 1) * LANES] = (s * (q1 * q_cos + q2 * sin)).astype(q_ref.dtype)

    gl = _dot(xn, win_ref[:, C_CONV + C_LAT:]) + gb_ref[...]

    k_cos = klane_ref[...] * cos
    ss_pe = jnp.sum(kpe * kpe, axis=-1, keepdims=True)
    k_pe_roped = kpe * k_cos + kpe_rot * sin
    for h in range(N_HEADS):
        kn = kv[:, h * LANES:(h + 1) * LANES]
        ss = jnp.sum(kn * kn, axis=-1, keepdims=True) + ss_pe
        s = lax.rsqrt(ss * (1.0 / QK_HEAD_DIM) + NORM_EPS)
        k_ref[:, h * LANES:(h + 1) * LANES] = (s * (kn * k_cos + k_pe_roped)).astype(k_ref.dtype)

    glu_ref[...] = u[:, :CONV_DIM] * _sigmoid(u[:, CONV_DIM:])
    g_ref[...] = _sigmoid(gl).astype(g_ref.dtype)


def _proj_call(x2, tables, weights, *, tm, tiles_per_seq):
    n = x2.shape[0]
    d = x2.shape[1]
    cos_t, sin_t = tables
    (anw, win, qaw, wq, kvaw, wk, wvt, qlane, klane, gb) = weights
    full = lambda a: pl.BlockSpec(a.shape, lambda i: (0,) * a.ndim)
    row = lambda c: pl.BlockSpec((tm, c), lambda i: (i, 0))
    pos = pl.BlockSpec((tm, LANES), lambda i: (i % tiles_per_seq, 0))
    q_scale = (QK_HEAD_DIM ** -0.5) * math.log2(math.e)
    return pl.pallas_call(
        functools.partial(_proj_body, q_scale=q_scale),
        grid=(n // tm,),
        in_specs=[row(d), full(anw), full(win), full(qaw), full(wq), full(kvaw), full(wk), full(wvt),
                  full(qlane), full(klane), pos, pos, full(gb)],
        out_specs=[row(CONV_DIM), row(N_HEADS * LANES), row(N_HEADS * LANES),
                   pl.BlockSpec((N_HEADS * V_HEAD_DIM, tm), lambda i: (0, i)), row(C_GATE)],
        out_shape=[jax.ShapeDtypeStruct((n, CONV_DIM), F32),
                   jax.ShapeDtypeStruct((n, N_HEADS * LANES), BF16),
                   jax.ShapeDtypeStruct((n, N_HEADS * LANES), BF16),
                   jax.ShapeDtypeStruct((N_HEADS * V_HEAD_DIM, n), BF16),
                   jax.ShapeDtypeStruct((n, C_GATE), BF16)],
        compiler_params=_cparams(("parallel",)),
        name="proj",
    )(x2, anw, win, qaw, wq, kvaw, wk, wvt, qlane, klane, cos_t, sin_t, gb)


def _attn_body(q_ref, k_ref, vt_ref, km_ref, vmt_ref, o_ref, m_ref, l_ref, acc_ref):
    i = pl.program_id(1)
    tq = q_ref.shape[0]

    def head_slices(h):
        return slice(h * LANES, (h + 1) * LANES), slice(h * V_HEAD_DIM, (h + 1) * V_HEAD_DIM)

    def _meta():
        ms, ls = [], []
        sts = [_dot_t(km_ref[:, head_slices(h)[0]], q_ref[:, head_slices(h)[0]]) for h in range(N_HEADS)]
        for h in range(N_HEADS):
            ks, vs = head_slices(h)
            st = sts[h]
            m = jnp.max(st, axis=0, keepdims=True)
            p = jnp.exp2(st - m)
            ms.append(m)
            ls.append(jnp.sum(p, axis=0, keepdims=True))
            acc_ref[h] = _dot(vmt_ref[vs, :], p.astype(BF16))
        m_ref[...] = jnp.concatenate(ms, axis=0)
        l_ref[...] = jnp.concatenate(ls, axis=0)

    def step(j, diagonal):
        keys = pl.ds(pl.multiple_of(j * tq, tq), tq)

        def scores(h):
            ks, _ = head_slices(h)
            return _dot_t(k_ref[keys, ks], q_ref[:, ks])

        m_all = m_ref[...]
        l_all = l_ref[...]
        ms, ls = [], []
        pending = [scores(h) for h in range(SCORE_LOOKAHEAD)]
        for h in range(N_HEADS):
            _, vs = head_slices(h)
            st = pending.pop(0)
            if h + SCORE_LOOKAHEAD < N_HEADS:
                pending.append(scores(h + SCORE_LOOKAHEAD))
            if diagonal:
                key = lax.broadcasted_iota(jnp.int32, st.shape, 0)
                qry = lax.broadcasted_iota(jnp.int32, st.shape, 1)
                st = jnp.where(key <= qry, st, NEG_INF)
            m_prev = m_all[h:h + 1, :]
            m_new = jnp.maximum(m_prev, jnp.max(st, axis=0, keepdims=True))
            alpha = jnp.exp2(m_prev - m_new)
            p = jnp.exp2(st - m_new)
            l_new = alpha * l_all[h:h + 1, :] + jnp.sum(p, axis=0, keepdims=True)
            acc = alpha * acc_ref[h] + _dot(vt_ref[vs, keys], p.astype(BF16))
            if diagonal:
                acc_ref[h] = acc / l_new
            else:
                acc_ref[h] = acc
                ms.append(m_new)
                ls.append(l_new)
        if not diagonal:
            m_ref[...] = jnp.concatenate(ms, axis=0)
            l_ref[...] = jnp.concatenate(ls, axis=0)

    def diag_step(j):
        half = tq // 2
        m_all = m_ref[...]
        l_all = l_ref[...]
        chains = [(h, c) for h in range(N_HEADS) for c in range(2)]
        ahead = 2 * SCORE_LOOKAHEAD

        def keys_of(c):
            return pl.ds(pl.multiple_of(j * tq, tq), (c + 1) * half)

        def scores(h, c):
            ks, _ = head_slices(h)
            return _dot_t(k_ref[keys_of(c), ks], q_ref[c * half:(c + 1) * half, ks])

        pending = [scores(*chains[n]) for n in range(ahead)]
        for n, (h, c) in enumerate(chains):
            _, vs = head_slices(h)
            qs = slice(c * half, (c + 1) * half)
            st = pending.pop(0)
            if n + ahead < len(chains):
                pending.append(scores(*chains[n + ahead]))
            key = lax.broadcasted_iota(jnp.int32, st.shape, 0)
            qry = lax.broadcasted_iota(jnp.int32, st.shape, 1) + c * half
            st = jnp.where(key <= qry, st, NEG_INF)
            m_prev = m_all[h:h + 1, qs]
            m_new = jnp.maximum(m_prev, jnp.max(st, axis=0, keepdims=True))
            alpha = jnp.exp2(m_prev - m_new)
            p = jnp.exp2(st - m_new)
            l_new = alpha * l_all[h:h + 1, qs] + jnp.sum(p, axis=0, keepdims=True)
            acc = alpha * acc_ref[h, :, qs] + _dot(vt_ref[vs, keys_of(c)], p.astype(BF16))
            acc_ref[h, :, qs] = acc / l_new

    def full_step(j, carry):
        step(j, False)
        return carry

    _meta()
    lax.fori_loop(0, i, full_step, 0)
    diag_step(i)
    ot = acc_ref[...].reshape(N_HEADS * V_HEAD_DIM, tq)
    o_ref[...] = jnp.transpose(ot).astype(o_ref.dtype)


def _attn_call(q, k, vt, km, vmt, *, tq, batch):
    n = q.shape[0]
    s = n // batch
    nq = s // tq
    const = lambda bi, i: (0, 0)
    return pl.pallas_call(
        _attn_body,
        grid=(batch, nq),
        in_specs=[pl.BlockSpec((tq, N_HEADS * LANES), lambda bi, i: (bi * nq + i, 0)),
                  pl.BlockSpec((s, N_HEADS * LANES), lambda bi, i: (bi, 0)),
                  pl.BlockSpec((N_HEADS * V_HEAD_DIM, s), lambda bi, i: (0, bi)),
                  pl.BlockSpec(km.shape, const), pl.BlockSpec(vmt.shape, const)],
        out_specs=pl.BlockSpec((tq, N_HEADS * V_HEAD_DIM), lambda bi, i: (bi * nq + i, 0)),
        out_shape=jax.ShapeDtypeStruct((n, N_HEADS * V_HEAD_DIM), BF16),
        scratch_shapes=[pltpu.VMEM((N_HEADS, tq), F32), pltpu.VMEM((N_HEADS, tq), F32),
                        pltpu.VMEM((N_HEADS, V_HEAD_DIM, tq), F32)],
        compiler_params=_cparams(("parallel", "arbitrary")),
        name="attn",
    )(q, k, vt, km, vmt)


def _tile_view_shape(rows):
    return (rows // SUBLANES, ROW_CHUNKS, SUBLANES, LANES)


def _to_row_tiles(ref, val):
    groups = val.shape[0] // SUBLANES
    for c in range(ROW_CHUNKS):
        ref[:, c] = val[:, c * LANES:(c + 1) * LANES].reshape(groups, SUBLANES, LANES)


def _row_of(ref, row_group, sublane):
    return ref.at[row_group, :, sublane, :]


def _route(hn, whi_ref, wlo_ref, rb_ref, idx_ref, gate_ref):
    x_hi = hn.astype(BF16)
    x_lo = (hn - x_hi.astype(F32)).astype(BF16)
    w_hi = whi_ref[...]
    logits = _dot_t(w_hi, x_hi) + _dot_t(w_hi, x_lo) + _dot_t(wlo_ref[...], x_hi) + rb_ref[...]

    e_iota = lax.broadcasted_iota(jnp.int32, logits.shape, 0).astype(F32)
    vals, idxs = [], []
    cur = logits
    for _ in range(TOP_K):
        m = jnp.max(cur, axis=0, keepdims=True)
        idx = jnp.min(jnp.where(cur == m, e_iota, float(N_EXPERTS)), axis=0, keepdims=True)
        vals.append(m)
        idxs.append(idx)
        cur = jnp.where(e_iota == idx, -jnp.inf, cur)
    exps = [jnp.exp(v - vals[0]) for v in vals]
    denom = exps[0] + exps[1] + exps[2] + exps[3]
    idx_ref[...] = jnp.concatenate(idxs, axis=0).astype(jnp.int32)
    gate_ref[...] = jnp.concatenate([e / denom for e in exps] + [jnp.zeros_like(denom)] * 4, axis=0)
    return idxs, e_iota


CONV_ROWS = 2 * SUBLANES


def _conv_prepare(glu_ref, halo_ref, glum_ref, xpad_ref, shift_ref, *, tm, tiles_per_seq):
    first = (pl.program_id(0) % tiles_per_seq) == 0
    meta_ctx = jnp.concatenate([jnp.zeros((HALO - N_META, CONV_DIM), F32), glum_ref[...]], axis=0)
    xpad_ref[0:HALO, :] = jnp.where(first, meta_ctx, halo_ref[...])
    xpad_ref[HALO:HALO + tm, :] = glu_ref[...]
    span = tm + HALO - SUBLANES
    for rho in range(1, SUBLANES):
        shift_ref[rho, 0:span, :] = xpad_ref[rho:rho + span, :]


def _conv_rows(dww_ref, dwb_ref, xpad_ref, shift_ref, conv_ref, r0):
    off = HALO - (CONV_WIDTH - 1)
    acc = jnp.zeros((CONV_ROWS, CONV_DIM), F32) + dwb_ref[...]
    for t in range(CONV_WIDTH):
        rho, base = (off + t) % SUBLANES, (off + t) // SUBLANES * SUBLANES
        rows = pl.ds(pl.multiple_of(base + r0, SUBLANES), CONV_ROWS)
        win = xpad_ref[rows, :] if rho == 0 else shift_ref[rho, rows, :]
        acc = acc + dww_ref[t:t + 1, :] * win
    conv_ref[pl.ds(pl.multiple_of(r0, CONV_ROWS), CONV_ROWS), :] = acc


def _mix_tail(conv_ref, o_ref, g_ref, x_ref, lnw_ref, lnb_ref, pw2_ref, wo_ref, wout_ref, fnw_ref, whi_ref, wlo_ref,
              rb_ref, h_ref, hn_ref, idx_ref, gate_ref, *, before):
    y_mla = _dot(o_ref[...], wo_ref[...])
    acc = conv_ref[...]
    mu = jnp.mean(acc, axis=-1, keepdims=True)
    xc = acc - mu
    y = xc * lax.rsqrt(jnp.mean(xc * xc, axis=-1, keepdims=True) + NORM_EPS) * lnw_ref[...] + lnb_ref[...]
    y = y * _sigmoid(y)
    y_conv = _dot(y.astype(BF16), pw2_ref[...])
    g = g_ref[...].astype(F32)
    mixed = g[:, :1024] * y_conv + g[:, 1024:] * y_mla
    h = x_ref[...] + _dot(mixed.astype(BF16), wout_ref[...])
    h_ref[...] = h
    hn = _rms(h, fnw_ref[...])
    before()
    _to_row_tiles(hn_ref, hn)
    return _route(hn, whi_ref, wlo_ref, rb_ref, idx_ref, gate_ref)


def _mixdisp_body(glu_ref, halo_ref, glum_ref, o_ref, g_ref, x_ref, dww_ref, dwb_ref, lnw_ref, lnb_ref,
                  pw2_ref, wo_ref, wout_ref, fnw_ref, whi_ref, wlo_ref, rb_ref, wgu_ref, wd_ref,
                  h_ref, idx_ref, gate_ref, dest_ref, cnt_ref, wgu_out, wd_out, xs_ref,
                  xpad_ref, shift_ref, conv_ref, hn_slots, dvec_ref, dsm0_ref, dsm1_ref, carry_ref, sems, dsem,
                  *, tm, tiles_per_seq, n_tiles, cap, t_dma):
    i = pl.program_id(0)
    slot = i % 2
    other = 1 - slot

    dsm_refs = (dsm0_ref, dsm1_ref)

    def send(sl, grp, s, k):
        d = dsm_refs[sl][0, 0, k * tm + grp * SUBLANES + s]
        return pltpu.make_async_copy(_row_of(hn_slots.at[sl], grp, s), xs_ref.at[d], sems.at[sl])

    def sent(sl):
        return pltpu.make_async_copy(_row_of(hn_slots.at[0], 0, 0), xs_ref.at[0], sems.at[sl])

    @pl.when(i == 0)
    def _init():
        carry_ref[...] = jnp.zeros_like(carry_ref)

    @pl.when(i >= 1)
    def _positions_arrived():
        pltpu.make_async_copy(dvec_ref, dsm_refs[0], dsem).wait()

    @pl.when(i < n_tiles)
    def _prepare():
        _conv_prepare(glu_ref, halo_ref, glum_ref, xpad_ref, shift_ref, tm=tm, tiles_per_seq=tiles_per_seq)

    iters = tm // CONV_ROWS

    def sweep(sl, sends, conv):
        cast_rows = wgu_ref.shape[0] // iters

        def body(it, c):
            if sends:
                for gg in range(CONV_ROWS // SUBLANES):
                    grp = it * (CONV_ROWS // SUBLANES) + gg
                    for s in range(SUBLANES):
                        for k in range(TOP_K):
                            send(sl, grp, s, k).start(priority=k % 2)
                rows = pl.ds(pl.multiple_of(it * cast_rows, cast_rows), cast_rows)
                wgu_out[rows, :] = wgu_ref[rows, :].astype(BF16)
                wd_out[rows, :] = wd_ref[rows, :].astype(BF16)
            if conv:
                _conv_rows(dww_ref, dwb_ref, xpad_ref, shift_ref, conv_ref, it * CONV_ROWS)
            return c

        lax.fori_loop(0, iters, body, 0)

    pl.when(i == 0)(functools.partial(sweep, 0, False, True))
    for sl in range(2):
        pl.when((i >= 1) & (i < n_tiles) & (other == sl))(functools.partial(sweep, sl, True, True))
        pl.when((i == n_tiles) & (other == sl))(functools.partial(sweep, sl, True, False))

    def _wait_slot():
        @pl.when(i >= 2)
        def _():
            _drain(sent(slot), TOP_K * tm)

    @pl.when(i < n_tiles)
    def _tile():
        idxs, e_iota = _mix_tail(conv_ref, o_ref, g_ref, x_ref, lnw_ref, lnb_ref, pw2_ref, wo_ref, wout_ref, fnw_ref,
                                 whi_ref, wlo_ref, rb_ref, h_ref, hn_slots.at[slot], idx_ref, gate_ref,
                                 before=_wait_slot)
        hits = [e_iota == idxs[k] for k in range(TOP_K)]
        onehot = jnp.zeros(e_iota.shape, F32)
        for hk in hits:
            onehot = onehot + jnp.where(hk, 1.0, 0.0)
        r = lax.broadcasted_iota(jnp.int32, (tm, tm), 0)
        c = lax.broadcasted_iota(jnp.int32, (tm, tm), 1)
        before = jnp.where(r < c, 1.0, 0.0).astype(BF16)
        val = _dot(onehot.astype(BF16), before) + carry_ref[...]
        dests = [(jnp.sum(jnp.where(hk, val, 0.0), axis=0, keepdims=True) + idxs[k] * float(cap)).astype(jnp.int32)
                 for k, hk in enumerate(hits)]
        total = carry_ref[...] + jnp.sum(onehot, axis=1, keepdims=True)
        carry_ref[...] = total
        cnt_ref[...] = jnp.broadcast_to(total, cnt_ref.shape).astype(jnp.int32)
        for sub in range(tm // t_dma):
            dest_ref[sub] = jnp.concatenate([dk[:, sub * t_dma:(sub + 1) * t_dma] for dk in dests], axis=1)
        dvec_ref[0] = jnp.concatenate(dests, axis=1)
        for sl in range(2):
            @pl.when(slot == sl)
            def _to_smem():
                pltpu.make_async_copy(dvec_ref, dsm_refs[sl], dsem).start()

    @pl.when(i == n_tiles)
    def _drain_all():
        _drain(sent(other), TOP_K * tm)

        @pl.when(i >= 2)
        def _():
            _drain(sent(slot), TOP_K * tm)


def _mixdisp_call(glu, glum, o, g, x2, weights, wgu, wd, *, tm, tiles_per_seq, t_dma):
    n, d = x2.shape
    n_tiles = n // tm
    cap = n
    wgu2 = wgu.reshape(-1, wgu.shape[-1])
    wd2 = wd.reshape(-1, wd.shape[-1])
    w_rows = wgu2.shape[0] // n_tiles
    assert wgu2.shape[0] == wd2.shape[0] == w_rows * n_tiles
    assert (w_rows // (tm // SUBLANES // 2)) % BF16_ROWS == 0
    cur = lambda i: jnp.minimum(i, n_tiles - 1)
    prev = lambda i: jnp.maximum(i - 1, 0)
    full = lambda a: pl.BlockSpec(a.shape, lambda i: (0,) * a.ndim)
    row = lambda c: pl.BlockSpec((tm, c), lambda i: (cur(i), 0))
    halo = pl.BlockSpec((HALO, CONV_DIM), lambda i: (jnp.maximum(cur(i) * (tm // HALO) - 1, 0), 0))
    wblk = lambda a: pl.BlockSpec((w_rows, a.shape[1]), lambda i: (prev(i), 0))
    subs = tm // t_dma
    outs = pl.pallas_call(
        functools.partial(_mixdisp_body, tm=tm, tiles_per_seq=tiles_per_seq, n_tiles=n_tiles, cap=cap, t_dma=t_dma),
        grid=(n_tiles + 1,),
        in_specs=[row(CONV_DIM), halo, full(glum), row(N_HEADS * V_HEAD_DIM), row(C_GATE), row(d)]
                 + [full(w) for w in weights] + [wblk(wgu2), wblk(wd2)],
        out_specs=[row(d), pl.BlockSpec((TOP_K, tm), lambda i: (0, cur(i))),
                   pl.BlockSpec((2 * TOP_K, tm), lambda i: (0, cur(i))),
                   pl.BlockSpec((subs, 1, TOP_K * t_dma), lambda i: (cur(i), 0, 0)),
                   pl.BlockSpec((N_EXPERTS, LANES), lambda i: (0, 0)),
                   wblk(wgu2), wblk(wd2), pl.BlockSpec(memory_space=pl.ANY)],
        out_shape=[jax.ShapeDtypeStruct((n, d), F32), jax.ShapeDtypeStruct((TOP_K, n), jnp.int32),
                   jax.ShapeDtypeStruct((2 * TOP_K, n), F32),
                   jax.ShapeDtypeStruct((n // t_dma, 1, TOP_K * t_dma), jnp.int32),
                   jax.ShapeDtypeStruct((N_EXPERTS, LANES), jnp.int32),
                   jax.ShapeDtypeStruct(wgu2.shape, BF16), jax.ShapeDtypeStruct(wd2.shape, BF16),
                   jax.ShapeDtypeStruct((N_EXPERTS * cap, ROW_CHUNKS, LANES), F32)],
        scratch_shapes=[pltpu.VMEM((HALO + tm, CONV_DIM), F32),
                        pltpu.VMEM((SUBLANES, HALO + tm, CONV_DIM), F32),
                        pltpu.VMEM((tm, CONV_DIM), F32),
                        pltpu.VMEM((2,) + _tile_view_shape(tm), F32),
                        pltpu.VMEM((1, 1, TOP_K * tm), jnp.int32),
                        pltpu.SMEM((1, 1, TOP_K * tm), jnp.int32), pltpu.SMEM((1, 1, TOP_K * tm), jnp.int32),
                        pltpu.VMEM((N_EXPERTS, 1), F32),
                        pltpu.SemaphoreType.DMA((2,)), pltpu.SemaphoreType.DMA],
        compiler_params=_cparams(("arbitrary",)),
        name="mixdisp",
    )(glu, glu, glum, o, g, x2, *weights, wgu2, wd2)
    h, idx, gates, dest, cnt, wgu_b, wd_b, xs = outs
    return h, gates, dest, cnt, wgu_b.reshape(wgu.shape), wd_b.reshape(wd.shape), xs


WAIT_GROUP = 256


def _drain(copy, count):
    def group(_, carry):
        for _ in range(WAIT_GROUP):
            copy.wait()
        return carry

    lax.fori_loop(0, count // WAIT_GROUP, group, 0)


def _plan_body(cnt_ref, bexp_ref, brow_ref, nact_ref, *, bm, n_blocks, cap_blocks):
    def per_expert(e, carry):
        blk, last = carry
        nb = (cnt_ref[e] + (bm - 1)) // bm

        def fill(b, _):
            bexp_ref[b] = e
            brow_ref[b] = e * cap_blocks + (b - blk)
            return 0

        lax.fori_loop(blk, blk + nb, fill, 0)
        return blk + nb, jnp.where(nb > 0, e, last)

    n_act, last = lax.fori_loop(0, N_EXPERTS, per_expert, (jnp.int32(0), jnp.int32(0)))
    nact_ref[0] = n_act

    def tail(b, _):
        bexp_ref[b] = last
        brow_ref[b] = 0
        return 0

    lax.fori_loop(n_act, n_blocks, tail, 0)


def _plan_call(cnt, *, bm, n_blocks, cap_blocks):
    smem = lambda: pl.BlockSpec(memory_space=pltpu.SMEM)
    blocks = jax.ShapeDtypeStruct((n_blocks,), jnp.int32)
    return pl.pallas_call(
        functools.partial(_plan_body, bm=bm, n_blocks=n_blocks, cap_blocks=cap_blocks),
        in_specs=[smem()],
        out_specs=[smem(), smem(), smem()],
        out_shape=[blocks, blocks, jax.ShapeDtypeStruct((1,), jnp.int32)],
        name="plan",
    )(cnt)


def _expert_body(bexp_ref, brow_ref, nact_ref, xs_ref, wgu_ref, bgu_ref, wd_ref, bd_ref, ys_ref,
                 xin_ref, yout_ref, in_sems, out_sems, *, bm):
    b = pl.program_id(0)
    n_act = nact_ref[0]
    slot = b % 2
    other = 1 - slot

    def chunk(c):
        return slice(c * LANES, (c + 1) * LANES)

    def block_rows(blk):
        return pl.ds(pl.multiple_of(brow_ref[blk] * bm, bm), bm)

    def fetch(blk, sl, c):
        return pltpu.make_async_copy(xs_ref.at[block_rows(blk), c], xin_ref.at[sl, :, chunk(c)], in_sems.at[sl])

    def write_back(blk, sl, c):
        return pltpu.make_async_copy(yout_ref.at[sl, :, chunk(c)], ys_ref.at[block_rows(blk), c], out_sems.at[sl])

    @pl.when(b == 0)
    def _prime():
        for c in range(ROW_CHUNKS):
            fetch(0, 0, c).start()

    @pl.when(b < n_act)
    def _block():
        for c in range(ROW_CHUNKS):
            fetch(b, slot, c).wait()

        @pl.when(b + 1 < n_act)
        def _prefetch():
            for c in range(ROW_CHUNKS):
                fetch(b + 1, other, c).start()

        @pl.when(b >= 2)
        def _free_result_slot():
            for c in range(ROW_CHUNKS):
                write_back(b - 2, slot, c).wait()

        x = xin_ref[slot].astype(BF16)
        gu = _dot(x, wgu_ref[0]) + bgu_ref[0]
        g = jnp.minimum(gu[:, :D_FF], SWIGLU_LIMIT)
        u = jnp.clip(gu[:, D_FF:], -SWIGLU_LIMIT, SWIGLU_LIMIT)
        act = (u + 1.0) * (g * _sigmoid(SWIGLU_ALPHA * g))
        yout_ref[slot] = _dot(act.astype(BF16), wd_ref[0]) + bd_ref[0]
        for c in range(ROW_CHUNKS):
            write_back(b, slot, c).start()

    @pl.when(b == n_act)
    def _drain_results():
        for c in range(ROW_CHUNKS):
            write_back(b - 1, other, c).wait()

        @pl.when(b >= 2)
        def _older():
            for c in range(ROW_CHUNKS):
                write_back(b - 2, slot, c).wait()


def _expert_call(bexp, brow, nact, xs, wgu, bgu, wd, bd, *, bm):
    d = ROW_CHUNKS * LANES
    n_blocks = bexp.shape[0]
    wsel = lambda b, be, br, na: (be[jnp.minimum(b, n_blocks - 1)], 0, 0)
    hbm = pl.BlockSpec(memory_space=pl.ANY)
    return pl.pallas_call(
        functools.partial(_expert_body, bm=bm),
        grid_spec=pltpu.PrefetchScalarGridSpec(
            num_scalar_prefetch=3, grid=(n_blocks + 1,),
            in_specs=[hbm, pl.BlockSpec((1, d, 2 * D_FF), wsel), pl.BlockSpec((1, 1, 2 * D_FF), wsel),
                      pl.BlockSpec((1, D_FF, d), wsel), pl.BlockSpec((1, 1, d), wsel)],
            out_specs=hbm,
            scratch_shapes=[pltpu.VMEM((2, bm, d), F32), pltpu.VMEM((2, bm, d), F32),
                            pltpu.SemaphoreType.DMA((2,)), pltpu.SemaphoreType.DMA((2,))]),
        out_shape=jax.ShapeDtypeStruct(xs.shape, F32),
        input_output_aliases={3: 0},
        compiler_params=_cparams(("arbitrary",)),
        name="experts",
    )(bexp, brow, nact, xs, wgu, bgu, wd, bd)


def _combine_body(dest_ref, ys_ref, gate_ref, h_ref, out_ref, buf_ref, gt_ref, sems, *, t, steps):
    i = pl.program_id(0)
    slot = i % 2
    other = 1 - slot

    def row_gather(d, k, grp, s, sl):
        return pltpu.make_async_copy(ys_ref.at[d], _row_of(buf_ref.at[sl, k], grp, s), sems.at[sl])

    @pl.when(i == 0)
    def _first():
        buf_ref[...] = jnp.zeros_like(buf_ref)

    @pl.when(i > 0)
    def _wait_previous_tile():
        _drain(row_gather(0, 0, 0, 0, other), TOP_K * t)

    gt_ref[...] = jnp.transpose(gate_ref[...])

    def body(grp, carry):
        for s in range(SUBLANES):
            for k in range(TOP_K):
                d = dest_ref[0, 0, k * t + grp * SUBLANES + s]
                row_gather(d, k, grp, s, slot).start(priority=k % 2)
        rows = pl.ds(pl.multiple_of(grp * SUBLANES, SUBLANES), SUBLANES)
        acc = h_ref[rows, :]
        g = gt_ref[rows, :]
        for k in range(TOP_K):
            tiles = buf_ref[other, k, grp]
            acc = acc + g[:, k:k + 1] * jnp.concatenate([tiles[c] for c in range(ROW_CHUNKS)], axis=1)
        out_ref[rows, :] = acc
        return carry

    lax.fori_loop(0, t // SUBLANES, body, 0)

    @pl.when(i == steps)
    def _drain_extra_gather():
        _drain(row_gather(0, 0, 0, 0, slot), TOP_K * t)


def _combine_call(dest, ys, gates, h, *, t):
    n, d = h.shape
    steps = n // t
    prev = lambda i: jnp.maximum(i - 1, 0)
    return pl.pallas_call(
        functools.partial(_combine_body, t=t, steps=steps),
        grid=(steps + 1,),
        in_specs=[pl.BlockSpec((1, 1, TOP_K * t), lambda i: (jnp.minimum(i, steps - 1), 0, 0),
                               memory_space=pltpu.SMEM),
                  pl.BlockSpec(memory_space=pl.ANY),
                  pl.BlockSpec((2 * TOP_K, t), lambda i: (0, prev(i))),
                  pl.BlockSpec((t, d), lambda i: (prev(i), 0))],
        out_specs=pl.BlockSpec((t, d), lambda i: (prev(i), 0)),
        out_shape=jax.ShapeDtypeStruct((n, d), F32),
        scratch_shapes=[pltpu.VMEM((2, TOP_K) + _tile_view_shape(t), F32), pltpu.VMEM((t, 2 * TOP_K), F32),
                        pltpu.SemaphoreType.DMA((2,))],
        compiler_params=_cparams(("arbitrary",)),
        name="combine",
    )(dest, ys, gates, h)


def _rot_half(z):
    half = QK_ROPE_DIM // 2
    return jnp.concatenate([-z[:, half:], z[:, :half]], axis=1)


def _pe_slot(z):
    rows = z.shape[0]
    return jnp.concatenate([jnp.zeros((rows, QK_NOPE_DIM), F32), z,
                            jnp.zeros((rows, LANES - QK_HEAD_DIM), F32)], axis=1)


def _prep_proj_weights(attn_norm_w, w_in, q_a_norm_w, w_q_b, kv_a_norm_w, w_kv_b, q_norm_w, k_norm_w, gate_b):
    c0 = 2 * CONV_DIM
    c1 = c0 + Q_LORA_RANK
    c2 = c1 + KV_LORA_RANK
    c3 = c2 + QK_ROPE_DIM
    kpe = w_in[:, c2:c3]
    win = jnp.concatenate([w_in[:, :c2].astype(BF16), _pe_slot(kpe).astype(BF16),
                           _pe_slot(_rot_half(kpe * k_norm_w[None, QK_NOPE_DIM:])).astype(BF16),
                           w_in[:, c3:].astype(BF16)], axis=1)

    pad = jnp.zeros((Q_LORA_RANK, LANES - QK_HEAD_DIM), F32)
    q1, q2 = [], []
    for h in range(N_HEADS):
        cols = w_q_b[:, h * QK_HEAD_DIM:(h + 1) * QK_HEAD_DIM]
        q1.append(jnp.concatenate([cols, pad], axis=1))
        q2.append(_pe_slot(_rot_half(cols[:, QK_NOPE_DIM:] * q_norm_w[None, QK_NOPE_DIM:])))
    wq = jnp.concatenate(q1 + q2, axis=1).astype(BF16)

    kpad = jnp.zeros((KV_LORA_RANK, LANES - QK_NOPE_DIM), F32)
    ks, vs = [], []
    per_head = QK_NOPE_DIM + V_HEAD_DIM
    for h in range(N_HEADS):
        cols = w_kv_b[:, h * per_head:(h + 1) * per_head]
        ks.append(jnp.concatenate([cols[:, :QK_NOPE_DIM], kpad], axis=1))
        vs.append(cols[:, QK_NOPE_DIM:])
    wk = jnp.concatenate(ks, axis=1).astype(BF16)
    wvt = jnp.concatenate(vs, axis=1).T.astype(BF16)

    lane_pad = jnp.zeros((LANES - QK_HEAD_DIM,), F32)
    qlane = jnp.concatenate([q_norm_w, lane_pad])[None, :]
    klane = jnp.concatenate([k_norm_w, lane_pad])[None, :]
    return (attn_norm_w[None, :], win, q_a_norm_w[None, :], wq, kv_a_norm_w[None, :], wk, wvt, qlane, klane,
            gate_b[None, :])


def _rope_tables(length):
    half = QK_ROPE_DIM // 2
    inv_freq = ROPE_THETA ** (-jnp.arange(half, dtype=F32) / half)
    ang = jnp.arange(length, dtype=F32)[:, None] * inv_freq[None, :]
    cos, sin = jnp.cos(ang), jnp.sin(ang)
    ones = jnp.ones((length, QK_NOPE_DIM), F32)
    tail = LANES - QK_HEAD_DIM
    cos_t = jnp.concatenate([ones, cos, cos, jnp.ones((length, tail), F32)], axis=1)
    sin_t = jnp.concatenate([0.0 * ones, sin, sin, jnp.zeros((length, tail), F32)], axis=1)
    return cos_t, sin_t


def _tile(n, pref):
    t = pref
    while n % t:
        t //= 2
    return t


def kernel(x, meta_tokens, attn_norm_w, w_in, conv_dw_w, conv_dw_b, conv_ln_w, conv_ln_b, conv_pw2_w, q_a_norm_w, w_q_b, kv_a_norm_w, w_kv_b, q_norm_w, k_norm_w, w_o_mla, gate_b, w_out, ffn_norm_w, router_w, router_b, w_gate_up, b_gate_up, w_down, b_down):
    assert attn_norm_w.shape[0] == 1, "one layer: rows of meta tokens never feed a later layer"
    b, s, d = x.shape
    n = b * s
    x2 = x.reshape(n, d)

    tm = _tile(s, MIX_TILE)
    tq = _tile(s, 512)
    cos_t, sin_t = _rope_tables(N_META + s)
    pw = _prep_proj_weights(attn_norm_w[0], w_in[0], q_a_norm_w[0], w_q_b[0], kv_a_norm_w[0], w_kv_b[0],
                            q_norm_w[0], k_norm_w[0], gate_b[0])

    tp = _tile(s, 512)
    glu, q, k, vt, g = _proj_call(x2, (cos_t[N_META:], sin_t[N_META:]), pw, tm=tp, tiles_per_seq=s // tp)
    glum, _, km, vmt, _ = _proj_call(meta_tokens.astype(F32), (cos_t[:N_META], sin_t[:N_META]), pw, tm=N_META,
                                     tiles_per_seq=1)

    o = _attn_call(q, k, vt, km, vmt, tq=tq, batch=b)

    rw = router_w[0].T
    rw_hi = rw.astype(BF16)
    rw_lo = (rw - rw_hi.astype(F32)).astype(BF16)
    mix_w = (conv_dw_w[0], conv_dw_b[0][None, :], conv_ln_w[0][None, :], conv_ln_b[0][None, :],
             conv_pw2_w[0].astype(BF16), w_o_mla[0].astype(BF16), w_out[0].astype(BF16), ffn_norm_w[0][None, :],
             rw_hi, rw_lo, router_b[0][:, None])
    t_dma = _tile(tm, 256)
    h, gates, dest, cnt, wgu_b, wd_b, xs = _mixdisp_call(glu, glum, o, g, x2, mix_w, w_gate_up[0], w_down[0], tm=tm,
                                                         tiles_per_seq=s // tm, t_dma=t_dma)

    bm = 512
    n_blocks = (n * TOP_K) // bm + N_EXPERTS
    bexp, brow, nact = _plan_call(cnt[:, 0], bm=bm, n_blocks=n_blocks, cap_blocks=n // bm)
    ys = _expert_call(bexp, brow, nact, xs, wgu_b, b_gate_up[0][:, None, :], wd_b, b_down[0][:, None, :], bm=bm)
    out = _combine_call(dest, ys, gates, h, t=t_dma)
    return out.reshape(b, s, d)
```
